```python
import math
import jax, jax.numpy as jnp
from jax import lax
import numpy as np


D_MODEL = 1024
BATCH = 8
SEQ = 2048
DEPTH = 2
DEC_BATCH = 128
DEC_SEQ = 8
PAST_LEN = 16384
PAGE_SIZE = 128

N_META = 16
H_A = 4
DK_A = 128
DV_A = 128
D_QK_A = H_A * DK_A
D_A = H_A * DV_A
D_QKV_A = 2 * D_QK_A + D_A
CONV_A = 4
H_B = 8
P_B = 64
D_B = H_B * P_B
G_B = 2
N_B = 128
D_XBC = D_B + 2 * G_B * N_B
CONV_B = 4
D_C = 512
CONV_C = 3
D_MIX = D_A + D_B + D_C
CHUNK = 64
PROJ_SPLITS = (D_QKV_A, D_A, H_A, H_A, D_XBC, D_B, H_B, D_C, D_C, D_C)
D_PROJ = D_QKV_A + D_A + 2 * H_A + D_XBC + D_B + H_B + 3 * D_C
N_EGROUPS = 4
EXP_PER_GROUP = 4
N_EXPERTS = N_EGROUPS * EXP_PER_GROUP
TOP_K = 2
D_FF_E = 256
EPS = 1e-6

kernel_name = "hymba_gdn_ssd_shortconv_hmoe_step"


def rmsnorm(x, w):
    xf = x.astype(jnp.float32)
    y = xf * lax.rsqrt(jnp.mean(xf * xf, axis=-1, keepdims=True) + EPS)
    return (y * w.astype(jnp.float32)).astype(x.dtype)


def l2norm(x):
    return x * lax.rsqrt(jnp.sum(x * x, axis=-1, keepdims=True) + EPS)


def causal_conv(u, hist, w):
    width, L = w.shape[0], u.shape[1]
    up = jnp.concatenate([hist.astype(u.dtype), u], axis=1)
    y = up[:, 0:L] * w[0]
    for i in range(1, width):
        y = y + up[:, i:i + L] * w[i]
    return y, up[:, L:]


def chunk_len(L):
    return CHUNK if L % CHUNK == 0 else L


def to_chunks(t, c):
    b, L, h = t.shape[:3]
    return t.reshape(b, L // c, c, h, *t.shape[3:]).swapaxes(2, 3)


def from_chunks(t):
    b, n, h, c = t.shape[:4]
    return t.swapaxes(2, 3).reshape(b, n * c, h, *t.shape[4:])


def decay_matrix(G):
    c = G.shape[-1]
    mask = jnp.tril(jnp.ones((c, c), bool))
    return jnp.exp(jnp.where(mask, G[..., :, None] - G[..., None, :], -jnp.inf))


def gdn_chunked(q, k, v, beta, g, S0):
    c = chunk_len(q.shape[1])
    q, k, v, beta, g = (to_chunks(t, c) for t in (q, k, v, beta, g))
    G = jnp.cumsum(g, axis=-1)
    dec = decay_matrix(G)
    strict = jnp.tril(jnp.ones((c, c), q.dtype), -1)
    kb = k * beta[..., None]
    A = jnp.eye(c, dtype=q.dtype) + jnp.einsum('bnhid,bnhjd->bnhij', kb, k) * dec * strict
    rhs = jnp.concatenate([v * beta[..., None], kb * jnp.exp(G)[..., None]], axis=-1)
    sol = lax.linalg.triangular_solve(A, rhs, left_side=True, lower=True, unit_diagonal=True)
    dv = v.shape[-1]
    U, Wk = sol[..., :dv], sol[..., dv:]
    qk = jnp.einsum('bnhid,bnhjd->bnhij', q, k) * dec
    qg = q * jnp.exp(G)[..., None]
    g_last = G[..., -1]
    kd = k * jnp.exp(g_last[..., None] - G)[..., None]

    def step(S, xs):
        u, wk, qg_c, qk_c, kd_c, gl = xs
        w = u - jnp.einsum('bhck,bhkv->bhcv', wk, S)
        o = jnp.einsum('bhck,bhkv->bhcv', qg_c, S) + jnp.einsum('bhij,bhjv->bhiv', qk_c, w)
        S = jnp.exp(gl)[..., None, None] * S + jnp.einsum('bhck,bhcv->bhkv', kd_c, w)
        return S, o

    xs = tuple(jnp.moveaxis(t, 1, 0) for t in (U, Wk, qg, qk, kd, g_last))
    S, o = lax.scan(step, S0, xs)
    return from_chunks(jnp.moveaxis(o, 0, 1)), S


def ssd_chunked(xdt, g, Bh, Ch, h0):
    c = chunk_len(xdt.shape[1])
    xdt, g, Bh, Ch = (to_chunks(t, c) for t in (xdt, g, Bh, Ch))
    G = jnp.cumsum(g, axis=-1)
    dec = decay_matrix(G)
    cb = jnp.einsum('bnhis,bnhjs->bnhij', Ch, Bh) * dec
    y_diag = jnp.einsum('bnhij,bnhjp->bnhip', cb, xdt)
    g_last = G[..., -1]
    dS = jnp.einsum('bnhjs,bnhjp->bnhps', Bh * jnp.exp(g_last[..., None] - G)[..., None], xdt)

    def step(h, xs):
        ds, gl = xs
        return jnp.exp(gl)[..., None, None] * h + ds, h

    hT, h_starts = lax.scan(step, h0, (jnp.moveaxis(dS, 1, 0), jnp.moveaxis(g_last, 1, 0)))
    h_starts = jnp.moveaxis(h_starts, 0, 1)
    y = y_diag + jnp.einsum('bnhis,bnhps->bnhip', Ch * jnp.exp(G)[..., None], h_starts)
    return from_chunks(y), hT


def run_segments(core, seq_args, state, seg_lens):
    outs, start = [], 0
    for L in seg_lens:
        o, state = core(*(a[:, start:start + L] for a in seq_args), state)
        outs.append(o)
        start += L
    return jnp.concatenate(outs, axis=1), state


def token_mixers(hn, hist_a, hist_b, hist_c, s_a, s_b, seg_lens, w_in, conv_a_w, a_log_a,
                 dt_bias_a, onorm_a_w, conv_b_w, conv_b_b, a_log_b, dt_bias_b, d_skip_b,
                 onorm_b_w, conv_c_w, w_out):
    f32 = jnp.float32
    bsz, L, _ = hn.shape
    proj = jnp.einsum('bld,dp->blp', hn, w_in)
    qkv_a, z_a, b_a, a_a, xbc_b, z_b, dt_b, h_c, c_c, b_c = jnp.split(
        proj, np.cumsum(PROJ_SPLITS)[:-1].tolist(), axis=-1)

    qkv, new_hist_a = causal_conv(qkv_a, hist_a, conv_a_w)
    qkv = jax.nn.silu(qkv).astype(f32)
    q, k, v = jnp.split(qkv, [D_QK_A, 2 * D_QK_A], axis=-1)
    q = l2norm(q.reshape(bsz, L, H_A, DK_A)) * (DK_A ** -0.5)
    k = l2norm(k.reshape(bsz, L, H_A, DK_A))
    v = v.reshape(bsz, L, H_A, DV_A)
    beta = jax.nn.sigmoid(b_a.astype(f32))
    g_a = -jnp.exp(a_log_a.astype(f32)) * jax.nn.softplus(a_a.astype(f32) + dt_bias_a.astype(f32))
    o_a, s_a = run_segments(gdn_chunked, (q, k, v, beta, g_a), s_a.astype(f32), seg_lens)
    o_a = rmsnorm(o_a, onorm_a_w) * jax.nn.silu(z_a.astype(f32).reshape(bsz, L, H_A, DV_A))
    o_a = o_a.reshape(bsz, L, D_A).astype(hn.dtype)

    xbc, new_hist_b = causal_conv(xbc_b, hist_b, conv_b_w)
    xbc = jax.nn.silu(xbc + conv_b_b).astype(f32)
    xs, Bm, Cm = jnp.split(xbc, [D_B, D_B + G_B * N_B], axis=-1)
    xs = xs.reshape(bsz, L, H_B, P_B)
    Bh = jnp.repeat(Bm.reshape(bsz, L, G_B, N_B), H_B // G_B, axis=2)
    Ch = jnp.repeat(Cm.reshape(bsz, L, G_B, N_B), H_B // G_B, axis=2)
    dt = jax.nn.softplus(dt_b.astype(f32) + dt_bias_b.astype(f32))
    g_b = -jnp.exp(a_log_b.astype(f32)) * dt
    y_b, s_b = run_segments(ssd_chunked, (xs * dt[..., None], g_b, Bh, Ch), s_b.astype(f32), seg_lens)
    y_b = (y_b + d_skip_b.astype(f32)[:, None] * xs).reshape(bsz, L, D_B) * jax.nn.silu(z_b.astype(f32))
    y_b = rmsnorm(y_b.reshape(bsz, L, G_B, D_B // G_B), onorm_b_w.reshape(G_B, D_B // G_B))
    y_b = y_b.reshape(bsz, L, D_B).astype(hn.dtype)

    y_c, new_hist_c = causal_conv(c_c * h_c, hist_c, conv_c_w)
    y_c = b_c * y_c

    mixed = jnp.concatenate([o_a, y_b, y_c], axis=-1)
    out = jnp.einsum('blm,md->bld', mixed, w_out)
    return out, new_hist_a, new_hist_b, new_hist_c, s_a, s_b


def hier_moe(h, w_router_g, w_router_e, w_gate, w_up, w_down):
    f32 = jnp.float32
    p_grp = jax.nn.softmax(jnp.einsum('bld,dg->blg', h, w_router_g).astype(f32), axis=-1)
    g_idx = jnp.argmax(p_grp, axis=-1)
    g_hot = jax.nn.one_hot(g_idx, N_EGROUPS, dtype=f32)
    p_top = jnp.sum(p_grp * g_hot, axis=-1)
    le = jnp.einsum('bld,de->ble', h, w_router_e).astype(f32)
    le = le.reshape(*le.shape[:2], N_EGROUPS, EXP_PER_GROUP)
    le_sel = jnp.einsum('blge,blg->ble', le, g_hot)
    top_v, top_i = lax.top_k(jax.nn.softmax(le_sel, axis=-1), TOP_K)
    top_v = top_v / jnp.sum(top_v, axis=-1, keepdims=True)
    e_idx = g_idx[..., None] * EXP_PER_GROUP + top_i
    comb = jnp.einsum('blke,blk->ble', jax.nn.one_hot(e_idx, N_EXPERTS, dtype=f32),
                      top_v * p_top[..., None])
    hg = jnp.einsum('bld,edf->blef', h, w_gate)
    hu = jnp.einsum('bld,edf->blef', h, w_up)
    act = jax.nn.silu(hg) * hu * comb.astype(h.dtype)[..., None]
    return jnp.einsum('blef,efd->bld', act, w_down)


def trunk_layer(x, hist_a, hist_b, hist_c, s_a, s_b, seg_lens, mix_w, moe_w, norm1_w, norm2_w):
    m, hist_a, hist_b, hist_c, s_a, s_b = token_mixers(
        rmsnorm(x, norm1_w), hist_a, hist_b, hist_c, s_a, s_b, seg_lens, *mix_w)
    x = x + m
    x = x + hier_moe(rmsnorm(x, norm2_w), *moe_w)
    return x, (s_a.astype(x.dtype), hist_a, s_b.astype(x.dtype), hist_b, hist_c)


def stack_layers(states, i):
    return jnp.stack([s[i] for s in states], axis=0)


def setup_inputs(seed: int = 0) -> dict:
    key = jax.random.key(seed)
    ks = iter(jax.random.split(key, 40))
    f32 = jnp.float32

    def nrm(shape, scale):
        return jax.random.normal(next(ks), shape, f32) * scale

    def gain(shape):
        return 1.0 + nrm(shape, 0.02)

    def a_log(shape):
        return jnp.log(jax.random.uniform(next(ks), shape, f32, 1.0, 16.0))

    def dt_bias(shape):
        dt = jnp.exp(jax.random.uniform(next(ks), shape, f32, math.log(1e-3), math.log(1e-1)))
        return dt + jnp.log(-jnp.expm1(-dt))

    return {
        'x_prompt': nrm((BATCH, SEQ, D_MODEL), 1.0),
        'x_sample': nrm((DEC_BATCH, DEC_SEQ, D_MODEL), 1.0),
        'state_gdn': nrm((DEPTH, DEC_BATCH, H_A, DK_A, DV_A), 0.5),
        'state_gdn_conv': nrm((DEPTH, DEC_BATCH, CONV_A - 1, D_QKV_A), 1.0),
        'state_ssd': nrm((DEPTH, DEC_BATCH, H_B, P_B, N_B), 0.5),
        'state_ssd_conv': nrm((DEPTH, DEC_BATCH, CONV_B - 1, D_XBC), 1.0),
        'state_sconv': nrm((DEPTH, DEC_BATCH, CONV_C - 1, D_C), 1.0),
        'meta_tokens': nrm((N_META, D_MODEL), 1.0),
        'norm1_w': gain((DEPTH, D_MODEL)),
        'w_in': nrm((DEPTH, D_MODEL, D_PROJ), D_MODEL ** -0.5),
        'conv_a_w': nrm((DEPTH, CONV_A, D_QKV_A), CONV_A ** -0.5),
        'a_log_a': a_log((DEPTH, H_A)),
        'dt_bias_a': dt_bias((DEPTH, H_A)),
        'onorm_a_w': gain((DEPTH, DV_A)),
        'conv_b_w': nrm((DEPTH, CONV_B, D_XBC), CONV_B ** -0.5),
        'conv_b_b': nrm((DEPTH, D_XBC), 0.02),
        'a_log_b': a_log((DEPTH, H_B)),
        'dt_bias_b': dt_bias((DEPTH, H_B)),
        'd_skip_b': gain((DEPTH, H_B)),
        'onorm_b_w': gain((DEPTH, D_B)),
        'conv_c_w': nrm((DEPTH, CONV_C, D_C), CONV_C ** -0.5),
        'w_out': nrm((DEPTH, D_MIX, D_MODEL), D_MIX ** -0.5),
        'norm2_w': gain((DEPTH, D_MODEL)),
        'w_router_g': nrm((DEPTH, D_MODEL, N_EGROUPS), D_MODEL ** -0.5),
        'w_router_e': nrm((DEPTH, D_MODEL, N_EXPERTS), D_MODEL ** -0.5),
        'w_gate': nrm((DEPTH, N_EXPERTS, D_MODEL, D_FF_E), D_MODEL ** -0.5),
        'w_up': nrm((DEPTH, N_EXPERTS, D_MODEL, D_FF_E), D_MODEL ** -0.5),
        'w_down': nrm((DEPTH, N_EXPERTS, D_FF_E, D_MODEL), D_FF_E ** -0.5),
        'final_norm_w': gain((D_MODEL,)),
    }


def reference(x_prompt, x_sample, state_gdn, state_gdn_conv, state_ssd, state_ssd_conv, state_sconv,
              meta_tokens, norm1_w, w_in, conv_a_w, a_log_a, dt_bias_a, onorm_a_w, conv_b_w, conv_b_b,
              a_log_b, dt_bias_b, d_skip_b, onorm_b_w, conv_c_w, w_out, norm2_w, w_router_g,
              w_router_e, w_gate, w_up, w_down, final_norm_w):
    bp = x_prompt.shape[0]
    dtp = x_prompt.dtype
    meta = jnp.broadcast_to(meta_tokens.astype(dtp)[None], (bp, N_META, x_prompt.shape[2]))
    x_p = jnp.concatenate([meta, x_prompt], axis=1)
    x_s = x_sample
    seg_p = (N_META, x_prompt.shape[1])
    seg_s = (x_sample.shape[1],)
    new_p, new_s = [], []
    for l in range(DEPTH):
        mix_w = (w_in[l], conv_a_w[l], a_log_a[l], dt_bias_a[l], onorm_a_w[l], conv_b_w[l], conv_b_b[l],
                 a_log_b[l], dt_bias_b[l], d_skip_b[l], onorm_b_w[l], conv_c_w[l], w_out[l])
        moe_w = (w_router_g[l], w_router_e[l], w_gate[l], w_up[l], w_down[l])
        x_p, st_p = trunk_layer(
            x_p, jnp.zeros((bp, CONV_A - 1, D_QKV_A), dtp), jnp.zeros((bp, CONV_B - 1, D_XBC), dtp),
            jnp.zeros((bp, CONV_C - 1, D_C), dtp), jnp.zeros((bp, H_A, DK_A, DV_A), jnp.float32),
            jnp.zeros((bp, H_B, P_B, N_B), jnp.float32), seg_p, mix_w, moe_w, norm1_w[l], norm2_w[l])
        x_s, st_s = trunk_layer(
            x_s, state_gdn_conv[l], state_ssd_conv[l], state_sconv[l], state_gdn[l], state_ssd[l],
            seg_s, mix_w, moe_w, norm1_w[l], norm2_w[l])
        new_p.append(st_p)
        new_s.append(st_s)
    y_prompt = rmsnorm(x_p, final_norm_w)[:, N_META:]
    y_sample = rmsnorm(x_s, final_norm_w)
    return (y_prompt, y_sample,
            stack_layers(new_p, 0), stack_layers(new_p, 1), stack_layers(new_p, 2),
            stack_layers(new_p, 3), stack_layers(new_p, 4),
            stack_layers(new_s, 0), stack_layers(new_s, 1), stack_layers(new_s, 2),
            stack_layers(new_s, 3), stack_layers(new_s, 4))
```

```python
import functools

import jax
import jax.numpy as jnp
from jax import lax
from jax.experimental import pallas as pl
from jax.experimental.pallas import tpu as pltpu

F32 = jnp.float32
BF16 = jnp.bfloat16

D_MODEL = 1024
N_META = 16
H_A = 4
DK_A = 128
DV_A = 128
D_QK_A = H_A * DK_A
D_A = H_A * DV_A
D_QKV_A = 2 * D_QK_A + D_A
CONV_A = 4
H_B = 8
P_B = 64
D_B = H_B * P_B
G_B = 2
N_B = 128
D_XBC = D_B + 2 * G_B * N_B
CONV_B = 4
D_C = 512
CONV_C = 3
D_MIX = D_A + D_B + D_C
N_EGROUPS = 4
EXP_PER_GROUP = 4
N_EXPERTS = N_EGROUPS * EXP_PER_GROUP
D_FF_E = 256
EPS = 1e-6
CHUNK = 64

D_HCB = 3 * D_C
D_SMALL = 128
PROJ_WIDTHS = (D_QKV_A, D_A, D_XBC, D_B, D_HCB, D_SMALL)
D_PROJ_PACKED = sum(PROJ_WIDTHS)
LANE_BETA = 0
LANE_GA = H_A
LANE_GB = 2 * H_A
HIST_ROWS = 8

VMEM_LIMIT_BYTES = 56 * 1024 * 1024


def _rmsnorm(x, w):
    return x * lax.rsqrt(jnp.mean(x * x, axis=-1, keepdims=True) + EPS) * w


def _silu(x):
    return x * jax.nn.sigmoid(x)


def _softplus(x):
    return jnp.maximum(x, 0.0) + jnp.log1p(jnp.exp(-jnp.abs(x)))


def _dot(a, b):
    return jnp.dot(a.astype(BF16), b.astype(BF16), preferred_element_type=F32)


def _dot_nt(a, b):
    return lax.dot_general(a.astype(BF16), b.astype(BF16), (((1,), (1,)), ((), ())),
                           preferred_element_type=F32)


def _dot_tn(a, b):
    return lax.dot_general(a.astype(BF16), b.astype(BF16), (((0,), (0,)), ((), ())),
                           preferred_element_type=F32)


def _split2(a):
    hi = a.astype(BF16)
    lo = (a - hi.astype(F32)).astype(BF16)
    return hi, lo


def _dot3(a, b):
    ah, al = _split2(a)
    bh, bl = _split2(b)
    d = functools.partial(jnp.dot, preferred_element_type=F32)
    return d(ah, bh) + d(ah, bl) + d(al, bh)


def _cumsum_rows(g, tril_b):
    g1 = g.astype(BF16)
    r1 = g - g1.astype(F32)
    g2 = r1.astype(BF16)
    g3 = (r1 - g2.astype(F32)).astype(BF16)
    d = functools.partial(jnp.dot, preferred_element_type=F32)
    return d(tril_b, g1) + d(tril_b, g2) + d(tril_b, g3)


def _unit_lower_inverse(nmat, row, col, C):
    eye = jnp.where(row == col, 1.0, 0.0).astype(F32)
    d = eye - jnp.where((row >> 1) == (col >> 1), nmat, 0.0)
    s, ls = 2, 1
    while s < C:
        same_2s = (row >> (ls + 1)) == (col >> (ls + 1))
        same_s = (row >> ls) == (col >> ls)
        e = jnp.where(same_2s, jnp.where(same_s, 0.0, nmat), 0.0)
        d = d - _dot3(_dot3(d, e), d)
        s, ls = 2 * s, ls + 1
    return d


def _proj_kernel(x_ref, nw_ref, w_ref, *out_refs):
    hn = _rmsnorm(x_ref[...], nw_ref[...]).astype(BF16)
    off = 0
    for o_ref, width in zip(out_refs, PROJ_WIDTHS):
        o_ref[...] = jnp.dot(hn, w_ref[:, off:off + width], preferred_element_type=F32)
        off += width


def _proj_call(x, norm_w, w_packed, tm):
    T = x.shape[0]
    const = lambda i: (0, 0)
    return pl.pallas_call(
        _proj_kernel,
        grid=(T // tm,),
        in_specs=[
            pl.BlockSpec((tm, D_MODEL), lambda i: (i, 0)),
            pl.BlockSpec((1, D_MODEL), const),
            pl.BlockSpec((D_MODEL, D_PROJ_PACKED), const, pipeline_mode=pl.Buffered(1)),
        ],
        out_specs=[pl.BlockSpec((tm, w), lambda i: (i, 0)) for w in PROJ_WIDTHS],
        out_shape=[jax.ShapeDtypeStruct((T, w), F32) for w in PROJ_WIDTHS],
        compiler_params=pltpu.CompilerParams(
            dimension_semantics=("parallel",), vmem_limit_bytes=VMEM_LIMIT_BYTES),
        name="proj",
    )(x, norm_w.reshape(1, D_MODEL), w_packed)


def _causal_conv(ext_ref, cur, w_ref, width, C):
    ext_ref[HIST_ROWS:HIST_ROWS + C, :] = cur
    y = None
    for i in range(width):
        off = HIST_ROWS - (width - 1) + i
        term = ext_ref[off:off + C, :] * w_ref[i:i + 1, :]
        y = term if y is None else y + term
    return y


def _mix_kernel(qkv_ref, za_ref, xbc_ref, zb_ref, hcb_ref, small_ref,
                hista_ref, histb_ref, histc_ref, s0a_ref, s0b_ref,
                cwa_ref, cwb_ref, cbb_ref, cwc_ref, hp_ref, ona_ref, onb_ref,
                mixed_ref, sa_out, sb_out, ha_out, hb_out, hc_out,
                exta, extb, extc, sa, sb, *, C, pad, nC):
    c = pl.program_id(1)

    @pl.when(c == 0)
    def _():
        exta[0:HIST_ROWS, :] = hista_ref[...]
        extb[0:HIST_ROWS, :] = histb_ref[...]
        extc[0:HIST_ROWS, :] = histc_ref[...]
        sa[...] = s0a_ref[...]
        sb[...] = s0b_ref[...]

    row = lax.broadcasted_iota(jnp.int32, (C, C), 0)
    col = lax.broadcasted_iota(jnp.int32, (C, C), 1)
    tril = row >= col
    strict = row > col
    tril_b = jnp.where(tril, 1.0, 0.0).astype(BF16)

    if pad:
        tok = c * C + lax.broadcasted_iota(jnp.int32, (C, 1), 0)
        valid = tok >= pad
        keep = lambda v: jnp.where(valid, v, 0.0)
    else:
        keep = lambda v: v

    small = small_ref[...]
    sp = keep(_softplus(small + hp_ref[0:1, :]))
    g_all = -jnp.exp(hp_ref[1:2, :]) * sp
    beta_all = jax.nn.sigmoid(small)
    G_all = _cumsum_rows(g_all, tril_b)
    GT_all = G_all.T
    eG_all = jnp.exp(G_all)
    G_last = G_all[C - 1:C, :]
    eG_last = jnp.exp(G_last)
    eG_rem = jnp.exp(G_last - G_all)

    def decay_matrix(lane):
        dlog = G_all[:, lane:lane + 1] - GT_all[lane:lane + 1, :]
        return jnp.where(tril, jnp.exp(jnp.minimum(dlog, 0.0)), 0.0)

    qkv = _silu(_causal_conv(exta, keep(qkv_ref[...]), cwa_ref, CONV_A, C))
    za = za_ref[...]
    for h in range(H_A):
        lane = LANE_GA + h
        q = qkv[:, h * DK_A:(h + 1) * DK_A]
        k = qkv[:, D_QK_A + h * DK_A:D_QK_A + (h + 1) * DK_A]
        v = qkv[:, 2 * D_QK_A + h * DV_A:2 * D_QK_A + (h + 1) * DV_A]
        q = q * lax.rsqrt(jnp.sum(q * q, axis=-1, keepdims=True) + EPS) * (DK_A ** -0.5)
        k = k * lax.rsqrt(jnp.sum(k * k, axis=-1, keepdims=True) + EPS)
        beta = beta_all[:, LANE_BETA + h:LANE_BETA + h + 1]
        dec = decay_matrix(lane)
        kb = k * beta
        kq = _dot_nt(jnp.concatenate([kb, q], axis=0), k)
        nmat = jnp.where(strict, kq[:C] * dec, 0.0)
        qk = kq[C:] * dec
        tinv = _unit_lower_inverse(nmat, row, col, C)
        eg = eG_all[:, lane:lane + 1]
        rhs = jnp.concatenate([v * beta, kb * eg], axis=1)
        sol = _dot3(tinv, rhs)
        u, wk = sol[:, :DV_A], sol[:, DV_A:]
        s_old = sa[h]
        ws = _dot(jnp.concatenate([wk, q * eg], axis=0), s_old)
        w = u - ws[:C]
        o = ws[C:] + _dot(qk, w)
        kd = k * eG_rem[:, lane:lane + 1]
        sa[h] = eG_last[:, lane:lane + 1] * s_old + _dot_tn(kd, w)
        o = _rmsnorm(o, ona_ref[...]) * _silu(za[:, h * DV_A:(h + 1) * DV_A])
        mixed_ref[:, h * DV_A:(h + 1) * DV_A] = o

    xbc = _silu(_causal_conv(extb, keep(xbc_ref[...]), cwb_ref, CONV_B, C) + cbb_ref[...])
    zb = zb_ref[...]
    heads_per_group = H_B // G_B
    ys = []
    for grp in range(G_B):
        bg = xbc[:, D_B + grp * N_B:D_B + (grp + 1) * N_B]
        cg = xbc[:, D_B + G_B * N_B + grp * N_B:D_B + G_B * N_B + (grp + 1) * N_B]
        cbt = _dot_nt(cg, bg)
        for hh in range(heads_per_group):
            h = grp * heads_per_group + hh
            lane = LANE_GB + h
            x_h = xbc[:, h * P_B:(h + 1) * P_B]
            xdt = x_h * sp[:, lane:lane + 1]
            y = _dot(cbt * decay_matrix(lane), xdt)
            h_old = sb[h]
            y = y + _dot_nt(cg * eG_all[:, lane:lane + 1], h_old)
            sb[h] = eG_last[:, lane:lane + 1] * h_old + _dot_tn(xdt, bg * eG_rem[:, lane:lane + 1])
            ys.append(y + hp_ref[2:3, lane:lane + 1] * x_h)
    gw = D_B // G_B
    for grp in range(G_B):
        yg = jnp.concatenate(ys[grp * heads_per_group:(grp + 1) * heads_per_group], axis=1)
        yg = yg * _silu(zb[:, grp * gw:(grp + 1) * gw])
        mixed_ref[:, D_A + grp * gw:D_A + (grp + 1) * gw] = _rmsnorm(yg, onb_ref[:, grp * gw:(grp + 1) * gw])

    hcb = hcb_ref[...]
    u_c = keep(hcb[:, D_C:2 * D_C] * hcb[:, 0:D_C])
    y_c = _causal_conv(extc, u_c, cwc_ref, CONV_C, C)
    mixed_ref[:, D_A + D_B:D_MIX] = hcb[:, 2 * D_C:3 * D_C] * y_c

    @pl.when(c == nC - 1)
    def _():
        sa_out[...] = sa[...]
        sb_out[...] = sb[...]
        ha_out[...] = exta[HIST_ROWS + C - (CONV_A - 1):HIST_ROWS + C, :]
        hb_out[...] = extb[HIST_ROWS + C - (CONV_B - 1):HIST_ROWS + C, :]
        hc_out[...] = extc[HIST_ROWS + C - (CONV_C - 1):HIST_ROWS + C, :]

    exta[0:HIST_ROWS, :] = exta[C:C + HIST_ROWS, :]
    extb[0:HIST_ROWS, :] = extb[C:C + HIST_ROWS, :]
    extc[0:HIST_ROWS, :] = extc[C:C + HIST_ROWS, :]


def _mix_call(proj, hists, s0a, s0b, layer, weights, *, B, nC, C, pad):
    T = B * nC * C
    cwa, cwb, cbb, cwc, hp, ona, onb = weights
    tok = lambda b, c: (b * nC + c, 0)
    per_b3 = lambda b, c: (b, 0, 0)
    const = lambda b, c: (0, 0)
    in_specs = [pl.BlockSpec((C, w), tok) for w in PROJ_WIDTHS]
    in_specs += [pl.BlockSpec((None, HIST_ROWS, w), per_b3) for w in (D_QKV_A, D_XBC, D_C)]
    in_specs += [
        pl.BlockSpec((None, None, H_A, DK_A, DV_A), lambda b, c: (layer, b, 0, 0, 0)),
        pl.BlockSpec((None, None, H_B, P_B, N_B), lambda b, c: (layer, b, 0, 0, 0)),
        pl.BlockSpec((CONV_A, D_QKV_A), const),
        pl.BlockSpec((CONV_B, D_XBC), const),
        pl.BlockSpec((1, D_XBC), const),
        pl.BlockSpec((CONV_C, D_C), const),
        pl.BlockSpec((8, D_SMALL), const),
        pl.BlockSpec((1, DV_A), const),
        pl.BlockSpec((1, D_B), const),
    ]
    out_specs = [
        pl.BlockSpec((C, D_MIX), tok),
        pl.BlockSpec((None, H_A, DK_A, DV_A), lambda b, c: (b, 0, 0, 0)),
        pl.BlockSpec((None, H_B, P_B, N_B), lambda b, c: (b, 0, 0, 0)),
        pl.BlockSpec((None, CONV_A - 1, D_QKV_A), per_b3),
        pl.BlockSpec((None, CONV_B - 1, D_XBC), per_b3),
        pl.BlockSpec((None, CONV_C - 1, D_C), per_b3),
    ]
    out_shape = [
        jax.ShapeDtypeStruct((T, D_MIX), F32),
        jax.ShapeDtypeStruct((B, H_A, DK_A, DV_A), F32),
        jax.ShapeDtypeStruct((B, H_B, P_B, N_B), F32),
        jax.ShapeDtypeStruct((B, CONV_A - 1, D_QKV_A), F32),
        jax.ShapeDtypeStruct((B, CONV_B - 1, D_XBC), F32),
        jax.ShapeDtypeStruct((B, CONV_C - 1, D_C), F32),
    ]
    scratch = [
        pltpu.VMEM((C + HIST_ROWS, D_QKV_A), F32),
        pltpu.VMEM((C + HIST_ROWS, D_XBC), F32),
        pltpu.VMEM((C + HIST_ROWS, D_C), F32),
        pltpu.VMEM((H_A, DK_A, DV_A), F32),
        pltpu.VMEM((H_B, P_B, N_B), F32),
    ]
    return pl.pallas_call(
        functools.partial(_mix_kernel, C=C, pad=pad, nC=nC),
        grid=(B, nC),
        in_specs=in_specs,
        out_specs=out_specs,
        out_shape=out_shape,
        scratch_shapes=scratch,
        compiler_params=pltpu.CompilerParams(
            dimension_semantics=("parallel", "arbitrary"), vmem_limit_bytes=VMEM_LIMIT_BYTES),
        name="mix",
    )(*proj, *hists, s0a, s0b, cwa, cwb, cbb, cwc, hp, ona, onb)


def _route(logits):
    lg = [logits[:, g:g + 1] for g in range(N_EGROUPS)]
    m = functools.reduce(jnp.maximum, lg)
    ex = [jnp.exp(l - m) for l in lg]
    den = functools.reduce(lambda a, b: a + b, ex)
    p = [e / den for e in ex]
    p_top, g_idx = p[0], jnp.zeros_like(p[0], dtype=jnp.int32)
    for g in range(1, N_EGROUPS):
        better = p[g] > p_top
        g_idx = jnp.where(better, g, g_idx)
        p_top = jnp.where(better, p[g], p_top)
    g_hot = [g_idx == g for g in range(N_EGROUPS)]
    le_sel = []
    for j in range(EXP_PER_GROUP):
        acc = None
        for g in range(N_EGROUPS):
            lane = N_EGROUPS + g * EXP_PER_GROUP + j
            t = jnp.where(g_hot[g], logits[:, lane:lane + 1], 0.0)
            acc = t if acc is None else acc + t
        le_sel.append(acc)
    m2 = functools.reduce(jnp.maximum, le_sel)
    ex2 = [jnp.exp(l - m2) for l in le_sel]
    den2 = functools.reduce(lambda a, b: a + b, ex2)
    s = [e / den2 for e in ex2]
    v1, i1 = s[0], jnp.zeros_like(g_idx)
    for j in range(1, EXP_PER_GROUP):
        better = s[j] > v1
        i1 = jnp.where(better, j, i1)
        v1 = jnp.where(better, s[j], v1)
    v2, i2 = jnp.full_like(v1, -1.0), jnp.zeros_like(g_idx)
    for j in range(EXP_PER_GROUP):
        s_rest = jnp.where(i1 == j, -1.0, s[j])
        better = s_rest > v2
        i2 = jnp.where(better, j, i2)
        v2 = jnp.where(better, s_rest, v2)
    tot = v1 + v2
    w1 = v1 / tot * p_top
    w2 = v2 / tot * p_top
    comb = []
    for g in range(N_EGROUPS):
        for j in range(EXP_PER_GROUP):
            inner = jnp.where(i1 == j, w1, jnp.where(i2 == j, w2, 0.0))
            comb.append(jnp.where(g_hot[g], inner, 0.0))
    return comb


def _post_kernel(x_ref, mixed_ref, wout_ref, n2_ref, wr_ref, wg_ref, wu_ref, wd_ref, fn_ref,
                 o_ref, *, final):
    x1 = x_ref[...] + jnp.dot(mixed_ref[...].astype(BF16), wout_ref[...], preferred_element_type=F32)
    hb = _rmsnorm(x1, n2_ref[...]).astype(BF16)
    comb = _route(jnp.dot(hb, wr_ref[...], preferred_element_type=F32))
    acc = x1
    for e in range(N_EXPERTS):
        hg = jnp.dot(hb, wg_ref[e], preferred_element_type=F32)
        hu = jnp.dot(hb, wu_ref[e], preferred_element_type=F32)
        act = _silu(hg) * hu * comb[e]
        acc = acc + jnp.dot(act.astype(BF16), wd_ref[e], preferred_element_type=F32)
    if final:
        acc = _rmsnorm(acc, fn_ref[...])
    o_ref[...] = acc


def _post_call(x, mixed, w_out, n2, w_router, w_gate, w_up, w_down, fn, *, tm, final):
    T = x.shape[0]
    const2 = lambda i: (0, 0)
    const3 = lambda i: (0, 0, 0)
    single = dict(pipeline_mode=pl.Buffered(1))
    return pl.pallas_call(
        functools.partial(_post_kernel, final=final),
        grid=(T // tm,),
        in_specs=[
            pl.BlockSpec((tm, D_MODEL), lambda i: (i, 0)),
            pl.BlockSpec((tm, D_MIX), lambda i: (i, 0)),
            pl.BlockSpec((D_MIX, D_MODEL), const2, **single),
            pl.BlockSpec((1, D_MODEL), const2),
            pl.BlockSpec((D_MODEL, D_SMALL), const2, **single),
            pl.BlockSpec((N_EXPERTS, D_MODEL, D_FF_E), const3, **single),
            pl.BlockSpec((N_EXPERTS, D_MODEL, D_FF_E), const3, **single),
            pl.BlockSpec((N_EXPERTS, D_FF_E, D_MODEL), const3, **single),
            pl.BlockSpec((1, D_MODEL), const2),
        ],
        out_specs=pl.BlockSpec((tm, D_MODEL), lambda i: (i, 0)),
        out_shape=jax.ShapeDtypeStruct((T, D_MODEL), F32),
        compiler_params=pltpu.CompilerParams(
            dimension_semantics=("parallel",), vmem_limit_bytes=VMEM_LIMIT_BYTES),
        name="post",
    )(x, mixed, w_out, n2, w_router, w_gate, w_up, w_down, fn)


def _pack_w_in(w):
    o = [0]
    for s in (D_QKV_A, D_A, H_A, H_A, D_XBC, D_B, H_B, D_C, D_C, D_C):
        o.append(o[-1] + s)
    cols = [w[:, o[0]:o[2]], w[:, o[4]:o[6]], w[:, o[7]:o[10]], w[:, o[2]:o[4]], w[:, o[6]:o[7]],
            jnp.zeros((D_MODEL, D_SMALL - 2 * H_A - H_B), w.dtype)]
    return jnp.concatenate(cols, axis=1).astype(BF16)


def _head_params(dt_bias_a, dt_bias_b, a_log_a, a_log_b, d_skip_b):
    z4 = jnp.zeros((H_A,), F32)
    tail = jnp.zeros((D_SMALL - 2 * H_A - H_B,), F32)
    rows = [
        jnp.concatenate([z4, dt_bias_a, dt_bias_b, tail]),
        jnp.concatenate([z4, a_log_a, a_log_b, tail]),
        jnp.concatenate([z4, z4, d_skip_b, tail]),
    ]
    return jnp.concatenate([jnp.stack(rows), jnp.zeros((8 - len(rows), D_SMALL), F32)], axis=0)


def _pad_hist(h):
    return jnp.pad(h, ((0, 0), (0, 0), (HIST_ROWS - h.shape[2], 0), (0, 0)))


def kernel(x_prompt, x_sample, state_gdn, state_gdn_conv, state_ssd, state_ssd_conv, state_sconv,
           meta_tokens, norm1_w, w_in, conv_a_w, a_log_a, dt_bias_a, onorm_a_w, conv_b_w, conv_b_b,
           a_log_b, dt_bias_b, d_skip_b, onorm_b_w, conv_c_w, w_out, norm2_w, w_router_g,
           w_router_e, w_gate, w_up, w_down, final_norm_w):
    bp, seq, _ = x_prompt.shape
    bs, dseq, _ = x_sample.shape
    depth = w_in.shape[0]
    pad = CHUNK - N_META
    lp = pad + N_META + seq
    ncp = lp // CHUNK

    meta = jnp.broadcast_to(meta_tokens[None], (bp, N_META, D_MODEL))
    x_p = jnp.concatenate([jnp.zeros((bp, pad, D_MODEL), F32), meta, x_prompt], axis=1)
    x_p = x_p.reshape(bp * lp, D_MODEL)
    x_s = x_sample.reshape(bs * dseq, D_MODEL)

    zero_hists = [jnp.zeros((bp, HIST_ROWS, w), F32) for w in (D_QKV_A, D_XBC, D_C)]
    zero_sa = jnp.zeros((1, bp, H_A, DK_A, DV_A), F32)
    zero_sb = jnp.zeros((1, bp, H_B, P_B, N_B), F32)
    hists_s = [_pad_hist(state_gdn_conv), _pad_hist(state_ssd_conv), _pad_hist(state_sconv)]

    fn = final_norm_w.reshape(1, D_MODEL)
    new_p, new_s = [], []
    for l in range(depth):
        w_packed = _pack_w_in(w_in[l])
        mix_w = (conv_a_w[l], conv_b_w[l], conv_b_b[l].reshape(1, D_XBC), conv_c_w[l],
                 _head_params(dt_bias_a[l], dt_bias_b[l], a_log_a[l], a_log_b[l], d_skip_b[l]),
                 onorm_a_w[l].reshape(1, DV_A), onorm_b_w[l].reshape(1, D_B))
        w_router = jnp.concatenate(
            [w_router_g[l], w_router_e[l],
             jnp.zeros((D_MODEL, D_SMALL - N_EGROUPS - N_EXPERTS), F32)], axis=1).astype(BF16)
        post_w = (w_out[l].astype(BF16), norm2_w[l].reshape(1, D_MODEL), w_router,
                  w_gate[l].astype(BF16), w_up[l].astype(BF16), w_down[l].astype(BF16), fn)
        final = l == depth - 1

        proj_p = _proj_call(x_p, norm1_w[l], w_packed, tm=256)
        mixed_p, *st_p = _mix_call(proj_p, zero_hists, zero_sa, zero_sb, 0, mix_w,
                                   B=bp, nC=ncp, C=CHUNK, pad=pad)
        x_p = _post_call(x_p, mixed_p, *post_w, tm=256, final=final)

        proj_s = _proj_call(x_s, norm1_w[l], w_packed, tm=256)
        mixed_s, *st_s = _mix_call(proj_s, [h[l] for h in hists_s], state_gdn, state_ssd, l, mix_w,
                                   B=bs, nC=1, C=dseq, pad=0)
        x_s = _post_call(x_s, mixed_s, *post_w, tm=256, final=final)
        new_p.append(st_p)
        new_s.append(st_s)

    y_prompt = x_p.reshape(bp, lp, D_MODEL)[:, pad + N_META:]
    y_sample = x_s.reshape(bs, dseq, D_MODEL)
    stack = lambda states, i: jnp.stack([s[i] for s in states], axis=0)
    return (y_prompt, y_sample,
            stack(new_p, 0), stack(new_p, 2), stack(new_p, 1), stack(new_p, 3), stack(new_p, 4),
            stack(new_s, 0), stack(new_s, 2), stack(new_s, 1), stack(new_s, 3), stack(new_s, 4))
```

```python
import functools

import jax
import jax.numpy as jnp
from jax import lax
from jax.experimental import pallas as pl
from jax.experimental.pallas import tpu as pltpu

F32 = jnp.float32
BF16 = jnp.bfloat16

D_MODEL = 1024
N_META = 16
H_A = 4
DK_A = 128
DV_A = 128
D_QK_A = H_A * DK_A
D_A = H_A * DV_A
D_QKV_A = 2 * D_QK_A + D_A
CONV_A = 4
H_B = 8
P_B = 64
D_B = H_B * P_B
G_B = 2
N_B = 128
D_XBC = D_B + 2 * G_B * N_B
CONV_B = 4
D_C = 512
CONV_C = 3
D_MIX = D_A + D_B + D_C
N_EGROUPS = 4
EXP_PER_GROUP = 4
N_EXPERTS = N_EGROUPS * EXP_PER_GROUP
D_FF_E = 256
EPS = 1e-6
CHUNK = 64

D_HCB = 3 * D_C
D_SMALL = 128
PROJ_WIDTHS = (D_QKV_A, D_A, D_XBC, D_B, D_HCB, D_SMALL)
D_PROJ_PACKED = sum(PROJ_WIDTHS)
LANE_BETA = 0
LANE_GA = H_A
LANE_GB = 2 * H_A
HIST_ROWS = 8
G_PROMPT = 2
G_SAMPLE = 8

VMEM_LIMIT_BYTES = 56 * 1024 * 1024


def _rmsnorm(x, w):
    return x * lax.rsqrt(jnp.mean(x * x, axis=-1, keepdims=True) + EPS) * w


def _silu(x):
    return x * jax.nn.sigmoid(x)


def _softplus(x):
    return jnp.maximum(x, 0.0) + jnp.log1p(jnp.exp(-jnp.abs(x)))


def _dot(a, b):
    return jnp.dot(a.astype(BF16), b.astype(BF16), preferred_element_type=F32)


def _dot_nt(a, b):
    return lax.dot_general(a.astype(BF16), b.astype(BF16), (((1,), (1,)), ((), ())),
                           preferred_element_type=F32)


def _dot_tn(a, b):
    return lax.dot_general(a.astype(BF16), b.astype(BF16), (((0,), (0,)), ((), ())),
                           preferred_element_type=F32)


def _cumsum_rows(g, tril_b):
    g1 = g.astype(BF16)
    r1 = g - g1.astype(F32)
    g2 = r1.astype(BF16)
    g3 = (r1 - g2.astype(F32)).astype(BF16)
    d = functools.partial(jnp.dot, preferred_element_type=F32)
    return d(tril_b, g1) + d(tril_b, g2) + d(tril_b, g3)


def _unit_lower_inverse_offdiag(nmats, row, col, C):
    ds = [-jnp.where((row >> 1) == (col >> 1), n, 0.0) for n in nmats]
    s, ls = 2, 1
    while s < C:
        same_2s = (row >> (ls + 1)) == (col >> (ls + 1))
        same_s = (row >> ls) == (col >> ls)
        es = [jnp.where(same_2s, jnp.where(same_s, 0.0, n), 0.0) for n in nmats]
        ys = [e + _dot(d, e) for d, e in zip(ds, es)]
        ds = [d - (y + _dot(y, d)) for d, y in zip(ds, ys)]
        s, ls = 2 * s, ls + 1
    return ds


def _proj_kernel(x_ref, nw_ref, w_ref, *out_refs):
    hn = _rmsnorm(x_ref[...], nw_ref[...]).astype(BF16)
    off = 0
    for o_ref, width in zip(out_refs, PROJ_WIDTHS):
        o_ref[...] = jnp.dot(hn, w_ref[:, off:off + width], preferred_element_type=F32)
        off += width


def _proj_call(x, norm_w, w_packed, tm):
    T = x.shape[0]
    const = lambda i: (0, 0)
    return pl.pallas_call(
        _proj_kernel,
        grid=(T // tm,),
        in_specs=[
            pl.BlockSpec((tm, D_MODEL), lambda i: (i, 0)),
            pl.BlockSpec((1, D_MODEL), const),
            pl.BlockSpec((D_MODEL, D_PROJ_PACKED), const, pipeline_mode=pl.Buffered(1)),
        ],
        out_specs=[pl.BlockSpec((tm, w), lambda i: (i, 0)) for w in PROJ_WIDTHS],
        out_shape=[jax.ShapeDtypeStruct((T, w), F32) for w in PROJ_WIDTHS],
        compiler_params=pltpu.CompilerParams(
            dimension_semantics=("parallel",), vmem_limit_bytes=VMEM_LIMIT_BYTES),
        name="proj",
    )(x, norm_w.reshape(1, D_MODEL), w_packed)


def _causal_conv(ext_ref, cur, w_ref, width, C):
    ext_ref[HIST_ROWS:HIST_ROWS + C, :] = cur
    y = None
    for i in range(width):
        off = HIST_ROWS - (width - 1) + i
        term = ext_ref[off:off + C, :] * w_ref[i:i + 1, :]
        y = term if y is None else y + term
    return y


def _mix_kernel(qkv_ref, za_ref, xbc_ref, zb_ref, hcb_ref, small_ref,
                hista_ref, histb_ref, histc_ref, s0a_ref, s0b_ref,
                cwa_ref, cwb_ref, cbb_ref, cwc_ref, hp_ref, ona_ref, onb_ref,
                mixed_ref, sa_out, sb_out, ha_out, hb_out, hc_out,
                exta, extb, extc, sa, sb, *, G, C, pad, nC):
    c = pl.program_id(1)

    @pl.when(c == 0)
    def _():
        exta[:, 0:HIST_ROWS, :] = hista_ref[...]
        extb[:, 0:HIST_ROWS, :] = histb_ref[...]
        extc[:, 0:HIST_ROWS, :] = histc_ref[...]
        sa[...] = s0a_ref[...]
        sb[...] = s0b_ref[...]

    row = lax.broadcasted_iota(jnp.int32, (C, C), 0)
    col = lax.broadcasted_iota(jnp.int32, (C, C), 1)
    tril = row >= col
    strict = row > col
    tril_b = jnp.where(tril, 1.0, 0.0).astype(BF16)

    if pad:
        tok = c * C + lax.broadcasted_iota(jnp.int32, (C, 1), 0)
        valid = tok >= pad
        keep = lambda v: jnp.where(valid, v, 0.0)
    else:
        keep = lambda v: v

    seqs = range(G)
    lane_col = lambda a, lane: a[:, lane:lane + 1]

    small = [small_ref[g] for g in seqs]
    sp = [keep(_softplus(s + hp_ref[0:1, :])) for s in small]
    beta_all = [jax.nn.sigmoid(s) for s in small]
    G_all = [_cumsum_rows(-jnp.exp(hp_ref[1:2, :]) * s, tril_b) for s in sp]
    GT_all = [x.T for x in G_all]
    eG_all = [jnp.exp(x) for x in G_all]
    eG_last = [jnp.exp(x[C - 1:C, :]) for x in G_all]
    eG_rem = [jnp.exp(x[C - 1:C, :] - x) for x in G_all]

    def decay_matrix(g, lane):
        dlog = lane_col(G_all[g], lane) - GT_all[g][lane:lane + 1, :]
        return jnp.where(tril, jnp.exp(jnp.minimum(dlog, 0.0)), 0.0)

    qkv = [_silu(_causal_conv(exta.at[g], keep(qkv_ref[g]), cwa_ref, CONV_A, C)) for g in seqs]
    chains = [(g, h) for g in seqs for h in range(H_A)]
    qs, ks, vs, kbs, egs = [], [], [], [], []
    for g, h in chains:
        q = qkv[g][:, h * DK_A:(h + 1) * DK_A]
        k = qkv[g][:, D_QK_A + h * DK_A:D_QK_A + (h + 1) * DK_A]
        v = qkv[g][:, 2 * D_QK_A + h * DV_A:2 * D_QK_A + (h + 1) * DV_A]
        q = q * lax.rsqrt(jnp.sum(q * q, axis=-1, keepdims=True) + EPS) * (DK_A ** -0.5)
        k = k * lax.rsqrt(jnp.sum(k * k, axis=-1, keepdims=True) + EPS)
        beta = lane_col(beta_all[g], LANE_BETA + h)
        qs.append(q)
        ks.append(k)
        vs.append(v * beta)
        kbs.append(k * beta)
        egs.append(lane_col(eG_all[g], LANE_GA + h))
    kqs = [_dot_nt(jnp.concatenate([kb, q], axis=0), k) for kb, q, k in zip(kbs, qs, ks)]
    decs = [decay_matrix(g, LANE_GA + h) for g, h in chains]
    nmats = [jnp.where(strict, kq[:C] * dec, 0.0) for kq, dec in zip(kqs, decs)]
    qks = [kq[C:] * dec for kq, dec in zip(kqs, decs)]
    tinvs = _unit_lower_inverse_offdiag(nmats, row, col, C)
    rhss = [jnp.concatenate([vb, kb * eg], axis=1) for vb, kb, eg in zip(vs, kbs, egs)]
    sols = [rhs + _dot(t, rhs) for t, rhs in zip(tinvs, rhss)]
    s_olds = [sa[g, h] for g, h in chains]
    wss = [_dot(jnp.concatenate([sol[:, DV_A:], q * eg], axis=0), s_old)
           for sol, q, eg, s_old in zip(sols, qs, egs, s_olds)]
    ws_ = [sol[:, :DV_A] - ws[:C] for sol, ws in zip(sols, wss)]
    os_ = [ws[C:] + _dot(qk, w) for ws, qk, w in zip(wss, qks, ws_)]
    for (g, h), k, w, s_old, o in zip(chains, ks, ws_, s_olds, os_):
        lane = LANE_GA + h
        sa[g, h] = lane_col(eG_last[g], lane) * s_old + _dot_tn(k * lane_col(eG_rem[g], lane), w)
        gate = _silu(za_ref[g, :, h * DV_A:(h + 1) * DV_A])
        mixed_ref[g, :, h * DV_A:(h + 1) * DV_A] = _rmsnorm(o, ona_ref[...]) * gate

    heads_per_group = H_B // G_B
    gw = D_B // G_B
    xbc = [_silu(_causal_conv(extb.at[g], keep(xbc_ref[g]), cwb_ref, CONV_B, C) + cbb_ref[...])
           for g in seqs]
    bgs = {(g, grp): xbc[g][:, D_B + grp * N_B:D_B + (grp + 1) * N_B] for g in seqs for grp in range(G_B)}
    cgs = {(g, grp): xbc[g][:, D_B + (G_B + grp) * N_B:D_B + (G_B + grp + 1) * N_B]
           for g in seqs for grp in range(G_B)}
    cbts = {key: _dot_nt(cgs[key], bgs[key]) for key in bgs}
    heads = [(g, h) for g in seqs for h in range(H_B)]
    x_hs = [xbc[g][:, h * P_B:(h + 1) * P_B] for g, h in heads]
    xdts = [x_h * lane_col(sp[g], LANE_GB + h) for (g, h), x_h in zip(heads, x_hs)]
    h_olds = [sb[g, h] for g, h in heads]
    y_offs = [_dot_nt(cgs[(g, h // heads_per_group)] * lane_col(eG_all[g], LANE_GB + h), h_old)
              for (g, h), h_old in zip(heads, h_olds)]
    for (g, h), xdt, h_old in zip(heads, xdts, h_olds):
        lane = LANE_GB + h
        bdec = bgs[(g, h // heads_per_group)] * lane_col(eG_rem[g], lane)
        sb[g, h] = lane_col(eG_last[g], lane) * h_old + _dot_tn(xdt, bdec)
    ys = [y_off + _dot(cbts[(g, h // heads_per_group)] * decay_matrix(g, LANE_GB + h), xdt)
          + hp_ref[2:3, LANE_GB + h:LANE_GB + h + 1] * x_h
          for (g, h), y_off, xdt, x_h in zip(heads, y_offs, xdts, x_hs)]
    for g in seqs:
        for grp in range(G_B):
            lo = g * H_B + grp * heads_per_group
            yg = jnp.concatenate(ys[lo:lo + heads_per_group], axis=1)
            yg = yg * _silu(zb_ref[g, :, grp * gw:(grp + 1) * gw])
            mixed_ref[g, :, D_A + grp * gw:D_A + (grp + 1) * gw] = _rmsnorm(
                yg, onb_ref[:, grp * gw:(grp + 1) * gw])

    for g in seqs:
        hcb = hcb_ref[g]
        u_c = keep(hcb[:, D_C:2 * D_C] * hcb[:, 0:D_C])
        y_c = _causal_conv(extc.at[g], u_c, cwc_ref, CONV_C, C)
        mixed_ref[g, :, D_A + D_B:D_MIX] = hcb[:, 2 * D_C:3 * D_C] * y_c

    @pl.when(c == nC - 1)
    def _():
        sa_out[...] = sa[...]
        sb_out[...] = sb[...]
        ha_out[...] = exta[:, HIST_ROWS + C - (CONV_A - 1):HIST_ROWS + C, :]
        hb_out[...] = extb[:, HIST_ROWS + C - (CONV_B - 1):HIST_ROWS + C, :]
        hc_out[...] = extc[:, HIST_ROWS + C - (CONV_C - 1):HIST_ROWS + C, :]

    exta[:, 0:HIST_ROWS, :] = exta[:, C:C + HIST_ROWS, :]
    extb[:, 0:HIST_ROWS, :] = extb[:, C:C + HIST_ROWS, :]
    extc[:, 0:HIST_ROWS, :] = extc[:, C:C + HIST_ROWS, :]


def _mix_call(proj, hists, s0a, s0b, layer, weights, *, B, nC, C, G, pad):
    L = nC * C
    cwa, cwb, cbb, cwc, hp, ona, onb = weights
    tok = lambda b, c: (b, c, 0)
    per_b3 = lambda b, c: (b, 0, 0)
    per_b4 = lambda b, c: (b, 0, 0, 0)
    const = lambda b, c: (0, 0)
    in_specs = [pl.BlockSpec((G, C, w), tok) for w in PROJ_WIDTHS]
    in_specs += [pl.BlockSpec((G, HIST_ROWS, w), per_b3) for w in (D_QKV_A, D_XBC, D_C)]
    in_specs += [
        pl.BlockSpec((None, G, H_A, DK_A, DV_A), lambda b, c: (layer, b, 0, 0, 0)),
        pl.BlockSpec((None, G, H_B, P_B, N_B), lambda b, c: (layer, b, 0, 0, 0)),
        pl.BlockSpec((CONV_A, D_QKV_A), const),
        pl.BlockSpec((CONV_B, D_XBC), const),
        pl.BlockSpec((1, D_XBC), const),
        pl.BlockSpec((CONV_C, D_C), const),
        pl.BlockSpec((8, D_SMALL), const),
        pl.BlockSpec((1, DV_A), const),
        pl.BlockSpec((1, D_B), const),
    ]
    out_specs = [
        pl.BlockSpec((G, C, D_MIX), tok),
        pl.BlockSpec((G, H_A, DK_A, DV_A), per_b4),
        pl.BlockSpec((G, H_B, P_B, N_B), per_b4),
        pl.BlockSpec((G, CONV_A - 1, D_QKV_A), per_b3),
        pl.BlockSpec((G, CONV_B - 1, D_XBC), per_b3),
        pl.BlockSpec((G, CONV_C - 1, D_C), per_b3),
    ]
    out_shape = [
        jax.ShapeDtypeStruct((B, L, D_MIX), F32),
        jax.ShapeDtypeStruct((B, H_A, DK_A, DV_A), F32),
        jax.ShapeDtypeStruct((B, H_B, P_B, N_B), F32),
        jax.ShapeDtypeStruct((B, CONV_A - 1, D_QKV_A), F32),
        jax.ShapeDtypeStruct((B, CONV_B - 1, D_XBC), F32),
        jax.ShapeDtypeStruct((B, CONV_C - 1, D_C), F32),
    ]
    scratch = [
        pltpu.VMEM((G, C + HIST_ROWS, D_QKV_A), F32),
        pltpu.VMEM((G, C + HIST_ROWS, D_XBC), F32),
        pltpu.VMEM((G, C + HIST_ROWS, D_C), F32),
        pltpu.VMEM((G, H_A, DK_A, DV_A), F32),
        pltpu.VMEM((G, H_B, P_B, N_B), F32),
    ]
    proj3 = [p.reshape(B, L, p.shape[-1]) for p in proj]
    outs = pl.pallas_call(
        functools.partial(_mix_kernel, G=G, C=C, pad=pad, nC=nC),
        grid=(B // G, nC),
        in_specs=in_specs,
        out_specs=out_specs,
        out_shape=out_shape,
        scratch_shapes=scratch,
        compiler_params=pltpu.CompilerParams(
            dimension_semantics=("parallel", "arbitrary"), vmem_limit_bytes=VMEM_LIMIT_BYTES),
        name="mix",
    )(*proj3, *hists, s0a, s0b, cwa, cwb, cbb, cwc, hp, ona, onb)
    return [outs[0].reshape(B * L, D_MIX)] + list(outs[1:])


def _route(logits):
    lg = [logits[:, g:g + 1] for g in range(N_EGROUPS)]
    m = functools.reduce(jnp.maximum, lg)
    ex = [jnp.exp(l - m) for l in lg]
    den = functools.reduce(lambda a, b: a + b, ex)
    p = [e / den for e in ex]
    p_top, g_idx = p[0], jnp.zeros_like(p[0], dtype=jnp.int32)
    for g in range(1, N_EGROUPS):
        better = p[g] > p_top
        g_idx = jnp.where(better, g, g_idx)
        p_top = jnp.where(better, p[g], p_top)
    g_hot = [g_idx == g for g in range(N_EGROUPS)]
    le_sel = []
    for j in range(EXP_PER_GROUP):
        acc = None
        for g in range(N_EGROUPS):
            lane = N_EGROUPS + g * EXP_PER_GROUP + j
            t = jnp.where(g_hot[g], logits[:, lane:lane + 1], 0.0)
            acc = t if acc is None else acc + t
        le_sel.append(acc)
    m2 = functools.reduce(jnp.maximum, le_sel)
    ex2 = [jnp.exp(l - m2) for l in le_sel]
    den2 = functools.reduce(lambda a, b: a + b, ex2)
    s = [e / den2 for e in ex2]
    v1, i1 = s[0], jnp.zeros_like(g_idx)
    for j in range(1, EXP_PER_GROUP):
        better = s[j] > v1
        i1 = jnp.where(better, j, i1)
        v1 = jnp.where(better, s[j], v1)
    v2, i2 = jnp.full_like(v1, -1.0), jnp.zeros_like(g_idx)
    for j in range(EXP_PER_GROUP):
        s_rest = jnp.where(i1 == j, -1.0, s[j])
        better = s_rest > v2
        i2 = jnp.where(better, j, i2)
        v2 = jnp.where(better, s_rest, v2)
    tot = v1 + v2
    w1 = v1 / tot * p_top
    w2 = v2 / tot * p_top
    comb = []
    for g in range(N_EGROUPS):
        for j in range(EXP_PER_GROUP):
            inner = jnp.where(i1 == j, w1, jnp.where(i2 == j, w2, 0.0))
            comb.append(jnp.where(g_hot[g], inner, 0.0))
    return comb


def _post_kernel(x_ref, mixed_ref, wout_ref, n2_ref, wr_ref, wg_ref, wu_ref, wd_ref, fn_ref,
                 o_ref, *, final):
    x1 = x_ref[...] + jnp.dot(mixed_ref[...].astype(BF16), wout_ref[...], preferred_element_type=F32)
    hb = _rmsnorm(x1, n2_ref[...]).astype(BF16)
    comb = _route(jnp.dot(hb, wr_ref[...], preferred_element_type=F32))
    acc = x1
    for e in range(N_EXPERTS):
        hg = jnp.dot(hb, wg_ref[e], preferred_element_type=F32)
        hu = jnp.dot(hb, wu_ref[e], preferred_element_type=F32)
        act = _silu(hg) * hu * comb[e]
        acc = acc + jnp.dot(act.astype(BF16), wd_ref[e], preferred_element_type=F32)
    if final:
        acc = _rmsnorm(acc, fn_ref[...])
    o_ref[...] = acc


def _post_call(x, mixed, w_out, n2, w_router, w_gate, w_up, w_down, fn, *, tm, final):
    T = x.shape[0]
    const2 = lambda i: (0, 0)
    const3 = lambda i: (0, 0, 0)
    single = dict(pipeline_mode=pl.Buffered(1))
    return pl.pallas_call(
        functools.partial(_post_kernel, final=final),
        grid=(T // tm,),
        in_specs=[
            pl.BlockSpec((tm, D_MODEL), lambda i: (i, 0)),
            pl.BlockSpec((tm, D_MIX), lambda i: (i, 0)),
            pl.BlockSpec((D_MIX, D_MODEL), const2, **single),
            pl.BlockSpec((1, D_MODEL), const2),
            pl.BlockSpec((D_MODEL, D_SMALL), const2, **single),
            pl.BlockSpec((N_EXPERTS, D_MODEL, D_FF_E), const3, **single),
            pl.BlockSpec((N_EXPERTS, D_MODEL, D_FF_E), const3, **single),
            pl.BlockSpec((N_EXPERTS, D_FF_E, D_MODEL), const3, **single),
            pl.BlockSpec((1, D_MODEL), const2),
        ],
        out_specs=pl.BlockSpec((tm, D_MODEL), lambda i: (i, 0)),
        out_shape=jax.ShapeDtypeStruct((T, D_MODEL), F32),
        compiler_params=pltpu.CompilerParams(
            dimension_semantics=("parallel",), vmem_limit_bytes=VMEM_LIMIT_BYTES),
        name="post",
    )(x, mixed, w_out, n2, w_router, w_gate, w_up, w_down, fn)


def _pack_w_in(w):
    o = [0]
    for s in (D_QKV_A, D_A, H_A, H_A, D_XBC, D_B, H_B, D_C, D_C, D_C):
        o.append(o[-1] + s)
    cols = [w[:, o[0]:o[2]], w[:, o[4]:o[6]], w[:, o[7]:o[10]], w[:, o[2]:o[4]], w[:, o[6]:o[7]],
            jnp.zeros((D_MODEL, D_SMALL - 2 * H_A - H_B), w.dtype)]
    return jnp.concatenate(cols, axis=1).astype(BF16)


def _head_params(dt_bias_a, dt_bias_b, a_log_a, a_log_b, d_skip_b):
    z4 = jnp.zeros((H_A,), F32)
    tail = jnp.zeros((D_SMALL - 2 * H_A - H_B,), F32)
    rows = [
        jnp.concatenate([z4, dt_bias_a, dt_bias_b, tail]),
        jnp.concatenate([z4, a_log_a, a_log_b, tail]),
        jnp.concatenate([z4, z4, d_skip_b, tail]),
    ]
    return jnp.concatenate([jnp.stack(rows), jnp.zeros((8 - len(rows), D_SMALL), F32)], axis=0)


def _pad_hist(h):
    return jnp.pad(h, ((0, 0), (0, 0), (HIST_ROWS - h.shape[2], 0), (0, 0)))


def kernel(x_prompt, x_sample, state_gdn, state_gdn_conv, state_ssd, state_ssd_conv, state_sconv,
           meta_tokens, norm1_w, w_in, conv_a_w, a_log_a, dt_bias_a, onorm_a_w, conv_b_w, conv_b_b,
           a_log_b, dt_bias_b, d_skip_b, onorm_b_w, conv_c_w, w_out, norm2_w, w_router_g,
           w_router_e, w_gate, w_up, w_down, final_norm_w):
    bp, seq, _ = x_prompt.shape
    bs, dseq, _ = x_sample.shape
    depth = w_in.shape[0]
    pad = CHUNK - N_META
    lp = pad + N_META + seq
    ncp = lp // CHUNK

    meta = jnp.broadcast_to(meta_tokens[None], (bp, N_META, D_MODEL))
    x_p = jnp.concatenate([jnp.zeros((bp, pad, D_MODEL), F32), meta, x_prompt], axis=1)
    x_p = x_p.reshape(bp * lp, D_MODEL)
    x_s = x_sample.reshape(bs * dseq, D_MODEL)

    zero_hists = [jnp.zeros((bp, HIST_ROWS, w), F32) for w in (D_QKV_A, D_XBC, D_C)]
    zero_sa = jnp.zeros((1, bp, H_A, DK_A, DV_A), F32)
    zero_sb = jnp.zeros((1, bp, H_B, P_B, N_B), F32)
    hists_s = [_pad_hist(state_gdn_conv), _pad_hist(state_ssd_conv), _pad_hist(state_sconv)]

    fn = final_norm_w.reshape(1, D_MODEL)
    new_p, new_s = [], []
    for l in range(depth):
        w_packed = _pack_w_in(w_in[l])
        mix_w = (conv_a_w[l], conv_b_w[l], conv_b_b[l].reshape(1, D_XBC), conv_c_w[l],
                 _head_params(dt_bias_a[l], dt_bias_b[l], a_log_a[l], a_log_b[l], d_skip_b[l]),
                 onorm_a_w[l].reshape(1, DV_A), onorm_b_w[l].reshape(1, D_B))
        w_router = jnp.concatenate(
            [w_router_g[l], w_router_e[l],
             jnp.zeros((D_MODEL, D_SMALL - N_EGROUPS - N_EXPERTS), F32)], axis=1).astype(BF16)
        post_w = (w_out[l].astype(BF16), norm2_w[l].reshape(1, D_MODEL), w_router,
                  w_gate[l].astype(BF16), w_up[l].astype(BF16), w_down[l].astype(BF16), fn)
        final = l == depth - 1

        proj_p = _proj_call(x_p, norm1_w[l], w_packed, tm=256)
        mixed_p, *st_p = _mix_call(proj_p, zero_hists, zero_sa, zero_sb, 0, mix_w,
                                   B=bp, nC=ncp, C=CHUNK, G=G_PROMPT, pad=pad)
        x_p = _post_call(x_p, mixed_p, *post_w, tm=256, final=final)

        proj_s = _proj_call(x_s, norm1_w[l], w_packed, tm=256)
        mixed_s, *st_s = _mix_call(proj_s, [h[l] for h in hists_s], state_gdn, state_ssd, l, mix_w,
                                   B=bs, nC=1, C=dseq, G=G_SAMPLE, pad=0)
        x_s = _post_call(x_s, mixed_s, *post_w, tm=256, final=final)
        new_p.append(st_p)
        new_s.append(st_s)

    y_prompt = x_p.reshape(bp, lp, D_MODEL)[:, pad + N_META:]
    y_sample = x_s.reshape(bs, dseq, D_MODEL)
    stack = lambda states, i: jnp.stack([s[i] for s in states], axis=0)
    return (y_prompt, y_sample,
            stack(new_p, 0), stack(new_p, 2), stack(new_p, 1), stack(new_p, 3), stack(new_p, 4),
            stack(new_s, 0), stack(new_s, 2), stack(new_s, 1), stack(new_s, 3), stack(new_s, 4))
```

```python
import functools

import jax
import jax.numpy as jnp
from jax import lax
from jax.experimental import pallas as pl
from jax.experimental.pallas import tpu as pltpu

F32 = jnp.float32
BF16 = jnp.bfloat16

D_MODEL = 1024
N_META = 16
H_A = 4
DK_A = 128
DV_A = 128
D_QK_A = H_A * DK_A
D_A = H_A * DV_A
D_QKV_A = 2 * D_QK_A + D_A
CONV_A = 4
H_B = 8
P_B = 64
D_B = H_B * P_B
G_B = 2
N_B = 128
D_XBC = D_B + 2 * G_B * N_B
CONV_B = 4
D_C = 512
CONV_C = 3
D_MIX = D_A + D_B + D_C
N_EGROUPS = 4
EXP_PER_GROUP = 4
N_EXPERTS = N_EGROUPS * EXP_PER_GROUP
D_FF_E = 256
EPS = 1e-6
CHUNK = 64

D_HCB = 3 * D_C
D_SMALL = 128
PROJ_WIDTHS = (D_QKV_A, D_A, D_XBC, D_B, D_HCB, D_SMALL)
D_PROJ_PACKED = sum(PROJ_WIDTHS)
LANE_BETA = 0
LANE_GA = H_A
LANE_GB = 2 * H_A
HIST_ROWS = 8
G_PROMPT = 2
G_SAMPLE = 8
EXPERT_BATCH = 4
TM_PROJ = 512
TM_POST = 512

VMEM_LIMIT_BYTES = 56 * 1024 * 1024


def _rmsnorm(x, w):
    return x * lax.rsqrt(jnp.mean(x * x, axis=-1, keepdims=True) + EPS) * w


def _silu(x):
    return x * jax.nn.sigmoid(x)


def _softplus(x):
    return jnp.maximum(x, 0.0) + jnp.log1p(jnp.exp(-jnp.abs(x)))


def _dot(a, b):
    return jnp.dot(a.astype(BF16), b.astype(BF16), preferred_element_type=F32)


def _dot_nt(a, b):
    return lax.dot_general(a.astype(BF16), b.astype(BF16), (((1,), (1,)), ((), ())),
                           preferred_element_type=F32)


def _dot_tn(a, b):
    return lax.dot_general(a.astype(BF16), b.astype(BF16), (((0,), (0,)), ((), ())),
                           preferred_element_type=F32)


def _cumsum_rows(g, tril_b):
    g1 = g.astype(BF16)
    r1 = g - g1.astype(F32)
    g2 = r1.astype(BF16)
    g3 = (r1 - g2.astype(F32)).astype(BF16)
    d = functools.partial(jnp.dot, preferred_element_type=F32)
    return d(tril_b, g1) + d(tril_b, g2) + d(tril_b, g3)


def _unit_lower_inverse_offdiag(nmats, row, col, C):
    ds = [-jnp.where((row >> 1) == (col >> 1), n, 0.0) for n in nmats]
    s, ls = 2, 1
    while s < C:
        same_2s = (row >> (ls + 1)) == (col >> (ls + 1))
        same_s = (row >> ls) == (col >> ls)
        es = [jnp.where(same_2s, jnp.where(same_s, 0.0, n), 0.0) for n in nmats]
        ys = [e + _dot(d, e) for d, e in zip(ds, es)]
        ds = [d - (y + _dot(y, d)) for d, y in zip(ds, ys)]
        s, ls = 2 * s, ls + 1
    return ds


def _proj_kernel(x_ref, nw_ref, w_ref, *out_refs):
    hn = _rmsnorm(x_ref[...], nw_ref[...]).astype(BF16)
    off = 0
    for o_ref, width in zip(out_refs, PROJ_WIDTHS):
        o_ref[...] = jnp.dot(hn, w_ref[:, off:off + width], preferred_element_type=F32)
        off += width


def _proj_call(x, norm_w, w_packed, tm):
    T = x.shape[0]
    const = lambda i: (0, 0)
    return pl.pallas_call(
        _proj_kernel,
        grid=(T // tm,),
        in_specs=[
            pl.BlockSpec((tm, D_MODEL), lambda i: (i, 0)),
            pl.BlockSpec((1, D_MODEL), const),
            pl.BlockSpec((D_MODEL, D_PROJ_PACKED), const, pipeline_mode=pl.Buffered(1)),
        ],
        out_specs=[pl.BlockSpec((tm, w), lambda i: (i, 0)) for w in PROJ_WIDTHS],
        out_shape=[jax.ShapeDtypeStruct((T, w), F32) for w in PROJ_WIDTHS],
        compiler_params=pltpu.CompilerParams(
            dimension_semantics=("parallel",), vmem_limit_bytes=VMEM_LIMIT_BYTES),
        name="proj",
    )(x, norm_w.reshape(1, D_MODEL), w_packed)


def _causal_conv(ext_ref, cur, w_ref, width, C):
    ext_ref[HIST_ROWS:HIST_ROWS + C, :] = cur
    y = None
    for i in range(width):
        off = HIST_ROWS - (width - 1) + i
        term = ext_ref[off:off + C, :] * w_ref[i:i + 1, :]
        y = term if y is None else y + term
    return y


def _mix_kernel(qkv_ref, za_ref, xbc_ref, zb_ref, hcb_ref, small_ref,
                hista_ref, histb_ref, histc_ref, s0a_ref, s0b_ref,
                cwa_ref, cwb_ref, cbb_ref, cwc_ref, hp_ref, ona_ref, onb_ref, *rest, G, C, pad, nC):
    (mixed_ref, sa_out, sb_out, ha_out, hb_out, hc_out, exta, extb, extc, sa, sb) = rest[-11:]
    c = pl.program_id(1)

    @pl.when(c == 0)
    def _():
        exta[:, 0:HIST_ROWS, :] = hista_ref[...]
        extb[:, 0:HIST_ROWS, :] = histb_ref[...]
        extc[:, 0:HIST_ROWS, :] = histc_ref[...]
        sa[...] = s0a_ref[...]
        sb[...] = s0b_ref[...]

    row = lax.broadcasted_iota(jnp.int32, (C, C), 0)
    col = lax.broadcasted_iota(jnp.int32, (C, C), 1)
    tril = row >= col
    strict = row > col
    tril_b = jnp.where(tril, 1.0, 0.0).astype(BF16)

    if pad:
        tok = c * C + lax.broadcasted_iota(jnp.int32, (C, 1), 0)
        valid = tok >= pad
        keep = lambda v: jnp.where(valid, v, 0.0)
    else:
        keep = lambda v: v

    seqs = range(G)
    lane_col = lambda a, lane: a[:, lane:lane + 1]

    small = [small_ref[g] for g in seqs]
    sp = [keep(_softplus(s + hp_ref[0:1, :])) for s in small]
    beta_all = [jax.nn.sigmoid(s) for s in small]
    G_all = [_cumsum_rows(-jnp.exp(hp_ref[1:2, :]) * s, tril_b) for s in sp]
    GT_all = [x.T for x in G_all]
    eG_all = [jnp.exp(x) for x in G_all]
    eG_last = [jnp.exp(x[C - 1:C, :]) for x in G_all]
    eG_rem = [jnp.exp(x[C - 1:C, :] - x) for x in G_all]

    def decay_matrix(g, lane):
        dlog = lane_col(G_all[g], lane) - GT_all[g][lane:lane + 1, :]
        return jnp.where(tril, jnp.exp(jnp.minimum(dlog, 0.0)), 0.0)

    qkv = [_silu(_causal_conv(exta.at[g], keep(qkv_ref[g]), cwa_ref, CONV_A, C)) for g in seqs]
    chains = [(g, h) for g in seqs for h in range(H_A)]
    qs, ks, vs, kbs, egs = [], [], [], [], []
    for g, h in chains:
        q = qkv[g][:, h * DK_A:(h + 1) * DK_A]
        k = qkv[g][:, D_QK_A + h * DK_A:D_QK_A + (h + 1) * DK_A]
        v = qkv[g][:, 2 * D_QK_A + h * DV_A:2 * D_QK_A + (h + 1) * DV_A]
        q = q * lax.rsqrt(jnp.sum(q * q, axis=-1, keepdims=True) + EPS) * (DK_A ** -0.5)
        k = k * lax.rsqrt(jnp.sum(k * k, axis=-1, keepdims=True) + EPS)
        beta = lane_col(beta_all[g], LANE_BETA + h)
        qs.append(q)
        ks.append(k)
        vs.append(v * beta)
        kbs.append(k * beta)
        egs.append(lane_col(eG_all[g], LANE_GA + h))
    kqs = [_dot_nt(jnp.concatenate([kb, q], axis=0), k) for kb, q, k in zip(kbs, qs, ks)]
    decs = [decay_matrix(g, LANE_GA + h) for g, h in chains]
    nmats = [jnp.where(strict, kq[:C] * dec, 0.0) for kq, dec in zip(kqs, decs)]
    qks = [kq[C:] * dec for kq, dec in zip(kqs, decs)]
    tinvs = _unit_lower_inverse_offdiag(nmats, row, col, C)
    rhss = [jnp.concatenate([vb, kb * eg], axis=1) for vb, kb, eg in zip(vs, kbs, egs)]
    sols = [rhs + _dot(t, rhs) for t, rhs in zip(tinvs, rhss)]
    s_olds = [sa[g, h] for g, h in chains]
    wss = [_dot(jnp.concatenate([sol[:, DV_A:], q * eg], axis=0), s_old)
           for sol, q, eg, s_old in zip(sols, qs, egs, s_olds)]
    ws_ = [sol[:, :DV_A] - ws[:C] for sol, ws in zip(sols, wss)]
    os_ = [ws[C:] + _dot(qk, w) for ws, qk, w in zip(wss, qks, ws_)]
    for (g, h), k, w, s_old, o in zip(chains, ks, ws_, s_olds, os_):
        lane = LANE_GA + h
        sa[g, h] = lane_col(eG_last[g], lane) * s_old + _dot_tn(k * lane_col(eG_rem[g], lane), w)
        gate = _silu(za_ref[g, :, h * DV_A:(h + 1) * DV_A])
        mixed_ref[g, :, h * DV_A:(h + 1) * DV_A] = _rmsnorm(o, ona_ref[...]) * gate

    heads_per_group = H_B // G_B
    gw = D_B // G_B
    xbc = [_silu(_causal_conv(extb.at[g], keep(xbc_ref[g]), cwb_ref, CONV_B, C) + cbb_ref[...])
           for g in seqs]
    bgs = {(g, grp): xbc[g][:, D_B + grp * N_B:D_B + (grp + 1) * N_B] for g in seqs for grp in range(G_B)}
    cgs = {(g, grp): xbc[g][:, D_B + (G_B + grp) * N_B:D_B + (G_B + grp + 1) * N_B]
           for g in seqs for grp in range(G_B)}
    cbts = {key: _dot_nt(cgs[key], bgs[key]) for key in bgs}
    heads = [(g, h) for g in seqs for h in range(H_B)]
    x_hs = [xbc[g][:, h * P_B:(h + 1) * P_B] for g, h in heads]
    xdts = [x_h * lane_col(sp[g], LANE_GB + h) for (g, h), x_h in zip(heads, x_hs)]
    h_olds = [sb[g, h] for g, h in heads]
    y_offs = [_dot_nt(cgs[(g, h // heads_per_group)] * lane_col(eG_all[g], LANE_GB + h), h_old)
              for (g, h), h_old in zip(heads, h_olds)]
    for (g, h), xdt, h_old in zip(heads, xdts, h_olds):
        lane = LANE_GB + h
        bdec = bgs[(g, h // heads_per_group)] * lane_col(eG_rem[g], lane)
        sb[g, h] = lane_col(eG_last[g], lane) * h_old + _dot_tn(xdt, bdec)
    ys = [y_off + _dot(cbts[(g, h // heads_per_group)] * decay_matrix(g, LANE_GB + h), xdt)
          + hp_ref[2:3, LANE_GB + h:LANE_GB + h + 1] * x_h
          for (g, h), y_off, xdt, x_h in zip(heads, y_offs, xdts, x_hs)]
    for g in seqs:
        for grp in range(G_B):
            lo = g * H_B + grp * heads_per_group
            yg = jnp.concatenate(ys[lo:lo + heads_per_group], axis=1)
            yg = yg * _silu(zb_ref[g, :, grp * gw:(grp + 1) * gw])
            mixed_ref[g, :, D_A + grp * gw:D_A + (grp + 1) * gw] = _rmsnorm(
                yg, onb_ref[:, grp * gw:(grp + 1) * gw])

    for g in seqs:
        hcb = hcb_ref[g]
        u_c = keep(hcb[:, D_C:2 * D_C] * hcb[:, 0:D_C])
        y_c = _causal_conv(extc.at[g], u_c, cwc_ref, CONV_C, C)
        mixed_ref[g, :, D_A + D_B:D_MIX] = hcb[:, 2 * D_C:3 * D_C] * y_c

    @pl.when(c == nC - 1)
    def _():
        sa_out[...] = sa[...]
        sb_out[...] = sb[...]
        ha_out[...] = exta[:, HIST_ROWS + C - (CONV_A - 1):HIST_ROWS + C, :]
        hb_out[...] = extb[:, HIST_ROWS + C - (CONV_B - 1):HIST_ROWS + C, :]
        hc_out[...] = extc[:, HIST_ROWS + C - (CONV_C - 1):HIST_ROWS + C, :]

    exta[:, 0:HIST_ROWS, :] = exta[:, C:C + HIST_ROWS, :]
    extb[:, 0:HIST_ROWS, :] = extb[:, C:C + HIST_ROWS, :]
    extc[:, 0:HIST_ROWS, :] = extc[:, C:C + HIST_ROWS, :]


def _mix_call(proj, hists, s0a, s0b, layer, weights, stacked, out_layer, depth, *, B, nC, C, G, pad):
    L = nC * C
    cwa, cwb, cbb, cwc, hp, ona, onb = weights
    tok = lambda b, c: (b, c, 0)
    per_b3 = lambda b, c: (b, 0, 0)
    out_b4 = lambda b, c: (out_layer, b, 0, 0)
    out_b5 = lambda b, c: (out_layer, b, 0, 0, 0)
    const = lambda b, c: (0, 0)
    in_specs = [pl.BlockSpec((G, C, w), tok) for w in PROJ_WIDTHS]
    in_specs += [pl.BlockSpec((G, HIST_ROWS, w), per_b3) for w in (D_QKV_A, D_XBC, D_C)]
    in_specs += [
        pl.BlockSpec((None, G, H_A, DK_A, DV_A), lambda b, c: (layer, b, 0, 0, 0)),
        pl.BlockSpec((None, G, H_B, P_B, N_B), lambda b, c: (layer, b, 0, 0, 0)),
        pl.BlockSpec((CONV_A, D_QKV_A), const),
        pl.BlockSpec((CONV_B, D_XBC), const),
        pl.BlockSpec((1, D_XBC), const),
        pl.BlockSpec((CONV_C, D_C), const),
        pl.BlockSpec((8, D_SMALL), const),
        pl.BlockSpec((1, DV_A), const),
        pl.BlockSpec((1, D_B), const),
    ]
    out_specs = [
        pl.BlockSpec((G, C, D_MIX), tok),
        pl.BlockSpec((None, G, H_A, DK_A, DV_A), out_b5),
        pl.BlockSpec((None, G, H_B, P_B, N_B), out_b5),
        pl.BlockSpec((None, G, CONV_A - 1, D_QKV_A), out_b4),
        pl.BlockSpec((None, G, CONV_B - 1, D_XBC), out_b4),
        pl.BlockSpec((None, G, CONV_C - 1, D_C), out_b4),
    ]
    out_shape = [
        jax.ShapeDtypeStruct((B, L, D_MIX), F32),
        jax.ShapeDtypeStruct((depth, B, H_A, DK_A, DV_A), F32),
        jax.ShapeDtypeStruct((depth, B, H_B, P_B, N_B), F32),
        jax.ShapeDtypeStruct((depth, B, CONV_A - 1, D_QKV_A), F32),
        jax.ShapeDtypeStruct((depth, B, CONV_B - 1, D_XBC), F32),
        jax.ShapeDtypeStruct((depth, B, CONV_C - 1, D_C), F32),
    ]
    stacked = list(stacked or [])
    n_in = len(in_specs)
    in_specs += [pl.BlockSpec(memory_space=pl.ANY) for _ in stacked]
    aliases = {n_in + i: 1 + i for i in range(len(stacked))}
    scratch = [
        pltpu.VMEM((G, C + HIST_ROWS, D_QKV_A), F32),
        pltpu.VMEM((G, C + HIST_ROWS, D_XBC), F32),
        pltpu.VMEM((G, C + HIST_ROWS, D_C), F32),
        pltpu.VMEM((G, H_A, DK_A, DV_A), F32),
        pltpu.VMEM((G, H_B, P_B, N_B), F32),
    ]
    proj3 = [p.reshape(B, L, p.shape[-1]) for p in proj]
    outs = pl.pallas_call(
        functools.partial(_mix_kernel, G=G, C=C, pad=pad, nC=nC),
        grid=(B // G, nC),
        in_specs=in_specs,
        out_specs=out_specs,
        out_shape=out_shape,
        scratch_shapes=scratch,
        input_output_aliases=aliases,
        compiler_params=pltpu.CompilerParams(
            dimension_semantics=("parallel", "arbitrary"), vmem_limit_bytes=VMEM_LIMIT_BYTES),
        name="mix",
    )(*proj3, *hists, s0a, s0b, cwa, cwb, cbb, cwc, hp, ona, onb, *stacked)
    return [outs[0].reshape(B * L, D_MIX)] + list(outs[1:])


def _route(logits):
    lg = [logits[:, g:g + 1] for g in range(N_EGROUPS)]
    m = functools.reduce(jnp.maximum, lg)
    ex = [jnp.exp(l - m) for l in lg]
    den = functools.reduce(lambda a, b: a + b, ex)
    p = [e / den for e in ex]
    p_top, g_idx = p[0], jnp.zeros_like(p[0], dtype=jnp.int32)
    for g in range(1, N_EGROUPS):
        better = p[g] > p_top
        g_idx = jnp.where(better, g, g_idx)
        p_top = jnp.where(better, p[g], p_top)
    g_hot = [g_idx == g for g in range(N_EGROUPS)]
    le_sel = []
    for j in range(EXP_PER_GROUP):
        acc = None
        for g in range(N_EGROUPS):
            lane = N_EGROUPS + g * EXP_PER_GROUP + j
            t = jnp.where(g_hot[g], logits[:, lane:lane + 1], 0.0)
            acc = t if acc is None else acc + t
        le_sel.append(acc)
    m2 = functools.reduce(jnp.maximum, le_sel)
    ex2 = [jnp.exp(l - m2) for l in le_sel]
    den2 = functools.reduce(lambda a, b: a + b, ex2)
    s = [e / den2 for e in ex2]
    v1, i1 = s[0], jnp.zeros_like(g_idx)
    for j in range(1, EXP_PER_GROUP):
        better = s[j] > v1
        i1 = jnp.where(better, j, i1)
        v1 = jnp.where(better, s[j], v1)
    v2, i2 = jnp.full_like(v1, -1.0), jnp.zeros_like(g_idx)
    for j in range(EXP_PER_GROUP):
        s_rest = jnp.where(i1 == j, -1.0, s[j])
        better = s_rest > v2
        i2 = jnp.where(better, j, i2)
        v2 = jnp.where(better, s_rest, v2)
    tot = v1 + v2
    w1 = v1 / tot * p_top
    w2 = v2 / tot * p_top
    comb = []
    for g in range(N_EGROUPS):
        for j in range(EXP_PER_GROUP):
            inner = jnp.where(i1 == j, w1, jnp.where(i2 == j, w2, 0.0))
            comb.append(jnp.where(g_hot[g], inner, 0.0))
    return comb


def _post_kernel(x_ref, mixed_ref, wout_ref, n2_ref, wr_ref, wg_ref, wu_ref, wd_ref, fn_ref,
                 o_ref, *, final):
    x1 = x_ref[...] + jnp.dot(mixed_ref[...].astype(BF16), wout_ref[...], preferred_element_type=F32)
    hb = _rmsnorm(x1, n2_ref[...]).astype(BF16)
    comb = _route(jnp.dot(hb, wr_ref[...], preferred_element_type=F32))
    def activations(grp):
        acts = []
        for e in range(grp * EXPERT_BATCH, (grp + 1) * EXPERT_BATCH):
            hg = jnp.dot(hb, wg_ref[e], preferred_element_type=F32)
            hu = jnp.dot(hb, wu_ref[e], preferred_element_type=F32)
            acts.append((_silu(hg) * hu * comb[e]).astype(BF16))
        return jnp.concatenate(acts, axis=1)

    n_batches = N_EXPERTS // EXPERT_BATCH
    kb = EXPERT_BATCH * D_FF_E
    acts = [activations(0)]
    acc = x1
    for grp in range(n_batches):
        if grp + 1 < n_batches:
            acts.append(activations(grp + 1))
        acc = acc + jnp.dot(acts[grp], wd_ref[grp * kb:(grp + 1) * kb, :], preferred_element_type=F32)
    if final:
        acc = _rmsnorm(acc, fn_ref[...])
    o_ref[...] = acc


def _post_call(x, mixed, w_out, n2, w_router, w_gate, w_up, w_down, fn, *, tm, final):
    T = x.shape[0]
    const2 = lambda i: (0, 0)
    const3 = lambda i: (0, 0, 0)
    single = dict(pipeline_mode=pl.Buffered(1))
    return pl.pallas_call(
        functools.partial(_post_kernel, final=final),
        grid=(T // tm,),
        in_specs=[
            pl.BlockSpec((tm, D_MODEL), lambda i: (i, 0)),
            pl.BlockSpec((tm, D_MIX), lambda i: (i, 0)),
            pl.BlockSpec((D_MIX, D_MODEL), const2, **single),
            pl.BlockSpec((1, D_MODEL), const2),
            pl.BlockSpec((D_MODEL, D_SMALL), const2, **single),
            pl.BlockSpec((N_EXPERTS, D_MODEL, D_FF_E), const3, **single),
            pl.BlockSpec((N_EXPERTS, D_MODEL, D_FF_E), const3, **single),
            pl.BlockSpec((N_EXPERTS * D_FF_E, D_MODEL), const2, **single),
            pl.BlockSpec((1, D_MODEL), const2),
        ],
        out_specs=pl.BlockSpec((tm, D_MODEL), lambda i: (i, 0)),
        out_shape=jax.ShapeDtypeStruct((T, D_MODEL), F32),
        compiler_params=pltpu.CompilerParams(
            dimension_semantics=("parallel",), vmem_limit_bytes=VMEM_LIMIT_BYTES),
        name="post",
    )(x, mixed, w_out, n2, w_router, w_gate, w_up, w_down, fn)


def _pack_w_in(w):
    o = [0]
    for s in (D_QKV_A, D_A, H_A, H_A, D_XBC, D_B, H_B, D_C, D_C, D_C):
        o.append(o[-1] + s)
    cols = [w[:, o[0]:o[2]], w[:, o[4]:o[6]], w[:, o[7]:o[10]], w[:, o[2]:o[4]], w[:, o[6]:o[7]],
            jnp.zeros((D_MODEL, D_SMALL - 2 * H_A - H_B), w.dtype)]
    return jnp.concatenate(cols, axis=1).astype(BF16)


def _head_params(dt_bias_a, dt_bias_b, a_log_a, a_log_b, d_skip_b):
    z4 = jnp.zeros((H_A,), F32)
    tail = jnp.zeros((D_SMALL - 2 * H_A - H_B,), F32)
    rows = [
        jnp.concatenate([z4, dt_bias_a, dt_bias_b, tail]),
        jnp.concatenate([z4, a_log_a, a_log_b, tail]),
        jnp.concatenate([z4, z4, d_skip_b, tail]),
    ]
    return jnp.concatenate([jnp.stack(rows), jnp.zeros((8 - len(rows), D_SMALL), F32)], axis=0)


def _pad_hist(h):
    return jnp.pad(h, ((0, 0), (0, 0), (HIST_ROWS - h.shape[2], 0), (0, 0)))


def kernel(x_prompt, x_sample, state_gdn, state_gdn_conv, state_ssd, state_ssd_conv, state_sconv,
           meta_tokens, norm1_w, w_in, conv_a_w, a_log_a, dt_bias_a, onorm_a_w, conv_b_w, conv_b_b,
           a_log_b, dt_bias_b, d_skip_b, onorm_b_w, conv_c_w, w_out, norm2_w, w_router_g,
           w_router_e, w_gate, w_up, w_down, final_norm_w):
    bp, seq, _ = x_prompt.shape
    bs, dseq, _ = x_sample.shape
    depth = w_in.shape[0]
    pad = CHUNK - N_META
    lp = pad + N_META + seq
    ncp = lp // CHUNK

    meta = jnp.broadcast_to(meta_tokens[None], (bp, N_META, D_MODEL))
    x_p = jnp.concatenate([jnp.zeros((bp, pad, D_MODEL), F32), meta, x_prompt], axis=1)
    x_p = x_p.reshape(bp * lp, D_MODEL)
    x_s = x_sample.reshape(bs * dseq, D_MODEL)

    zero_hists = [jnp.zeros((bp, HIST_ROWS, w), F32) for w in (D_QKV_A, D_XBC, D_C)]
    zero_sa = jnp.zeros((1, bp, H_A, DK_A, DV_A), F32)
    zero_sb = jnp.zeros((1, bp, H_B, P_B, N_B), F32)
    hists_s = [_pad_hist(state_gdn_conv), _pad_hist(state_ssd_conv), _pad_hist(state_sconv)]

    fn = final_norm_w.reshape(1, D_MODEL)
    st_p = st_s = None
    for l in range(depth):
        w_packed = _pack_w_in(w_in[l])
        mix_w = (conv_a_w[l], conv_b_w[l], conv_b_b[l].reshape(1, D_XBC), conv_c_w[l],
                 _head_params(dt_bias_a[l], dt_bias_b[l], a_log_a[l], a_log_b[l], d_skip_b[l]),
                 onorm_a_w[l].reshape(1, DV_A), onorm_b_w[l].reshape(1, D_B))
        w_router = jnp.concatenate(
            [w_router_g[l], w_router_e[l],
             jnp.zeros((D_MODEL, D_SMALL - N_EGROUPS - N_EXPERTS), F32)], axis=1).astype(BF16)
        post_w = (w_out[l].astype(BF16), norm2_w[l].reshape(1, D_MODEL), w_router,
                  w_gate[l].astype(BF16), w_up[l].astype(BF16),
                  w_down[l].reshape(N_EXPERTS * D_FF_E, D_MODEL).astype(BF16), fn)
        final = l == depth - 1

        proj_p = _proj_call(x_p, norm1_w[l], w_packed, tm=TM_PROJ)
        mixed_p, *st_p = _mix_call(proj_p, zero_hists, zero_sa, zero_sb, 0, mix_w, st_p, l, depth,
                                   B=bp, nC=ncp, C=CHUNK, G=G_PROMPT, pad=pad)
        x_p = _post_call(x_p, mixed_p, *post_w, tm=TM_POST, final=final)

        proj_s = _proj_call(x_s, norm1_w[l], w_packed, tm=TM_PROJ)
        mixed_s, *st_s = _mix_call(proj_s, [h[l] for h in hists_s], state_gdn, state_ssd, l, mix_w,
                                   st_s, l, depth, B=bs, nC=1, C=dseq, G=G_SAMPLE, pad=0)
        x_s = _post_call(x_s, mixed_s, *post_w, tm=TM_POST, final=final)

    y_prompt = x_p.reshape(bp, lp, D_MODEL)[:, pad + N_META:]
    y_sample = x_s.reshape(bs, dseq, D_MODEL)
    return (y_prompt, y_sample, st_p[0], st_p[2], st_p[1], st_p[3], st_p[4],
            st_s[0], st_s[2], st_s[1], st_s[3], st_s[4])
```

```python
import functools

import jax
import jax.numpy as jnp
from jax import lax
from jax.experimental import pallas as pl
from jax.experimental.pallas import tpu as pltpu

F32 = jnp.float32
BF16 = jnp.bfloat16

D_MODEL = 1024
N_META = 16
H_A = 4
DK_A = 128
DV_A = 128
D_QK_A = H_A * DK_A
D_A = H_A * DV_A
D_QKV_A = 2 * D_QK_A + D_A
CONV_A = 4
H_B = 8
P_B = 64
D_B = H_B * P_B
G_B = 2
N_B = 128
D_XBC = D_B + 2 * G_B * N_B
CONV_B = 4
D_C = 512
CONV_C = 3
D_AB = D_A + D_B
D_MIX = D_AB + D_C
N_EGROUPS = 4
EXP_PER_GROUP = 4
N_EXPERTS = N_EGROUPS * EXP_PER_GROUP
D_FF_E = 256
EPS = 1e-6
CHUNK = 64

D_HCB = 3 * D_C
D_SMALL = 128
PROJ_WIDTHS = (D_QKV_A, D_A, D_XBC, D_B, D_HCB, D_SMALL)
D_PROJ_PACKED = sum(PROJ_WIDTHS)
PROJ_OFFSETS = tuple(sum(PROJ_WIDTHS[:i]) for i in range(len(PROJ_WIDTHS)))
ACT_WIDTHS = (D_QKV_A, D_A, D_XBC, D_B, D_C, D_SMALL)
HIST_WIDTHS = (D_QKV_A, D_XBC, D_C)
HIST_LENS = (CONV_A - 1, CONV_B - 1, CONV_C - 1)
LANE_BETA = 0
LANE_GA = H_A
LANE_GB = 2 * H_A
HIST_ROWS = 8
G_PROMPT = 2
G_SAMPLE = 8
EXPERT_BATCH = 4
TM_PROJ = 512
TM_POST = 512
TILES_PER_PROMPT = 4
CONV_COLS = 512

VMEM_LIMIT_BYTES = 56 * 1024 * 1024


def _rmsnorm(x, w):
    return x * lax.rsqrt(jnp.mean(x * x, axis=-1, keepdims=True) + EPS) * w


def _silu(x):
    return x * jax.nn.sigmoid(x)


def _softplus(x):
    return jnp.maximum(x, 0.0) + jnp.log1p(jnp.exp(-jnp.abs(x)))


def _dot(a, b):
    return jnp.dot(a.astype(BF16), b.astype(BF16), preferred_element_type=F32)


def _dot_nt(a, b):
    return lax.dot_general(a.astype(BF16), b.astype(BF16), (((1,), (1,)), ((), ())),
                           preferred_element_type=F32)


def _dot_tn(a, b):
    return lax.dot_general(a.astype(BF16), b.astype(BF16), (((0,), (0,)), ((), ())),
                           preferred_element_type=F32)


def _cumsum_rows(g, tril_b):
    g1 = g.astype(BF16)
    r1 = g - g1.astype(F32)
    g2 = r1.astype(BF16)
    g3 = (r1 - g2.astype(F32)).astype(BF16)
    d = functools.partial(jnp.dot, preferred_element_type=F32)
    return d(tril_b, g1) + d(tril_b, g2) + d(tril_b, g3)


def _unit_lower_inverse_offdiag(nmats, row, col, C):
    ds = [-jnp.where((row >> 1) == (col >> 1), n, 0.0) for n in nmats]
    s, ls = 2, 1
    while s < C:
        same_2s = (row >> (ls + 1)) == (col >> (ls + 1))
        same_s = (row >> ls) == (col >> ls)
        es = [jnp.where(same_2s, jnp.where(same_s, 0.0, n), 0.0) for n in nmats]
        ys = [e + _dot(d, e) for d, e in zip(ds, es)]
        ds = [d - (y + _dot(y, d)) for d, y in zip(ds, ys)]
        s, ls = 2 * s, ls + 1
    return ds


def _causal_conv(hist_ref, cur, w_ref, width, C, col0=0):
    n_cols = cur.shape[1]
    x3 = jnp.concatenate([hist_ref[...], cur], axis=0).reshape((C + HIST_ROWS) // 8, 8, n_cols)
    sub = lax.broadcasted_iota(jnp.int32, x3.shape, 1)

    def shift(a3, k):
        r = pltpu.roll(a3, k, axis=1)
        from_prev_tile = jnp.concatenate([r[:1], r[:-1]], axis=0)
        return jnp.where(sub < k, from_prev_tile, r)

    w = lambda i: w_ref[i:i + 1, col0:col0 + n_cols]
    if width == 4:
        x1 = shift(x3, 1)
        y3 = (x3 * w(3) + x1 * w(2)) + shift(x3 * w(1) + x1 * w(0), 2)
    else:
        assert width == 3
        y3 = x3 * w(2) + shift(x3, 1) * w(1) + shift(x3, 2) * w(0)
    hist_ref[...] = cur[C - HIST_ROWS:C]
    return y3.reshape(C + HIST_ROWS, n_cols)[HIST_ROWS:]


def _conv_front_end(raw_qkv, raw_xbc, hcb, exts, conv_w, keep, C):
    exta, extb, extc = exts
    cwa_ref, cwb_ref, cbb_ref, cwc_ref = conv_w
    qkv_act = _silu(_causal_conv(exta, keep(raw_qkv), cwa_ref, CONV_A, C))
    xbc_act = _silu(_causal_conv(extb, keep(raw_xbc), cwb_ref, CONV_B, C) + cbb_ref[...])
    u_c = keep(hcb[:, D_C:2 * D_C] * hcb[:, 0:D_C])
    y_c = hcb[:, 2 * D_C:3 * D_C] * _causal_conv(extc, u_c, cwc_ref, CONV_C, C)
    return qkv_act, xbc_act, y_c


def _proj_kernel(x_ref, nw_ref, w_ref, *out_refs):
    hn = _rmsnorm(x_ref[...], nw_ref[...]).astype(BF16)
    for o_ref, off, width in zip(out_refs, PROJ_OFFSETS, PROJ_WIDTHS):
        o_ref[...] = jnp.dot(hn, w_ref[:, off:off + width], preferred_element_type=F32)


def _proj_call(x, norm_w, w_packed, tm):
    T = x.shape[0]
    const = lambda i: (0, 0)
    return pl.pallas_call(
        _proj_kernel,
        grid=(T // tm,),
        in_specs=[
            pl.BlockSpec((tm, D_MODEL), lambda i: (i, 0)),
            pl.BlockSpec((1, D_MODEL), const),
            pl.BlockSpec((D_MODEL, D_PROJ_PACKED), const, pipeline_mode=pl.Buffered(1)),
        ],
        out_specs=[pl.BlockSpec((tm, w), lambda i: (i, 0)) for w in PROJ_WIDTHS],
        out_shape=[jax.ShapeDtypeStruct((T, w), F32) for w in PROJ_WIDTHS],
        compiler_params=pltpu.CompilerParams(
            dimension_semantics=("parallel",), vmem_limit_bytes=VMEM_LIMIT_BYTES),
        name="proj",
    )(x, norm_w.reshape(1, D_MODEL), w_packed)


def _proj_conv_kernel(x_ref, nw_ref, w_ref, cwa_ref, cwb_ref, cbb_ref, cwc_ref, *rest, tm, pad):
    n_ext = (D_QKV_A + D_XBC + D_C) // CONV_COLS
    (qkv_o, za_o, xbc_o, zb_o, yc_o, small_o, ha_o, hb_o, hc_o) = rest[-9 - n_ext:-n_ext]
    exts = rest[-n_ext:]
    j = lax.rem(pl.program_id(0), TILES_PER_PROMPT)

    @pl.when(j == 0)
    def _():
        for ext in exts:
            ext[...] = jnp.zeros((HIST_ROWS, CONV_COLS), F32)

    valid = j * tm + lax.broadcasted_iota(jnp.int32, (tm, 1), 0) >= pad
    keep = lambda v: jnp.where(valid, v, 0.0)
    hn = _rmsnorm(x_ref[...], nw_ref[...]).astype(BF16)
    mm = lambda off, width: jnp.dot(hn, w_ref[:, off:off + width], preferred_element_type=F32)
    off_qkv, off_za, off_xbc, off_zb, off_hcb, off_small = PROJ_OFFSETS

    def carry(ext, hist_ref, n, col0):
        hist_ref[:, col0:col0 + CONV_COLS] = ext[HIST_ROWS - n:HIST_ROWS, :]

    jobs = []
    for k in range(D_QKV_A // CONV_COLS):
        def tail(raw, k=k, ext=exts[k]):
            col0 = k * CONV_COLS
            qkv_o[:, col0:col0 + CONV_COLS] = _silu(_causal_conv(ext, keep(raw), cwa_ref, CONV_A, tm, col0))
            carry(ext, ha_o, CONV_A - 1, col0)
        jobs.append((functools.partial(mm, off_qkv + k * CONV_COLS, CONV_COLS), tail))
    for k in range(D_XBC // CONV_COLS):
        def tail(raw, k=k, ext=exts[D_QKV_A // CONV_COLS + k]):
            col0 = k * CONV_COLS
            conv = _causal_conv(ext, keep(raw), cwb_ref, CONV_B, tm, col0)
            xbc_o[:, col0:col0 + CONV_COLS] = _silu(conv + cbb_ref[:, col0:col0 + CONV_COLS])
            carry(ext, hb_o, CONV_B - 1, col0)
        jobs.append((functools.partial(mm, off_xbc + k * CONV_COLS, CONV_COLS), tail))

    def tail_c(hcb, ext=exts[-1]):
        u_c = keep(hcb[:, D_C:2 * D_C] * hcb[:, 0:D_C])
        yc_o[...] = hcb[:, 2 * D_C:3 * D_C] * _causal_conv(ext, u_c, cwc_ref, CONV_C, tm)
        carry(ext, hc_o, CONV_C - 1, 0)
    jobs.append((functools.partial(mm, off_hcb, D_HCB), tail_c))

    def tail_plain(raws):
        za_o[...], zb_o[...], small_o[...] = raws
    jobs.append((lambda: (mm(off_za, D_A), mm(off_zb, D_B), mm(off_small, D_SMALL)), tail_plain))

    pending = None
    for matmuls, tail in jobs:
        raw = matmuls()
        if pending is not None:
            pending[1](pending[0])
        pending = (raw, tail)
    pending[1](pending[0])


def _proj_conv_call(x, norm_w, w_packed, conv_w, stacked, out_layer, depth, *, B, pad):
    T = x.shape[0]
    tm = T // (B * TILES_PER_PROMPT)
    const = lambda i: (0, 0)
    row = lambda i: (i, 0)
    hist = lambda i: (out_layer, i // TILES_PER_PROMPT, 0, 0)
    cwa, cwb, cbb, cwc = conv_w
    in_specs = [
        pl.BlockSpec((tm, D_MODEL), row),
        pl.BlockSpec((1, D_MODEL), const),
        pl.BlockSpec((D_MODEL, D_PROJ_PACKED), const, pipeline_mode=pl.Buffered(1)),
        pl.BlockSpec((CONV_A, D_QKV_A), const),
        pl.BlockSpec((CONV_B, D_XBC), const),
        pl.BlockSpec((1, D_XBC), const),
        pl.BlockSpec((CONV_C, D_C), const),
    ]
    stacked = list(stacked or [])
    n_in = len(in_specs)
    in_specs += [pl.BlockSpec(memory_space=pl.ANY) for _ in stacked]
    n_act = len(ACT_WIDTHS)
    outs = pl.pallas_call(
        functools.partial(_proj_conv_kernel, tm=tm, pad=pad),
        grid=(T // tm,),
        in_specs=in_specs,
        out_specs=([pl.BlockSpec((tm, w), row) for w in ACT_WIDTHS]
                   + [pl.BlockSpec((None, None, n, w), hist) for n, w in zip(HIST_LENS, HIST_WIDTHS)]),
        out_shape=([jax.ShapeDtypeStruct((T, w), F32) for w in ACT_WIDTHS]
                   + [jax.ShapeDtypeStruct((depth, B, n, w), F32) for n, w in zip(HIST_LENS, HIST_WIDTHS)]),
        scratch_shapes=[pltpu.VMEM((HIST_ROWS, CONV_COLS), F32)
                        for _ in range(sum(HIST_WIDTHS) // CONV_COLS)],
        input_output_aliases={n_in + i: n_act + i for i in range(len(stacked))},
        compiler_params=pltpu.CompilerParams(
            dimension_semantics=("arbitrary",), vmem_limit_bytes=VMEM_LIMIT_BYTES),
        name="proj_conv",
    )(x, norm_w.reshape(1, D_MODEL), w_packed, cwa, cwb, cbb, cwc, *stacked)
    return outs[:n_act], outs[n_act:]


def _mix_kernel(*refs, G, C, pad, nC, preact):
    if preact:
        qkv_ref, za_ref, xbc_ref, zb_ref, small_ref, s0a_ref, s0b_ref, hp_ref, ona_ref, onb_ref = refs[:10]
        mixed_ref, sa_out, sb_out, sa, sb = refs[-5:]
    else:
        (qkv_ref, za_ref, xbc_ref, zb_ref, hcb_ref, small_ref, hista_ref, histb_ref, histc_ref,
         s0a_ref, s0b_ref, cwa_ref, cwb_ref, cbb_ref, cwc_ref, hp_ref, ona_ref, onb_ref) = refs[:18]
        (mixed_ref, yc_ref, sa_out, sb_out, ha_out, hb_out, hc_out, exta, extb, extc, sa, sb) = refs[-12:]
        exts, hist_refs, hist_outs = (exta, extb, extc), (hista_ref, histb_ref, histc_ref), (ha_out, hb_out, hc_out)
    c = pl.program_id(1)

    @pl.when(c == 0)
    def _():
        sa[...] = s0a_ref[...]
        sb[...] = s0b_ref[...]
        if not preact:
            for ext, h_ref in zip(exts, hist_refs):
                ext[...] = h_ref[...]

    row = lax.broadcasted_iota(jnp.int32, (C, C), 0)
    col = lax.broadcasted_iota(jnp.int32, (C, C), 1)
    tril = row >= col
    strict = row > col
    tril_b = jnp.where(tril, 1.0, 0.0).astype(BF16)

    if pad:
        tok = c * C + lax.broadcasted_iota(jnp.int32, (C, 1), 0)
        valid = tok >= pad
        keep = lambda v: jnp.where(valid, v, 0.0)
    else:
        keep = lambda v: v

    seqs = range(G)
    lane_col = lambda a, lane: a[:, lane:lane + 1]

    small = [small_ref[g] for g in seqs]
    sp = [keep(_softplus(s + hp_ref[0:1, :])) for s in small]
    beta_all = [jax.nn.sigmoid(s) for s in small]
    G_all = [_cumsum_rows(-jnp.exp(hp_ref[1:2, :]) * s, tril_b) for s in sp]
    GT_all = [x.T for x in G_all]
    eG_all = [jnp.exp(x) for x in G_all]
    eG_last = [jnp.exp(x[C - 1:C, :]) for x in G_all]
    eG_rem = [jnp.exp(x[C - 1:C, :] - x) for x in G_all]

    def decay_matrix(g, lane):
        dlog = lane_col(G_all[g], lane) - GT_all[g][lane:lane + 1, :]
        return jnp.where(tril, jnp.exp(jnp.minimum(dlog, 0.0)), 0.0)

    if preact:
        qkv = [qkv_ref[g] for g in seqs]
        xbc = [xbc_ref[g] for g in seqs]
    else:
        qkv, xbc = [], []
        for g in seqs:
            qkv_act, xbc_act, y_c = _conv_front_end(
                qkv_ref[g], xbc_ref[g], hcb_ref[g], [ext.at[g] for ext in exts],
                (cwa_ref, cwb_ref, cbb_ref, cwc_ref), keep, C)
            qkv.append(qkv_act)
            xbc.append(xbc_act)
            yc_ref[g] = y_c

    chains = [(g, h) for g in seqs for h in range(H_A)]
    qs, ks, vs, kbs, egs = [], [], [], [], []
    for g, h in chains:
        q = qkv[g][:, h * DK_A:(h + 1) * DK_A]
        k = qkv[g][:, D_QK_A + h * DK_A:D_QK_A + (h + 1) * DK_A]
        v = qkv[g][:, 2 * D_QK_A + h * DV_A:2 * D_QK_A + (h + 1) * DV_A]
        q = q * lax.rsqrt(jnp.sum(q * q, axis=-1, keepdims=True) + EPS) * (DK_A ** -0.5)
        k = k * lax.rsqrt(jnp.sum(k * k, axis=-1, keepdims=True) + EPS)
        beta = lane_col(beta_all[g], LANE_BETA + h)
        qs.append(q)
        ks.append(k)
        vs.append(v * beta)
        kbs.append(k * beta)
        egs.append(lane_col(eG_all[g], LANE_GA + h))
    kqs = [_dot_nt(jnp.concatenate([kb, q], axis=0), k) for kb, q, k in zip(kbs, qs, ks)]
    decs = [decay_matrix(g, LANE_GA + h) for g, h in chains]
    nmats = [jnp.where(strict, kq[:C] * dec, 0.0) for kq, dec in zip(kqs, decs)]
    qks = [kq[C:] * dec for kq, dec in zip(kqs, decs)]
    tinvs = _unit_lower_inverse_offdiag(nmats, row, col, C)
    rhss = [jnp.concatenate([vb, kb * eg], axis=1) for vb, kb, eg in zip(vs, kbs, egs)]
    sols = [rhs + _dot(t, rhs) for t, rhs in zip(tinvs, rhss)]
    s_olds = [sa[g, h] for g, h in chains]
    wss = [_dot(jnp.concatenate([sol[:, DV_A:], q * eg], axis=0), s_old)
           for sol, q, eg, s_old in zip(sols, qs, egs, s_olds)]
    ws_ = [sol[:, :DV_A] - ws[:C] for sol, ws in zip(sols, wss)]
    os_ = [ws[C:] + _dot(qk, w) for ws, qk, w in zip(wss, qks, ws_)]
    for (g, h), k, w, s_old, o in zip(chains, ks, ws_, s_olds, os_):
        lane = LANE_GA + h
        sa[g, h] = lane_col(eG_last[g], lane) * s_old + _dot_tn(k * lane_col(eG_rem[g], lane), w)
        gate = _silu(za_ref[g, :, h * DV_A:(h + 1) * DV_A])
        mixed_ref[g, :, h * DV_A:(h + 1) * DV_A] = _rmsnorm(o, ona_ref[...]) * gate

    heads_per_group = H_B // G_B
    gw = D_B // G_B
    bgs = {(g, grp): xbc[g][:, D_B + grp * N_B:D_B + (grp + 1) * N_B] for g in seqs for grp in range(G_B)}
    cgs = {(g, grp): xbc[g][:, D_B + (G_B + grp) * N_B:D_B + (G_B + grp + 1) * N_B]
           for g in seqs for grp in range(G_B)}
    cbts = {key: _dot_nt(cgs[key], bgs[key]) for key in bgs}
    heads = [(g, h) for g in seqs for h in range(H_B)]
    x_hs = [xbc[g][:, h * P_B:(h + 1) * P_B] for g, h in heads]
    xdts = [x_h * lane_col(sp[g], LANE_GB + h) for (g, h), x_h in zip(heads, x_hs)]
    h_olds = [sb[g, h] for g, h in heads]
    y_offs = [_dot_nt(cgs[(g, h // heads_per_group)] * lane_col(eG_all[g], LANE_GB + h), h_old)
              for (g, h), h_old in zip(heads, h_olds)]
    for (g, h), xdt, h_old in zip(heads, xdts, h_olds):
        lane = LANE_GB + h
        bdec = bgs[(g, h // heads_per_group)] * lane_col(eG_rem[g], lane)
        sb[g, h] = lane_col(eG_last[g], lane) * h_old + _dot_tn(xdt, bdec)
    ys = [y_off + _dot(cbts[(g, h // heads_per_group)] * decay_matrix(g, LANE_GB + h), xdt)
          + hp_ref[2:3, LANE_GB + h:LANE_GB + h + 1] * x_h
          for (g, h), y_off, xdt, x_h in zip(heads, y_offs, xdts, x_hs)]
    for g in seqs:
        for grp in range(G_B):
            lo = g * H_B + grp * heads_per_group
            yg = jnp.concatenate(ys[lo:lo + heads_per_group], axis=1)
            yg = yg * _silu(zb_ref[g, :, grp * gw:(grp + 1) * gw])
            mixed_ref[g, :, D_A + grp * gw:D_A + (grp + 1) * gw] = _rmsnorm(
                yg, onb_ref[:, grp * gw:(grp + 1) * gw])

    @pl.when(c == nC - 1)
    def _():
        sa_out[...] = sa[...]
        sb_out[...] = sb[...]
        if not preact:
            for o_ref, ext, n in zip(hist_outs, exts, HIST_LENS):
                o_ref[...] = ext[:, HIST_ROWS - n:HIST_ROWS, :]


def _mix_call(acts, hists, s0a, s0b, layer, weights, stacked, out_layer, depth, *, B, nC, C, G, pad, preact):
    L = nC * C
    conv_w, hp, ona, onb = weights
    tok = lambda b, c: (b, c, 0)
    per_b3 = lambda b, c: (b, 0, 0)
    out_b4 = lambda b, c: (out_layer, b, 0, 0)
    out_b5 = lambda b, c: (out_layer, b, 0, 0, 0)
    const = lambda b, c: (0, 0)
    state_specs = [
        pl.BlockSpec((None, G, H_A, DK_A, DV_A), lambda b, c: (layer, b, 0, 0, 0)),
        pl.BlockSpec((None, G, H_B, P_B, N_B), lambda b, c: (layer, b, 0, 0, 0)),
    ]
    head_specs = [pl.BlockSpec((8, D_SMALL), const), pl.BlockSpec((1, DV_A), const), pl.BlockSpec((1, D_B), const)]
    acts3 = [a.reshape(B, L, a.shape[-1]) for a in acts]
    in_specs = [pl.BlockSpec((G, C, a.shape[-1]), tok) for a in acts3]
    operands = list(acts3)
    if not preact:
        in_specs += [pl.BlockSpec((G, HIST_ROWS, w), per_b3) for w in HIST_WIDTHS]
        operands += list(hists)
    in_specs += state_specs
    operands += [s0a, s0b]
    if not preact:
        in_specs += [pl.BlockSpec((CONV_A, D_QKV_A), const), pl.BlockSpec((CONV_B, D_XBC), const),
                     pl.BlockSpec((1, D_XBC), const), pl.BlockSpec((CONV_C, D_C), const)]
        operands += list(conv_w)
    in_specs += head_specs
    operands += [hp, ona, onb]

    out_specs = [pl.BlockSpec((G, C, D_AB), tok)]
    out_shape = [jax.ShapeDtypeStruct((B, L, D_AB), F32)]
    if not preact:
        out_specs.append(pl.BlockSpec((G, C, D_C), tok))
        out_shape.append(jax.ShapeDtypeStruct((B, L, D_C), F32))
    n_plain = len(out_specs)
    out_specs += [pl.BlockSpec((None, G, H_A, DK_A, DV_A), out_b5), pl.BlockSpec((None, G, H_B, P_B, N_B), out_b5)]
    out_shape += [jax.ShapeDtypeStruct((depth, B, H_A, DK_A, DV_A), F32),
                  jax.ShapeDtypeStruct((depth, B, H_B, P_B, N_B), F32)]
    scratch = []
    if not preact:
        out_specs += [pl.BlockSpec((None, G, n, w), out_b4) for n, w in zip(HIST_LENS, HIST_WIDTHS)]
        out_shape += [jax.ShapeDtypeStruct((depth, B, n, w), F32) for n, w in zip(HIST_LENS, HIST_WIDTHS)]
        scratch += [pltpu.VMEM((G, HIST_ROWS, w), F32) for w in HIST_WIDTHS]
    scratch += [pltpu.VMEM((G, H_A, DK_A, DV_A), F32), pltpu.VMEM((G, H_B, P_B, N_B), F32)]

    stacked = list(stacked or [])
    n_in = len(in_specs)
    in_specs += [pl.BlockSpec(memory_space=pl.ANY) for _ in stacked]
    outs = pl.pallas_call(
        functools.partial(_mix_kernel, G=G, C=C, pad=pad, nC=nC, preact=preact),
        grid=(B // G, nC),
        in_specs=in_specs,
        out_specs=out_specs,
        out_shape=out_shape,
        scratch_shapes=scratch,
        input_output_aliases={n_in + i: n_plain + i for i in range(len(stacked))},
        compiler_params=pltpu.CompilerParams(
            dimension_semantics=("parallel", "arbitrary"), vmem_limit_bytes=VMEM_LIMIT_BYTES),
        name="mix",
    )(*operands, *stacked)
    plain = [o.reshape(B * L, o.shape[-1]) for o in outs[:n_plain]]
    return plain, list(outs[n_plain:])


def _route(logits):
    lg = [logits[:, g:g + 1] for g in range(N_EGROUPS)]
    m = functools.reduce(jnp.maximum, lg)
    ex = [jnp.exp(l - m) for l in lg]
    den = functools.reduce(lambda a, b: a + b, ex)
    p = [e / den for e in ex]
    p_top, g_idx = p[0], jnp.zeros_like(p[0], dtype=jnp.int32)
    for g in range(1, N_EGROUPS):
        better = p[g] > p_top
        g_idx = jnp.where(better, g, g_idx)
        p_top = jnp.where(better, p[g], p_top)
    g_hot = [g_idx == g for g in range(N_EGROUPS)]
    le_sel = []
    for j in range(EXP_PER_GROUP):
        acc = None
        for g in range(N_EGROUPS):
            lane = N_EGROUPS + g * EXP_PER_GROUP + j
            t = jnp.where(g_hot[g], logits[:, lane:lane + 1], 0.0)
            acc = t if acc is None else acc + t
        le_sel.append(acc)
    m2 = functools.reduce(jnp.maximum, le_sel)
    ex2 = [jnp.exp(l - m2) for l in le_sel]
    den2 = functools.reduce(lambda a, b: a + b, ex2)
    s = [e / den2 for e in ex2]
    v1, i1 = s[0], jnp.zeros_like(g_idx)
    for j in range(1, EXP_PER_GROUP):
        better = s[j] > v1
        i1 = jnp.where(better, j, i1)
        v1 = jnp.where(better, s[j], v1)
    v2, i2 = jnp.full_like(v1, -1.0), jnp.zeros_like(g_idx)
    for j in range(EXP_PER_GROUP):
        s_rest = jnp.where(i1 == j, -1.0, s[j])
        better = s_rest > v2
        i2 = jnp.where(better, j, i2)
        v2 = jnp.where(better, s_rest, v2)
    tot = v1 + v2
    w1 = v1 / tot * p_top
    w2 = v2 / tot * p_top
    comb = []
    for g in range(N_EGROUPS):
        for j in range(EXP_PER_GROUP):
            inner = jnp.where(i1 == j, w1, jnp.where(i2 == j, w2, 0.0))
            comb.append(jnp.where(g_hot[g], inner, 0.0))
    return comb


def _post_kernel(x_ref, mab_ref, yc_ref, wout_ref, n2_ref, wr_ref, wg_ref, wu_ref, wd_ref, fn_ref,
                 o_ref, *, final):
    mixed = jnp.concatenate([mab_ref[...].astype(BF16), yc_ref[...].astype(BF16)], axis=1)
    x1 = x_ref[...] + jnp.dot(mixed, wout_ref[...], preferred_element_type=F32)
    hb = _rmsnorm(x1, n2_ref[...]).astype(BF16)
    comb = _route(jnp.dot(hb, wr_ref[...], preferred_element_type=F32))

    def activations(grp):
        acts = []
        for e in range(grp * EXPERT_BATCH, (grp + 1) * EXPERT_BATCH):
            hg = jnp.dot(hb, wg_ref[e], preferred_element_type=F32)
            hu = jnp.dot(hb, wu_ref[e], preferred_element_type=F32)
            acts.append((_silu(hg) * hu * comb[e]).astype(BF16))
        return jnp.concatenate(acts, axis=1)

    n_batches = N_EXPERTS // EXPERT_BATCH
    kb = EXPERT_BATCH * D_FF_E
    acts = [activations(0)]
    acc = x1
    for grp in range(n_batches):
        if grp + 1 < n_batches:
            acts.append(activations(grp + 1))
        acc = acc + jnp.dot(acts[grp], wd_ref[grp * kb:(grp + 1) * kb, :], preferred_element_type=F32)
    if final:
        acc = _rmsnorm(acc, fn_ref[...])
    o_ref[...] = acc


def _post_call(x, mixed_ab, y_c, w_out, n2, w_router, w_gate, w_up, w_down, fn, *, tm, final, window=None):
    T = x.shape[0]
    single = dict(pipeline_mode=pl.Buffered(1))
    if window is None:
        grid = (T // tm,)
        const2 = lambda i: (0, 0)
        const3 = lambda i: (0, 0, 0)
        row = lambda i: (i, 0)
        tok_specs = [pl.BlockSpec((tm, w), row) for w in (D_MODEL, D_AB, D_C)]
        out_spec = pl.BlockSpec((tm, D_MODEL), row)
        out_shape = jax.ShapeDtypeStruct((T, D_MODEL), F32)
    else:
        B, L, start, length = window
        grid = (B, length // tm)
        const2 = lambda b, j: (0, 0)
        const3 = lambda b, j: (0, 0, 0)
        window_row = lambda b, j: (pl.multiple_of(b * L + start + j * tm, 8), 0)
        tok_specs = [pl.BlockSpec((pl.Element(tm), pl.Element(w)), window_row) for w in (D_MODEL, D_AB, D_C)]
        out_spec = pl.BlockSpec((None, tm, D_MODEL), lambda b, j: (b, j, 0))
        out_shape = jax.ShapeDtypeStruct((B, length, D_MODEL), F32)
    return pl.pallas_call(
        functools.partial(_post_kernel, final=final),
        grid=grid,
        in_specs=tok_specs + [
            pl.BlockSpec((D_MIX, D_MODEL), const2, **single),
            pl.BlockSpec((1, D_MODEL), const2),
            pl.BlockSpec((D_MODEL, D_SMALL), const2, **single),
            pl.BlockSpec((N_EXPERTS, D_MODEL, D_FF_E), const3, **single),
            pl.BlockSpec((N_EXPERTS, D_MODEL, D_FF_E), const3, **single),
            pl.BlockSpec((N_EXPERTS * D_FF_E, D_MODEL), const2, **single),
            pl.BlockSpec((1, D_MODEL), const2),
        ],
        out_specs=out_spec,
        out_shape=out_shape,
        compiler_params=pltpu.CompilerParams(
            dimension_semantics=("parallel",) * len(grid), vmem_limit_bytes=VMEM_LIMIT_BYTES),
        name="post",
    )(x, mixed_ab, y_c, w_out, n2, w_router, w_gate, w_up, w_down, fn)


def _pack_w_in(w):
    o = [0]
    for s in (D_QKV_A, D_A, H_A, H_A, D_XBC, D_B, H_B, D_C, D_C, D_C):
        o.append(o[-1] + s)
    cols = [w[:, o[0]:o[2]], w[:, o[4]:o[6]], w[:, o[7]:o[10]], w[:, o[2]:o[4]], w[:, o[6]:o[7]],
            jnp.zeros((D_MODEL, D_SMALL - 2 * H_A - H_B), w.dtype)]
    return jnp.concatenate(cols, axis=1).astype(BF16)


def _head_params(dt_bias_a, dt_bias_b, a_log_a, a_log_b, d_skip_b):
    z4 = jnp.zeros((H_A,), F32)
    tail = jnp.zeros((D_SMALL - 2 * H_A - H_B,), F32)
    rows = [
        jnp.concatenate([z4, dt_bias_a, dt_bias_b, tail]),
        jnp.concatenate([z4, a_log_a, a_log_b, tail]),
        jnp.concatenate([z4, z4, d_skip_b, tail]),
    ]
    return jnp.concatenate([jnp.stack(rows), jnp.zeros((8 - len(rows), D_SMALL), F32)], axis=0)


def _pad_hist(h):
    return jnp.pad(h, ((0, 0), (0, 0), (HIST_ROWS - h.shape[2], 0), (0, 0)))


def kernel(x_prompt, x_sample, state_gdn, state_gdn_conv, state_ssd, state_ssd_conv, state_sconv,
           meta_tokens, norm1_w, w_in, conv_a_w, a_log_a, dt_bias_a, onorm_a_w, conv_b_w, conv_b_b,
           a_log_b, dt_bias_b, d_skip_b, onorm_b_w, conv_c_w, w_out, norm2_w, w_router_g,
           w_router_e, w_gate, w_up, w_down, final_norm_w):
    bp, seq, _ = x_prompt.shape
    bs, dseq, _ = x_sample.shape
    depth = w_in.shape[0]
    pad = CHUNK - N_META
    lp = pad + N_META + seq
    ncp = lp // CHUNK

    meta = jnp.broadcast_to(meta_tokens[None], (bp, N_META, D_MODEL))
    x_p = jnp.concatenate([jnp.zeros((bp, pad, D_MODEL), F32), meta, x_prompt], axis=1)
    x_p = x_p.reshape(bp * lp, D_MODEL)
    x_s = x_sample.reshape(bs * dseq, D_MODEL)

    zero_sa = jnp.zeros((1, bp, H_A, DK_A, DV_A), F32)
    zero_sb = jnp.zeros((1, bp, H_B, P_B, N_B), F32)
    hists_s = [_pad_hist(state_gdn_conv), _pad_hist(state_ssd_conv), _pad_hist(state_sconv)]

    fn = final_norm_w.reshape(1, D_MODEL)
    hist_p = st_p = st_s = None
    for l in range(depth):
        w_packed = _pack_w_in(w_in[l])
        conv_w = (conv_a_w[l], conv_b_w[l], conv_b_b[l].reshape(1, D_XBC), conv_c_w[l])
        mix_w = (conv_w, _head_params(dt_bias_a[l], dt_bias_b[l], a_log_a[l], a_log_b[l], d_skip_b[l]),
                 onorm_a_w[l].reshape(1, DV_A), onorm_b_w[l].reshape(1, D_B))
        w_router = jnp.concatenate(
            [w_router_g[l], w_router_e[l],
             jnp.zeros((D_MODEL, D_SMALL - N_EGROUPS - N_EXPERTS), F32)], axis=1).astype(BF16)
        post_w = (w_out[l].astype(BF16), norm2_w[l].reshape(1, D_MODEL), w_router,
                  w_gate[l].astype(BF16), w_up[l].astype(BF16),
                  w_down[l].reshape(N_EXPERTS * D_FF_E, D_MODEL).astype(BF16), fn)
        final = l == depth - 1

        acts_p, hist_p = _proj_conv_call(x_p, norm1_w[l], w_packed, conv_w, hist_p, l, depth, B=bp, pad=pad)
        qkv_act, za, xbc_act, zb, yc_p, small = acts_p
        (mixed_p,), st_p = _mix_call((qkv_act, za, xbc_act, zb, small), None, zero_sa, zero_sb, 0, mix_w,
                                     st_p, l, depth, B=bp, nC=ncp, C=CHUNK, G=G_PROMPT, pad=pad, preact=True)
        window = (bp, lp, pad + N_META, seq) if final else None
        x_p = _post_call(x_p, mixed_p, yc_p, *post_w, tm=TM_POST, final=final, window=window)

        proj_s = _proj_call(x_s, norm1_w[l], w_packed, tm=TM_PROJ)
        (mixed_s, yc_s), st_s = _mix_call(proj_s, [h[l] for h in hists_s], state_gdn, state_ssd, l, mix_w,
                                          st_s, l, depth, B=bs, nC=1, C=dseq, G=G_SAMPLE, pad=0, preact=False)
        x_s = _post_call(x_s, mixed_s, yc_s, *post_w, tm=TM_POST, final=final)

    y_sample = x_s.reshape(bs, dseq, D_MODEL)
    return (x_p, y_sample, st_p[0], hist_p[0], st_p[1], hist_p[1], hist_p[2],
            st_s[0], st_s[2], st_s[1], st_s[3], st_s[4])
```

```python
import functools

import jax
import jax.numpy as jnp
from jax import lax
from jax.experimental import pallas as pl
from jax.experimental.pallas import tpu as pltpu

F32 = jnp.float32
BF16 = jnp.bfloat16

D_MODEL = 1024
N_META = 16
H_A = 4
DK_A = 128
DV_A = 128
D_QK_A = H_A * DK_A
D_A = H_A * DV_A
D_QKV_A = 2 * D_QK_A + D_A
CONV_A = 4
H_B = 8
P_B = 64
D_B = H_B * P_B
G_B = 2
N_B = 128
D_XBC = D_B + 2 * G_B * N_B
CONV_B = 4
D_C = 512
CONV_C = 3
D_AB = D_A + D_B
D_MIX = D_AB + D_C
N_EGROUPS = 4
EXP_PER_GROUP = 4
N_EXPERTS = N_EGROUPS * EXP_PER_GROUP
D_FF_E = 256
EPS = 1e-6
CHUNK = 128

D_HCB = 3 * D_C
D_SMALL = 128
PROJ_WIDTHS = (D_QKV_A, D_A, D_XBC, D_B, D_HCB, D_SMALL)
D_PROJ_PACKED = sum(PROJ_WIDTHS)
PROJ_OFFSETS = tuple(sum(PROJ_WIDTHS[:i]) for i in range(len(PROJ_WIDTHS)))
ACT_WIDTHS = (D_QKV_A, D_A, D_XBC, D_B, D_C, D_SMALL)
HIST_WIDTHS = (D_QKV_A, D_XBC, D_C)
HIST_LENS = (CONV_A - 1, CONV_B - 1, CONV_C - 1)
LANE_BETA = 0
LANE_GA = H_A
LANE_GB = 2 * H_A
HIST_ROWS = 8
G_PROMPT = 2
G_SAMPLE = 8
EXPERT_BATCH = 4
TM_PROJ = 512
TM_POST = 512
TILES_PER_PROMPT = 4
CONV_COLS = 512

VMEM_LIMIT_BYTES = 56 * 1024 * 1024


def _rmsnorm(x, w):
    return x * lax.rsqrt(jnp.mean(x * x, axis=-1, keepdims=True) + EPS) * w


def _silu(x):
    return x * jax.nn.sigmoid(x)


def _softplus(x):
    return jnp.maximum(x, 0.0) + jnp.log1p(jnp.exp(-jnp.abs(x)))


def _dot(a, b):
    return jnp.dot(a.astype(BF16), b.astype(BF16), preferred_element_type=F32)


def _dot_nt(a, b):
    return lax.dot_general(a.astype(BF16), b.astype(BF16), (((1,), (1,)), ((), ())),
                           preferred_element_type=F32)


def _dot_tn(a, b):
    return lax.dot_general(a.astype(BF16), b.astype(BF16), (((0,), (0,)), ((), ())),
                           preferred_element_type=F32)


def _cumsum_rows(g, tril_b):
    g1 = g.astype(BF16)
    r1 = g - g1.astype(F32)
    g2 = r1.astype(BF16)
    g3 = (r1 - g2.astype(F32)).astype(BF16)
    d = functools.partial(jnp.dot, preferred_element_type=F32)
    return d(tril_b, g1) + d(tril_b, g2) + d(tril_b, g3)


def _unit_lower_inverse_offdiag(nmats, row, col, C):
    ds = [-jnp.where((row >> 1) == (col >> 1), n, 0.0) for n in nmats]
    s, ls = 2, 1
    while s < C:
        same_2s = (row >> (ls + 1)) == (col >> (ls + 1))
        same_s = (row >> ls) == (col >> ls)
        es = [jnp.where(same_2s, jnp.where(same_s, 0.0, n), 0.0) for n in nmats]
        ys = [e + _dot(d, e) for d, e in zip(ds, es)]
        ds = [d - (y + _dot(y, d)) for d, y in zip(ds, ys)]
        s, ls = 2 * s, ls + 1
    return ds


def _causal_conv(hist_ref, cur, w_ref, width, C, col0=0):
    n_cols = cur.shape[1]
    x3 = jnp.concatenate([hist_ref[...], cur], axis=0).reshape((C + HIST_ROWS) // 8, 8, n_cols)
    sub = lax.broadcasted_iota(jnp.int32, x3.shape, 1)

    def shift(a3, k):
        r = pltpu.roll(a3, k, axis=1)
        from_prev_tile = jnp.concatenate([r[:1], r[:-1]], axis=0)
        return jnp.where(sub < k, from_prev_tile, r)

    w = lambda i: w_ref[i:i + 1, col0:col0 + n_cols]
    if width == 4:
        x1 = shift(x3, 1)
        y3 = (x3 * w(3) + x1 * w(2)) + shift(x3 * w(1) + x1 * w(0), 2)
    else:
        assert width == 3
        y3 = x3 * w(2) + shift(x3, 1) * w(1) + shift(x3, 2) * w(0)
    hist_ref[...] = cur[C - HIST_ROWS:C]
    return y3.reshape(C + HIST_ROWS, n_cols)[HIST_ROWS:]


def _conv_front_end(raw_qkv, raw_xbc, hcb, exts, conv_w, keep, C):
    exta, extb, extc = exts
    cwa_ref, cwb_ref, cbb_ref, cwc_ref = conv_w
    qkv_act = _silu(_causal_conv(exta, keep(raw_qkv), cwa_ref, CONV_A, C))
    xbc_act = _silu(_causal_conv(extb, keep(raw_xbc), cwb_ref, CONV_B, C) + cbb_ref[...])
    u_c = keep(hcb[:, D_C:2 * D_C] * hcb[:, 0:D_C])
    y_c = hcb[:, 2 * D_C:3 * D_C] * _causal_conv(extc, u_c, cwc_ref, CONV_C, C)
    return qkv_act, xbc_act, y_c


def _proj_kernel(x_ref, nw_ref, w_ref, *out_refs):
    hn = _rmsnorm(x_ref[...], nw_ref[...]).astype(BF16)
    for o_ref, off, width in zip(out_refs, PROJ_OFFSETS, PROJ_WIDTHS):
        o_ref[...] = jnp.dot(hn, w_ref[:, off:off + width], preferred_element_type=F32)


def _proj_call(x, norm_w, w_packed, tm):
    T = x.shape[0]
    const = lambda i: (0, 0)
    return pl.pallas_call(
        _proj_kernel,
        grid=(T // tm,),
        in_specs=[
            pl.BlockSpec((tm, D_MODEL), lambda i: (i, 0)),
            pl.BlockSpec((1, D_MODEL), const),
            pl.BlockSpec((D_MODEL, D_PROJ_PACKED), const, pipeline_mode=pl.Buffered(1)),
        ],
        out_specs=[pl.BlockSpec((tm, w), lambda i: (i, 0)) for w in PROJ_WIDTHS],
        out_shape=[jax.ShapeDtypeStruct((T, w), F32) for w in PROJ_WIDTHS],
        compiler_params=pltpu.CompilerParams(
            dimension_semantics=("parallel",), vmem_limit_bytes=VMEM_LIMIT_BYTES),
        name="proj",
    )(x, norm_w.reshape(1, D_MODEL), w_packed)


def _proj_conv_kernel(x_ref, nw_ref, w_ref, cwa_ref, cwb_ref, cbb_ref, cwc_ref, *rest, tm, pad):
    n_ext = (D_QKV_A + D_XBC + D_C) // CONV_COLS
    (qkv_o, za_o, xbc_o, zb_o, yc_o, small_o, ha_o, hb_o, hc_o) = rest[-9 - n_ext:-n_ext]
    exts = rest[-n_ext:]
    j = lax.rem(pl.program_id(0), TILES_PER_PROMPT)

    @pl.when(j == 0)
    def _():
        for ext in exts:
            ext[...] = jnp.zeros((HIST_ROWS, CONV_COLS), F32)

    valid = j * tm + lax.broadcasted_iota(jnp.int32, (tm, 1), 0) >= pad
    keep = lambda v: jnp.where(valid, v, 0.0)
    hn = _rmsnorm(x_ref[...], nw_ref[...]).astype(BF16)
    mm = lambda off, width: jnp.dot(hn, w_ref[:, off:off + width], preferred_element_type=F32)
    off_qkv, off_za, off_xbc, off_zb, off_hcb, off_small = PROJ_OFFSETS

    def carry(ext, hist_ref, n, col0):
        hist_ref[:, col0:col0 + CONV_COLS] = ext[HIST_ROWS - n:HIST_ROWS, :]

    jobs = []
    for k in range(D_QKV_A // CONV_COLS):
        def tail(raw, k=k, ext=exts[k]):
            col0 = k * CONV_COLS
            qkv_o[:, col0:col0 + CONV_COLS] = _silu(_causal_conv(ext, keep(raw), cwa_ref, CONV_A, tm, col0))
            carry(ext, ha_o, CONV_A - 1, col0)
        jobs.append((functools.partial(mm, off_qkv + k * CONV_COLS, CONV_COLS), tail))
    for k in range(D_XBC // CONV_COLS):
        def tail(raw, k=k, ext=exts[D_QKV_A // CONV_COLS + k]):
            col0 = k * CONV_COLS
            conv = _causal_conv(ext, keep(raw), cwb_ref, CONV_B, tm, col0)
            xbc_o[:, col0:col0 + CONV_COLS] = _silu(conv + cbb_ref[:, col0:col0 + CONV_COLS])
            carry(ext, hb_o, CONV_B - 1, col0)
        jobs.append((functools.partial(mm, off_xbc + k * CONV_COLS, CONV_COLS), tail))

    def tail_c(hcb, ext=exts[-1]):
        u_c = keep(hcb[:, D_C:2 * D_C] * hcb[:, 0:D_C])
        yc_o[...] = hcb[:, 2 * D_C:3 * D_C] * _causal_conv(ext, u_c, cwc_ref, CONV_C, tm)
        carry(ext, hc_o, CONV_C - 1, 0)
    jobs.append((functools.partial(mm, off_hcb, D_HCB), tail_c))

    def tail_plain(raws):
        za_o[...], zb_o[...], small_o[...] = raws
    jobs.append((lambda: (mm(off_za, D_A), mm(off_zb, D_B), mm(off_small, D_SMALL)), tail_plain))

    pending = None
    for matmuls, tail in jobs:
        raw = matmuls()
        if pending is not None:
            pending[1](pending[0])
        pending = (raw, tail)
    pending[1](pending[0])


def _proj_conv_call(x, norm_w, w_packed, conv_w, stacked, out_layer, depth, *, B, pad):
    T = x.shape[0]
    tm = T // (B * TILES_PER_PROMPT)
    const = lambda i: (0, 0)
    row = lambda i: (i, 0)
    hist = lambda i: (out_layer, i // TILES_PER_PROMPT, 0, 0)
    cwa, cwb, cbb, cwc = conv_w
    in_specs = [
        pl.BlockSpec((tm, D_MODEL), row),
        pl.BlockSpec((1, D_MODEL), const),
        pl.BlockSpec((D_MODEL, D_PROJ_PACKED), const, pipeline_mode=pl.Buffered(1)),
        pl.BlockSpec((CONV_A, D_QKV_A), const),
        pl.BlockSpec((CONV_B, D_XBC), const),
        pl.BlockSpec((1, D_XBC), const),
        pl.BlockSpec((CONV_C, D_C), const),
    ]
    stacked = list(stacked or [])
    n_in = len(in_specs)
    in_specs += [pl.BlockSpec(memory_space=pl.ANY) for _ in stacked]
    n_act = len(ACT_WIDTHS)
    outs = pl.pallas_call(
        functools.partial(_proj_conv_kernel, tm=tm, pad=pad),
        grid=(T // tm,),
        in_specs=in_specs,
        out_specs=([pl.BlockSpec((tm, w), row) for w in ACT_WIDTHS]
                   + [pl.BlockSpec((None, None, n, w), hist) for n, w in zip(HIST_LENS, HIST_WIDTHS)]),
        out_shape=([jax.ShapeDtypeStruct((T, w), F32) for w in ACT_WIDTHS]
                   + [jax.ShapeDtypeStruct((depth, B, n, w), F32) for n, w in zip(HIST_LENS, HIST_WIDTHS)]),
        scratch_shapes=[pltpu.VMEM((HIST_ROWS, CONV_COLS), F32)
                        for _ in range(sum(HIST_WIDTHS) // CONV_COLS)],
        input_output_aliases={n_in + i: n_act + i for i in range(len(stacked))},
        compiler_params=pltpu.CompilerParams(
            dimension_semantics=("arbitrary",), vmem_limit_bytes=VMEM_LIMIT_BYTES),
        name="proj_conv",
    )(x, norm_w.reshape(1, D_MODEL), w_packed, cwa, cwb, cbb, cwc, *stacked)
    return outs[:n_act], outs[n_act:]


def _mix_kernel(*refs, G, C, pad, nC, preact):
    if preact:
        qkv_ref, za_ref, xbc_ref, zb_ref, small_ref, s0a_ref, s0b_ref, hp_ref, ona_ref, onb_ref = refs[:10]
        mixed_ref, sa_out, sb_out, sa, sb = refs[-5:]
    else:
        (qkv_ref, za_ref, xbc_ref, zb_ref, hcb_ref, small_ref, hista_ref, histb_ref, histc_ref,
         s0a_ref, s0b_ref, cwa_ref, cwb_ref, cbb_ref, cwc_ref, hp_ref, ona_ref, onb_ref) = refs[:18]
        (mixed_ref, yc_ref, sa_out, sb_out, ha_out, hb_out, hc_out, exta, extb, extc, sa, sb) = refs[-12:]
        exts, hist_refs, hist_outs = (exta, extb, extc), (hista_ref, histb_ref, histc_ref), (ha_out, hb_out, hc_out)
    c = pl.program_id(1)

    @pl.when(c == 0)
    def _():
        sa[...] = s0a_ref[...]
        sb[...] = s0b_ref[...]
        if not preact:
            for ext, h_ref in zip(exts, hist_refs):
                ext[...] = h_ref[...]

    row = lax.broadcasted_iota(jnp.int32, (C, C), 0)
    col = lax.broadcasted_iota(jnp.int32, (C, C), 1)
    tril = row >= col
    strict = row > col
    tril_b = jnp.where(tril, 1.0, 0.0).astype(BF16)

    if pad:
        tok = c * C + lax.broadcasted_iota(jnp.int32, (C, 1), 0)
        valid = tok >= pad
        keep = lambda v: jnp.where(valid, v, 0.0)
    else:
        keep = lambda v: v

    seqs = range(G)
    lane_col = lambda a, lane: a[:, lane:lane + 1]

    small = [small_ref[g] for g in seqs]
    sp = [keep(_softplus(s + hp_ref[0:1, :])) for s in small]
    beta_all = [jax.nn.sigmoid(s) for s in small]
    G_all = [_cumsum_rows(-jnp.exp(hp_ref[1:2, :]) * s, tril_b) for s in sp]
    GT_all = [x.T for x in G_all]
    eG_all = [jnp.exp(x) for x in G_all]
    eG_last = [jnp.exp(x[C - 1:C, :]) for x in G_all]
    eG_rem = [jnp.exp(x[C - 1:C, :] - x) for x in G_all]

    def decay_matrix(g, lane):
        dlog = lane_col(G_all[g], lane) - GT_all[g][lane:lane + 1, :]
        return jnp.where(tril, jnp.exp(jnp.minimum(dlog, 0.0)), 0.0)

    if preact:
        qkv = [qkv_ref[g] for g in seqs]
        xbc = [xbc_ref[g] for g in seqs]
    else:
        qkv, xbc = [], []
        for g in seqs:
            qkv_act, xbc_act, y_c = _conv_front_end(
                qkv_ref[g], xbc_ref[g], hcb_ref[g], [ext.at[g] for ext in exts],
                (cwa_ref, cwb_ref, cbb_ref, cwc_ref), keep, C)
            qkv.append(qkv_act)
            xbc.append(xbc_act)
            yc_ref[g] = y_c

    chains = [(g, h) for g in seqs for h in range(H_A)]
    qs, ks, vs, kbs, egs, kdts = [], [], [], [], [], []
    for g, h in chains:
        q = qkv[g][:, h * DK_A:(h + 1) * DK_A]
        k = qkv[g][:, D_QK_A + h * DK_A:D_QK_A + (h + 1) * DK_A]
        v = qkv[g][:, 2 * D_QK_A + h * DV_A:2 * D_QK_A + (h + 1) * DV_A]
        q = q * lax.rsqrt(jnp.sum(q * q, axis=-1, keepdims=True) + EPS) * (DK_A ** -0.5)
        k = k * lax.rsqrt(jnp.sum(k * k, axis=-1, keepdims=True) + EPS)
        beta = lane_col(beta_all[g], LANE_BETA + h)
        qs.append(q)
        ks.append(k)
        vs.append(v * beta)
        kbs.append(k * beta)
        egs.append(lane_col(eG_all[g], LANE_GA + h))
        kdts.append((k * lane_col(eG_rem[g], LANE_GA + h)).T)
    kqs = [_dot_nt(jnp.concatenate([kb, q], axis=0), k) for kb, q, k in zip(kbs, qs, ks)]
    decs = [decay_matrix(g, LANE_GA + h) for g, h in chains]
    nmats = [jnp.where(strict, kq[:C] * dec, 0.0) for kq, dec in zip(kqs, decs)]
    qks = [kq[C:] * dec for kq, dec in zip(kqs, decs)]
    heads_per_group = H_B // G_B
    gw = D_B // G_B
    grp_of = lambda h: h // heads_per_group
    bgs = {(g, grp): xbc[g][:, D_B + grp * N_B:D_B + (grp + 1) * N_B] for g in seqs for grp in range(G_B)}
    cgs = {(g, grp): xbc[g][:, D_B + (G_B + grp) * N_B:D_B + (G_B + grp + 1) * N_B]
           for g in seqs for grp in range(G_B)}
    heads = [(g, h) for g in seqs for h in range(H_B)]
    x_hs = [xbc[g][:, h * P_B:(h + 1) * P_B] for g, h in heads]
    xdts = [x_h * lane_col(sp[g], LANE_GB + h) for (g, h), x_h in zip(heads, x_hs)]
    xdtts = [xdt.T for xdt in xdts]
    cdecs = [cgs[(g, grp_of(h))] * lane_col(eG_all[g], LANE_GB + h) for g, h in heads]
    bdecs = [bgs[(g, grp_of(h))] * lane_col(eG_rem[g], LANE_GB + h) for g, h in heads]
    decs_b = [decay_matrix(g, LANE_GB + h) for g, h in heads]
    tinvs = _unit_lower_inverse_offdiag(nmats, row, col, C)
    rhss = [jnp.concatenate([vb, kb * eg], axis=1) for vb, kb, eg in zip(vs, kbs, egs)]
    sols = [rhs + _dot(t, rhs) for t, rhs in zip(tinvs, rhss)]
    s_olds = [sa[g, h] for g, h in chains]
    wss = [_dot(jnp.concatenate([sol[:, DV_A:], q * eg], axis=0), s_old)
           for sol, q, eg, s_old in zip(sols, qs, egs, s_olds)]
    ws_ = [sol[:, :DV_A] - ws[:C] for sol, ws in zip(sols, wss)]
    os_ = [ws[C:] + _dot(qk, w) for ws, qk, w in zip(wss, qks, ws_)]
    for (g, h), kdt, w, s_old, o in zip(chains, kdts, ws_, s_olds, os_):
        sa[g, h] = lane_col(eG_last[g], LANE_GA + h) * s_old + _dot(kdt, w)
        gate = _silu(za_ref[g, :, h * DV_A:(h + 1) * DV_A])
        mixed_ref[g, :, h * DV_A:(h + 1) * DV_A] = _rmsnorm(o, ona_ref[...]) * gate

    cbts = {key: _dot_nt(cgs[key], bgs[key]) for key in bgs}
    h_olds = [sb[g, h] for g, h in heads]
    y_offs = [_dot_nt(cdec, h_old) for cdec, h_old in zip(cdecs, h_olds)]
    for (g, h), xdtt, bdec, h_old in zip(heads, xdtts, bdecs, h_olds):
        sb[g, h] = lane_col(eG_last[g], LANE_GB + h) * h_old + _dot(xdtt, bdec)
    ys = [y_off + _dot(cbts[(g, grp_of(h))] * dec, xdt) + hp_ref[2:3, LANE_GB + h:LANE_GB + h + 1] * x_h
          for (g, h), y_off, dec, xdt, x_h in zip(heads, y_offs, decs_b, xdts, x_hs)]
    for g in seqs:
        for grp in range(G_B):
            lo = g * H_B + grp * heads_per_group
            yg = jnp.concatenate(ys[lo:lo + heads_per_group], axis=1)
            yg = yg * _silu(zb_ref[g, :, grp * gw:(grp + 1) * gw])
            mixed_ref[g, :, D_A + grp * gw:D_A + (grp + 1) * gw] = _rmsnorm(
                yg, onb_ref[:, grp * gw:(grp + 1) * gw])

    @pl.when(c == nC - 1)
    def _():
        sa_out[...] = sa[...]
        sb_out[...] = sb[...]
        if not preact:
            for o_ref, ext, n in zip(hist_outs, exts, HIST_LENS):
                o_ref[...] = ext[:, HIST_ROWS - n:HIST_ROWS, :]


def _mix_call(acts, hists, s0a, s0b, layer, weights, stacked, out_layer, depth, *, B, nC, C, G, pad, preact):
    L = nC * C
    conv_w, hp, ona, onb = weights
    tok = lambda b, c: (b, c, 0)
    per_b3 = lambda b, c: (b, 0, 0)
    out_b4 = lambda b, c: (out_layer, b, 0, 0)
    out_b5 = lambda b, c: (out_layer, b, 0, 0, 0)
    const = lambda b, c: (0, 0)
    state_specs = [
        pl.BlockSpec((None, G, H_A, DK_A, DV_A), lambda b, c: (layer, b, 0, 0, 0)),
        pl.BlockSpec((None, G, H_B, P_B, N_B), lambda b, c: (layer, b, 0, 0, 0)),
    ]
    head_specs = [pl.BlockSpec((8, D_SMALL), const), pl.BlockSpec((1, DV_A), const), pl.BlockSpec((1, D_B), const)]
    acts3 = [a.reshape(B, L, a.shape[-1]) for a in acts]
    in_specs = [pl.BlockSpec((G, C, a.shape[-1]), tok) for a in acts3]
    operands = list(acts3)
    if not preact:
        in_specs += [pl.BlockSpec((G, HIST_ROWS, w), per_b3) for w in HIST_WIDTHS]
        operands += list(hists)
    in_specs += state_specs
    operands += [s0a, s0b]
    if not preact:
        in_specs += [pl.BlockSpec((CONV_A, D_QKV_A), const), pl.BlockSpec((CONV_B, D_XBC), const),
                     pl.BlockSpec((1, D_XBC), const), pl.BlockSpec((CONV_C, D_C), const)]
        operands += list(conv_w)
    in_specs += head_specs
    operands += [hp, ona, onb]

    out_specs = [pl.BlockSpec((G, C, D_AB), tok)]
    out_shape = [jax.ShapeDtypeStruct((B, L, D_AB), F32)]
    if not preact:
        out_specs.append(pl.BlockSpec((G, C, D_C), tok))
        out_shape.append(jax.ShapeDtypeStruct((B, L, D_C), F32))
    n_plain = len(out_specs)
    out_specs += [pl.BlockSpec((None, G, H_A, DK_A, DV_A), out_b5), pl.BlockSpec((None, G, H_B, P_B, N_B), out_b5)]
    out_shape += [jax.ShapeDtypeStruct((depth, B, H_A, DK_A, DV_A), F32),
                  jax.ShapeDtypeStruct((depth, B, H_B, P_B, N_B), F32)]
    scratch = []
    if not preact:
        out_specs += [pl.BlockSpec((None, G, n, w), out_b4) for n, w in zip(HIST_LENS, HIST_WIDTHS)]
        out_shape += [jax.ShapeDtypeStruct((depth, B, n, w), F32) for n, w in zip(HIST_LENS, HIST_WIDTHS)]
        scratch += [pltpu.VMEM((G, HIST_ROWS, w), F32) for w in HIST_WIDTHS]
    scratch += [pltpu.VMEM((G, H_A, DK_A, DV_A), F32), pltpu.VMEM((G, H_B, P_B, N_B), F32)]

    stacked = list(stacked or [])
    n_in = len(in_specs)
    in_specs += [pl.BlockSpec(memory_space=pl.ANY) for _ in stacked]
    outs = pl.pallas_call(
        functools.partial(_mix_kernel, G=G, C=C, pad=pad, nC=nC, preact=preact),
        grid=(B // G, nC),
        in_specs=in_specs,
        out_specs=out_specs,
        out_shape=out_shape,
        scratch_shapes=scratch,
        input_output_aliases={n_in + i: n_plain + i for i in range(len(stacked))},
        compiler_params=pltpu.CompilerParams(
            dimension_semantics=("parallel", "arbitrary"), vmem_limit_bytes=VMEM_LIMIT_BYTES),
        name="mix",
    )(*operands, *stacked)
    plain = [o.reshape(B * L, o.shape[-1]) for o in outs[:n_plain]]
    return plain, list(outs[n_plain:])


def _route(logits):
    lg = [logits[:, g:g + 1] for g in range(N_EGROUPS)]
    m = functools.reduce(jnp.maximum, lg)
    ex = [jnp.exp(l - m) for l in lg]
    den = functools.reduce(lambda a, b: a + b, ex)
    p = [e / den for e in ex]
    p_top, g_idx = p[0], jnp.zeros_like(p[0], dtype=jnp.int32)
    for g in range(1, N_EGROUPS):
        better = p[g] > p_top
        g_idx = jnp.where(better, g, g_idx)
        p_top = jnp.where(better, p[g], p_top)
    g_hot = [g_idx == g for g in range(N_EGROUPS)]
    le_sel = []
    for j in range(EXP_PER_GROUP):
        acc = None
        for g in range(N_EGROUPS):
            lane = N_EGROUPS + g * EXP_PER_GROUP + j
            t = jnp.where(g_hot[g], logits[:, lane:lane + 1], 0.0)
            acc = t if acc is None else acc + t
        le_sel.append(acc)
    m2 = functools.reduce(jnp.maximum, le_sel)
    ex2 = [jnp.exp(l - m2) for l in le_sel]
    den2 = functools.reduce(lambda a, b: a + b, ex2)
    s = [e / den2 for e in ex2]
    v1, i1 = s[0], jnp.zeros_like(g_idx)
    for j in range(1, EXP_PER_GROUP):
        better = s[j] > v1
        i1 = jnp.where(better, j, i1)
        v1 = jnp.where(better, s[j], v1)
    v2, i2 = jnp.full_like(v1, -1.0), jnp.zeros_like(g_idx)
    for j in range(EXP_PER_GROUP):
        s_rest = jnp.where(i1 == j, -1.0, s[j])
        better = s_rest > v2
        i2 = jnp.where(better, j, i2)
        v2 = jnp.where(better, s_rest, v2)
    tot = v1 + v2
    w1 = v1 / tot * p_top
    w2 = v2 / tot * p_top
    comb = []
    for g in range(N_EGROUPS):
        for j in range(EXP_PER_GROUP):
            inner = jnp.where(i1 == j, w1, jnp.where(i2 == j, w2, 0.0))
            comb.append(jnp.where(g_hot[g], inner, 0.0))
    return comb


def _post_kernel(x_ref, mab_ref, yc_ref, wout_ref, n2_ref, wr_ref, wg_ref, wu_ref, wd_ref, fn_ref,
                 o_ref, *, final):
    mixed = jnp.concatenate([mab_ref[...].astype(BF16), yc_ref[...].astype(BF16)], axis=1)
    x1 = x_ref[...] + jnp.dot(mixed, wout_ref[...], preferred_element_type=F32)
    hb = _rmsnorm(x1, n2_ref[...]).astype(BF16)
    comb = _route(jnp.dot(hb, wr_ref[...], preferred_element_type=F32))

    def activations(grp):
        acts = []
        for e in range(grp * EXPERT_BATCH, (grp + 1) * EXPERT_BATCH):
            hg = jnp.dot(hb, wg_ref[e], preferred_element_type=F32)
            hu = jnp.dot(hb, wu_ref[e], preferred_element_type=F32)
            acts.append((_silu(hg) * hu * comb[e]).astype(BF16))
        return jnp.concatenate(acts, axis=1)

    n_batches = N_EXPERTS // EXPERT_BATCH
    kb = EXPERT_BATCH * D_FF_E
    acts = [activations(0)]
    acc = x1
    for grp in range(n_batches):
        if grp + 1 < n_batches:
            acts.append(activations(grp + 1))
        acc = acc + jnp.dot(acts[grp], wd_ref[grp * kb:(grp + 1) * kb, :], preferred_element_type=F32)
    if final:
        acc = _rmsnorm(acc, fn_ref[...])
    o_ref[...] = acc


def _post_call(x, mixed_ab, y_c, w_out, n2, w_router, w_gate, w_up, w_down, fn, *, tm, final, window=None):
    T = x.shape[0]
    single = dict(pipeline_mode=pl.Buffered(1))
    if window is None:
        grid = (T // tm,)
        const2 = lambda i: (0, 0)
        const3 = lambda i: (0, 0, 0)
        row = lambda i: (i, 0)
        tok_specs = [pl.BlockSpec((tm, w), row) for w in (D_MODEL, D_AB, D_C)]
        out_spec = pl.BlockSpec((tm, D_MODEL), row)
        out_shape = jax.ShapeDtypeStruct((T, D_MODEL), F32)
    else:
        B, L, start, length = window
        grid = (B, length // tm)
        const2 = lambda b, j: (0, 0)
        const3 = lambda b, j: (0, 0, 0)
        window_row = lambda b, j: (pl.multiple_of(b * L + start + j * tm, 8), 0)
        tok_specs = [pl.BlockSpec((pl.Element(tm), pl.Element(w)), window_row) for w in (D_MODEL, D_AB, D_C)]
        out_spec = pl.BlockSpec((None, tm, D_MODEL), lambda b, j: (b, j, 0))
        out_shape = jax.ShapeDtypeStruct((B, length, D_MODEL), F32)
    return pl.pallas_call(
        functools.partial(_post_kernel, final=final),
        grid=grid,
        in_specs=tok_specs + [
            pl.BlockSpec((D_MIX, D_MODEL), const2, **single),
            pl.BlockSpec((1, D_MODEL), const2),
            pl.BlockSpec((D_MODEL, D_SMALL), const2, **single),
            pl.BlockSpec((N_EXPERTS, D_MODEL, D_FF_E), const3, **single),
            pl.BlockSpec((N_EXPERTS, D_MODEL, D_FF_E), const3, **single),
            pl.BlockSpec((N_EXPERTS * D_FF_E, D_MODEL), const2, **single),
            pl.BlockSpec((1, D_MODEL), const2),
        ],
        out_specs=out_spec,
        out_shape=out_shape,
        compiler_params=pltpu.CompilerParams(
            dimension_semantics=("parallel",) * len(grid), vmem_limit_bytes=VMEM_LIMIT_BYTES),
        name="post",
    )(x, mixed_ab, y_c, w_out, n2, w_router, w_gate, w_up, w_down, fn)


def _pack_w_in(w):
    o = [0]
    for s in (D_QKV_A, D_A, H_A, H_A, D_XBC, D_B, H_B, D_C, D_C, D_C):
        o.append(o[-1] + s)
    cols = [w[:, o[0]:o[2]], w[:, o[4]:o[6]], w[:, o[7]:o[10]], w[:, o[2]:o[4]], w[:, o[6]:o[7]],
            jnp.zeros((D_MODEL, D_SMALL - 2 * H_A - H_B), w.dtype)]
    return jnp.concatenate(cols, axis=1).astype(BF16)


def _head_params(dt_bias_a, dt_bias_b, a_log_a, a_log_b, d_skip_b):
    z4 = jnp.zeros((H_A,), F32)
    tail = jnp.zeros((D_SMALL - 2 * H_A - H_B,), F32)
    rows = [
        jnp.concatenate([z4, dt_bias_a, dt_bias_b, tail]),
        jnp.concatenate([z4, a_log_a, a_log_b, tail]),
        jnp.concatenate([z4, z4, d_skip_b, tail]),
    ]
    return jnp.concatenate([jnp.stack(rows), jnp.zeros((8 - len(rows), D_SMALL), F32)], axis=0)


def _pad_hist(h):
    return jnp.pad(h, ((0, 0), (0, 0), (HIST_ROWS - h.shape[2], 0), (0, 0)))


def kernel(x_prompt, x_sample, state_gdn, state_gdn_conv, state_ssd, state_ssd_conv, state_sconv,
           meta_tokens, norm1_w, w_in, conv_a_w, a_log_a, dt_bias_a, onorm_a_w, conv_b_w, conv_b_b,
           a_log_b, dt_bias_b, d_skip_b, onorm_b_w, conv_c_w, w_out, norm2_w, w_router_g,
           w_router_e, w_gate, w_up, w_down, final_norm_w):
    bp, seq, _ = x_prompt.shape
    bs, dseq, _ = x_sample.shape
    depth = w_in.shape[0]
    pad = CHUNK - N_META
    lp = pad + N_META + seq
    ncp = lp // CHUNK

    meta = jnp.broadcast_to(meta_tokens[None], (bp, N_META, D_MODEL))
    x_p = jnp.concatenate([jnp.zeros((bp, pad, D_MODEL), F32), meta, x_prompt], axis=1)
    x_p = x_p.reshape(bp * lp, D_MODEL)
    x_s = x_sample.reshape(bs * dseq, D_MODEL)

    zero_sa = jnp.zeros((1, bp, H_A, DK_A, DV_A), F32)
    zero_sb = jnp.zeros((1, bp, H_B, P_B, N_B), F32)
    hists_s = [_pad_hist(state_gdn_conv), _pad_hist(state_ssd_conv), _pad_hist(state_sconv)]

    fn = final_norm_w.reshape(1, D_MODEL)
    hist_p = st_p = st_s = None
    for l in range(depth):
        w_packed = _pack_w_in(w_in[l])
        conv_w = (conv_a_w[l], conv_b_w[l], conv_b_b[l].reshape(1, D_XBC), conv_c_w[l])
        mix_w = (conv_w, _head_params(dt_bias_a[l], dt_bias_b[l], a_log_a[l], a_log_b[l], d_skip_b[l]),
                 onorm_a_w[l].reshape(1, DV_A), onorm_b_w[l].reshape(1, D_B))
        w_router = jnp.concatenate(
            [w_router_g[l], w_router_e[l],
             jnp.zeros((D_MODEL, D_SMALL - N_EGROUPS - N_EXPERTS), F32)], axis=1).astype(BF16)
        post_w = (w_out[l].astype(BF16), norm2_w[l].reshape(1, D_MODEL), w_router,
                  w_gate[l].astype(BF16), w_up[l].astype(BF16),
                  w_down[l].reshape(N_EXPERTS * D_FF_E, D_MODEL).astype(BF16), fn)
        final = l == depth - 1

        acts_p, hist_p = _proj_conv_call(x_p, norm1_w[l], w_packed, conv_w, hist_p, l, depth, B=bp, pad=pad)
        qkv_act, za, xbc_act, zb, yc_p, small = acts_p
        (mixed_p,), st_p = _mix_call((qkv_act, za, xbc_act, zb, small), None, zero_sa, zero_sb, 0, mix_w,
                                     st_p, l, depth, B=bp, nC=ncp, C=CHUNK, G=G_PROMPT, pad=pad, preact=True)
        window = (bp, lp, pad + N_META, seq) if final else None
        x_p = _post_call(x_p, mixed_p, yc_p, *post_w, tm=TM_POST, final=final, window=window)

        proj_s = _proj_call(x_s, norm1_w[l], w_packed, tm=TM_PROJ)
        (mixed_s, yc_s), st_s = _mix_call(proj_s, [h[l] for h in hists_s], state_gdn, state_ssd, l, mix_w,
                                          st_s, l, depth, B=bs, nC=1, C=dseq, G=G_SAMPLE, pad=0, preact=False)
        x_s = _post_call(x_s, mixed_s, yc_s, *post_w, tm=TM_POST, final=final)

    y_sample = x_s.reshape(bs, dseq, D_MODEL)
    return (x_p, y_sample, st_p[0], hist_p[0], st_p[1], hist_p[1], hist_p[2],
            st_s[0], st_s[2], st_s[1], st_s[3], st_s[4])
```

```python
import functools

import jax
import jax.numpy as jnp
from jax import lax
from jax.experimental import pallas as pl
from jax.experimental.pallas import tpu as pltpu

F32 = jnp.float32
BF16 = jnp.bfloat16

D_MODEL = 1024
N_META = 16
H_A = 4
DK_A = 128
DV_A = 128
D_QK_A = H_A * DK_A
D_A = H_A * DV_A
D_QKV_A = 2 * D_QK_A + D_A
CONV_A = 4
H_B = 8
P_B = 64
D_B = H_B * P_B
G_B = 2
N_B = 128
D_XBC = D_B + 2 * G_B * N_B
CONV_B = 4
D_C = 512
CONV_C = 3
D_AB = D_A + D_B
D_MIX = D_AB + D_C
N_EGROUPS = 4
EXP_PER_GROUP = 4
N_EXPERTS = N_EGROUPS * EXP_PER_GROUP
D_FF_E = 256
EPS = 1e-6
CHUNK = 128

D_HCB = 3 * D_C
D_SMALL = 128
PROJ_WIDTHS = (D_QKV_A, D_A, D_XBC, D_B, D_HCB, D_SMALL)
D_PROJ_PACKED = sum(PROJ_WIDTHS)
PROJ_OFFSETS = tuple(sum(PROJ_WIDTHS[:i]) for i in range(len(PROJ_WIDTHS)))
ACT_WIDTHS = (D_QKV_A, D_A, D_XBC, D_B, D_C, D_SMALL)
HIST_WIDTHS = (D_QKV_A, D_XBC, D_C)
HIST_LENS = (CONV_A - 1, CONV_B - 1, CONV_C - 1)
LANE_BETA = 0
LANE_GA = H_A
LANE_GB = 2 * H_A
HIST_ROWS = 8
G_PROMPT = 2
G_SAMPLE = 8
EXPERT_BATCH = 4
TM_PROJ = 512
TM_POST = 512
TILES_PER_PROMPT = 4
CONV_COLS = 512

VMEM_LIMIT_BYTES = 56 * 1024 * 1024


def _rmsnorm(x, w):
    return x * lax.rsqrt(jnp.mean(x * x, axis=-1, keepdims=True) + EPS) * w


def _silu(x):
    h = 0.5 * x
    return h * jnp.tanh(h) + h


def _softplus(x):
    return jnp.maximum(x, 0.0) + jnp.log1p(jnp.exp(-jnp.abs(x)))


def _dot(a, b):
    return jnp.dot(a.astype(BF16), b.astype(BF16), preferred_element_type=F32)


def _dot_nt(a, b):
    return lax.dot_general(a.astype(BF16), b.astype(BF16), (((1,), (1,)), ((), ())),
                           preferred_element_type=F32)


def _dot_tn(a, b):
    return lax.dot_general(a.astype(BF16), b.astype(BF16), (((0,), (0,)), ((), ())),
                           preferred_element_type=F32)


def _cumsum_rows(g, tril_b):
    g1 = g.astype(BF16)
    r1 = g - g1.astype(F32)
    g2 = r1.astype(BF16)
    g3 = (r1 - g2.astype(F32)).astype(BF16)
    d = functools.partial(jnp.dot, preferred_element_type=F32)
    return d(tril_b, g1) + d(tril_b, g2) + d(tril_b, g3)


def _unit_lower_inverse_offdiag(nmats, row, col, C):
    ds = [-jnp.where((row >> 1) == (col >> 1), n, 0.0) for n in nmats]
    s, ls = 2, 1
    while s < C:
        same_2s = (row >> (ls + 1)) == (col >> (ls + 1))
        same_s = (row >> ls) == (col >> ls)
        es = [jnp.where(same_2s, jnp.where(same_s, 0.0, n), 0.0) for n in nmats]
        ys = [e + _dot(d, e) for d, e in zip(ds, es)]
        ds = [d - (y + _dot(y, d)) for d, y in zip(ds, ys)]
        s, ls = 2 * s, ls + 1
    return ds


def _causal_conv(hist_ref, cur, w_ref, width, C, col0=0):
    n_cols = cur.shape[1]
    x3 = jnp.concatenate([hist_ref[...], cur], axis=0).reshape((C + HIST_ROWS) // 8, 8, n_cols)
    sub = lax.broadcasted_iota(jnp.int32, x3.shape, 1)

    def shift(a3, k):
        r = pltpu.roll(a3, k, axis=1)
        from_prev_tile = jnp.concatenate([r[:1], r[:-1]], axis=0)
        return jnp.where(sub < k, from_prev_tile, r)

    w = lambda i: w_ref[i:i + 1, col0:col0 + n_cols]
    if width == 4:
        x1 = shift(x3, 1)
        y3 = (x3 * w(3) + x1 * w(2)) + shift(x3 * w(1) + x1 * w(0), 2)
    else:
        assert width == 3
        y3 = x3 * w(2) + shift(x3, 1) * w(1) + shift(x3, 2) * w(0)
    hist_ref[...] = cur[C - HIST_ROWS:C]
    return y3.reshape(C + HIST_ROWS, n_cols)[HIST_ROWS:]


def _conv_front_end(raw_qkv, raw_xbc, hcb, exts, conv_w, keep, C):
    exta, extb, extc = exts
    cwa_ref, cwb_ref, cbb_ref, cwc_ref = conv_w
    qkv_act = _silu(_causal_conv(exta, keep(raw_qkv), cwa_ref, CONV_A, C))
    xbc_act = _silu(_causal_conv(extb, keep(raw_xbc), cwb_ref, CONV_B, C) + cbb_ref[...])
    u_c = keep(hcb[:, D_C:2 * D_C] * hcb[:, 0:D_C])
    y_c = hcb[:, 2 * D_C:3 * D_C] * _causal_conv(extc, u_c, cwc_ref, CONV_C, C)
    return qkv_act, xbc_act, y_c


def _proj_kernel(x_ref, nw_ref, w_ref, *out_refs):
    hn = _rmsnorm(x_ref[...], nw_ref[...]).astype(BF16)
    for o_ref, off, width in zip(out_refs, PROJ_OFFSETS, PROJ_WIDTHS):
        o_ref[...] = jnp.dot(hn, w_ref[:, off:off + width], preferred_element_type=F32)


def _proj_call(x, norm_w, w_packed, layer, tm):
    T = x.shape[0]
    const = lambda i: (0, 0)
    return pl.pallas_call(
        _proj_kernel,
        grid=(T // tm,),
        in_specs=[
            pl.BlockSpec((tm, D_MODEL), lambda i: (i, 0)),
            pl.BlockSpec((1, D_MODEL), const),
            pl.BlockSpec((None, D_MODEL, D_PROJ_PACKED), lambda i: (layer, 0, 0), pipeline_mode=pl.Buffered(1)),
        ],
        out_specs=[pl.BlockSpec((tm, w), lambda i: (i, 0)) for w in PROJ_WIDTHS],
        out_shape=[jax.ShapeDtypeStruct((T, w), F32) for w in PROJ_WIDTHS],
        compiler_params=pltpu.CompilerParams(
            dimension_semantics=("parallel",), vmem_limit_bytes=VMEM_LIMIT_BYTES),
        name="proj",
    )(x, norm_w.reshape(1, D_MODEL), w_packed)


def _proj_conv_kernel(x_ref, nw_ref, w_ref, cwa_ref, cwb_ref, cbb_ref, cwc_ref, *rest, tm, pad):
    n_ext = (D_QKV_A + D_XBC + D_C) // CONV_COLS
    (qkv_o, za_o, xbc_o, zb_o, yc_o, small_o, ha_o, hb_o, hc_o) = rest[-9 - n_ext:-n_ext]
    exts = rest[-n_ext:]
    j = lax.rem(pl.program_id(0), TILES_PER_PROMPT)

    @pl.when(j == 0)
    def _():
        for ext in exts:
            ext[...] = jnp.zeros((HIST_ROWS, CONV_COLS), F32)

    valid = j * tm + lax.broadcasted_iota(jnp.int32, (tm, 1), 0) >= pad
    hn = _rmsnorm(jnp.where(valid, x_ref[...], 0.0), nw_ref[...]).astype(BF16)
    mm = lambda off, width: jnp.dot(hn, w_ref[:, off:off + width], preferred_element_type=F32)
    off_qkv, off_za, off_xbc, off_zb, off_hcb, off_small = PROJ_OFFSETS

    def carry(ext, hist_ref, n, col0):
        hist_ref[:, col0:col0 + CONV_COLS] = ext[HIST_ROWS - n:HIST_ROWS, :]

    jobs = []
    for k in range(D_QKV_A // CONV_COLS):
        def tail(raw, k=k, ext=exts[k]):
            col0 = k * CONV_COLS
            qkv_o[:, col0:col0 + CONV_COLS] = _silu(_causal_conv(ext, raw, cwa_ref, CONV_A, tm, col0))
            carry(ext, ha_o, CONV_A - 1, col0)
        jobs.append((functools.partial(mm, off_qkv + k * CONV_COLS, CONV_COLS), tail))
    for k in range(D_XBC // CONV_COLS):
        def tail(raw, k=k, ext=exts[D_QKV_A // CONV_COLS + k]):
            col0 = k * CONV_COLS
            conv = _causal_conv(ext, raw, cwb_ref, CONV_B, tm, col0)
            xbc_o[:, col0:col0 + CONV_COLS] = _silu(conv + cbb_ref[:, col0:col0 + CONV_COLS])
            carry(ext, hb_o, CONV_B - 1, col0)
        jobs.append((functools.partial(mm, off_xbc + k * CONV_COLS, CONV_COLS), tail))

    def tail_c(hcb, ext=exts[-1]):
        u_c = hcb[:, D_C:2 * D_C] * hcb[:, 0:D_C]
        yc_o[...] = hcb[:, 2 * D_C:3 * D_C] * _causal_conv(ext, u_c, cwc_ref, CONV_C, tm)
        carry(ext, hc_o, CONV_C - 1, 0)
    jobs.append((functools.partial(mm, off_hcb, D_HCB), tail_c))

    def tail_plain(raws):
        za_o[...], zb_o[...], small_o[...] = raws
    jobs.append((lambda: (mm(off_za, D_A), mm(off_zb, D_B), mm(off_small, D_SMALL)), tail_plain))

    pending = None
    for matmuls, tail in jobs:
        raw = matmuls()
        if pending is not None:
            pending[1](pending[0])
        pending = (raw, tail)
    pending[1](pending[0])


def _proj_conv_call(x, norm_w, w_packed, layer, conv_w, stacked, out_layer, depth, *, B, pad):
    T = x.shape[0]
    tm = T // (B * TILES_PER_PROMPT)
    const = lambda i: (0, 0)
    row = lambda i: (i, 0)
    hist = lambda i: (out_layer, i // TILES_PER_PROMPT, 0, 0)
    cwa, cwb, cbb, cwc = conv_w
    in_specs = [
        pl.BlockSpec((tm, D_MODEL), row),
        pl.BlockSpec((1, D_MODEL), const),
        pl.BlockSpec((None, D_MODEL, D_PROJ_PACKED), lambda i: (layer, 0, 0), pipeline_mode=pl.Buffered(1)),
        pl.BlockSpec((CONV_A, D_QKV_A), const),
        pl.BlockSpec((CONV_B, D_XBC), const),
        pl.BlockSpec((1, D_XBC), const),
        pl.BlockSpec((CONV_C, D_C), const),
    ]
    stacked = list(stacked or [])
    n_in = len(in_specs)
    in_specs += [pl.BlockSpec(memory_space=pl.ANY) for _ in stacked]
    n_act = len(ACT_WIDTHS)
    outs = pl.pallas_call(
        functools.partial(_proj_conv_kernel, tm=tm, pad=pad),
        grid=(T // tm,),
        in_specs=in_specs,
        out_specs=([pl.BlockSpec((tm, w), row) for w in ACT_WIDTHS]
                   + [pl.BlockSpec((None, None, n, w), hist) for n, w in zip(HIST_LENS, HIST_WIDTHS)]),
        out_shape=([jax.ShapeDtypeStruct((T, w), F32) for w in ACT_WIDTHS]
                   + [jax.ShapeDtypeStruct((depth, B, n, w), F32) for n, w in zip(HIST_LENS, HIST_WIDTHS)]),
        scratch_shapes=[pltpu.VMEM((HIST_ROWS, CONV_COLS), F32)
                        for _ in range(sum(HIST_WIDTHS) // CONV_COLS)],
        input_output_aliases={n_in + i: n_act + i for i in range(len(stacked))},
        compiler_params=pltpu.CompilerParams(
            dimension_semantics=("arbitrary",), vmem_limit_bytes=VMEM_LIMIT_BYTES),
        name="proj_conv",
    )(x, norm_w.reshape(1, D_MODEL), w_packed, cwa, cwb, cbb, cwc, *stacked)
    return outs[:n_act], outs[n_act:]


def _mix_kernel(*refs, G, C, pad, nC, preact):
    if preact:
        qkv_ref, za_ref, xbc_ref, zb_ref, small_ref, s0a_ref, s0b_ref, hp_ref, ona_ref, onb_ref = refs[:10]
        mixed_ref, sa_out, sb_out, sa, sb = refs[-5:]
    else:
        (qkv_ref, za_ref, xbc_ref, zb_ref, hcb_ref, small_ref, hista_ref, histb_ref, histc_ref,
         s0a_ref, s0b_ref, cwa_ref, cwb_ref, cbb_ref, cwc_ref, hp_ref, ona_ref, onb_ref) = refs[:18]
        (mixed_ref, yc_ref, sa_out, sb_out, ha_out, hb_out, hc_out, exta, extb, extc, sa, sb) = refs[-12:]
        exts, hist_refs, hist_outs = (exta, extb, extc), (hista_ref, histb_ref, histc_ref), (ha_out, hb_out, hc_out)
    c = pl.program_id(1)

    @pl.when(c == 0)
    def _():
        sa[...] = s0a_ref[...]
        sb[...] = s0b_ref[...]
        if not preact:
            for ext, h_ref in zip(exts, hist_refs):
                ext[...] = h_ref[...]

    row = lax.broadcasted_iota(jnp.int32, (C, C), 0)
    col = lax.broadcasted_iota(jnp.int32, (C, C), 1)
    tril = row >= col
    strict = row > col
    tril_b = jnp.where(tril, 1.0, 0.0).astype(BF16)

    if pad:
        tok = c * C + lax.broadcasted_iota(jnp.int32, (C, 1), 0)
        valid = tok >= pad
        keep = lambda v: jnp.where(valid, v, 0.0)
    else:
        keep = lambda v: v

    seqs = range(G)
    lane_col = lambda a, lane: a[:, lane:lane + 1]

    small = [small_ref[g] for g in seqs]
    sp = [keep(_softplus(s + hp_ref[0:1, :])) for s in small]
    beta_all = [jax.nn.sigmoid(s) for s in small]
    G_all = [_cumsum_rows(-jnp.exp(hp_ref[1:2, :]) * s, tril_b) for s in sp]
    GT_all = [x.T for x in G_all]
    eG_all = [jnp.exp(x) for x in G_all]
    eG_last = [jnp.exp(x[C - 1:C, :]) for x in G_all]
    eG_rem = [jnp.exp(x[C - 1:C, :] - x) for x in G_all]

    def decay_matrix(g, lane):
        dlog = lane_col(G_all[g], lane) - GT_all[g][lane:lane + 1, :]
        return jnp.where(tril, jnp.exp(jnp.minimum(dlog, 0.0)), 0.0)

    if preact:
        qkv = [qkv_ref[g] for g in seqs]
        xbc = [xbc_ref[g] for g in seqs]
    else:
        qkv, xbc = [], []
        for g in seqs:
            qkv_act, xbc_act, y_c = _conv_front_end(
                qkv_ref[g], xbc_ref[g], hcb_ref[g], [ext.at[g] for ext in exts],
                (cwa_ref, cwb_ref, cbb_ref, cwc_ref), keep, C)
            qkv.append(qkv_act)
            xbc.append(xbc_act)
            yc_ref[g] = y_c

    chains = [(g, h) for g in seqs for h in range(H_A)]
    qs, ks, vs, kbs, egs, kdts = [], [], [], [], [], []
    for g, h in chains:
        q = qkv[g][:, h * DK_A:(h + 1) * DK_A]
        k = qkv[g][:, D_QK_A + h * DK_A:D_QK_A + (h + 1) * DK_A]
        v = qkv[g][:, 2 * D_QK_A + h * DV_A:2 * D_QK_A + (h + 1) * DV_A]
        q = q * lax.rsqrt(jnp.sum(q * q, axis=-1, keepdims=True) + EPS) * (DK_A ** -0.5)
        k = k * lax.rsqrt(jnp.sum(k * k, axis=-1, keepdims=True) + EPS)
        beta = lane_col(beta_all[g], LANE_BETA + h)
        qs.append(q)
        ks.append(k)
        vs.append(v * beta)
        kbs.append(k * beta)
        egs.append(lane_col(eG_all[g], LANE_GA + h))
        kdts.append((k * lane_col(eG_rem[g], LANE_GA + h)).T)
    kqs = [_dot_nt(jnp.concatenate([kb, q], axis=0), k) for kb, q, k in zip(kbs, qs, ks)]
    decs = [decay_matrix(g, LANE_GA + h) for g, h in chains]
    nmats = [jnp.where(strict, kq[:C] * dec, 0.0) for kq, dec in zip(kqs, decs)]
    qks = [kq[C:] * dec for kq, dec in zip(kqs, decs)]
    heads_per_group = H_B // G_B
    gw = D_B // G_B
    grp_of = lambda h: h // heads_per_group
    bgs = {(g, grp): xbc[g][:, D_B + grp * N_B:D_B + (grp + 1) * N_B] for g in seqs for grp in range(G_B)}
    cgs = {(g, grp): xbc[g][:, D_B + (G_B + grp) * N_B:D_B + (G_B + grp + 1) * N_B]
           for g in seqs for grp in range(G_B)}
    heads = [(g, h) for g in seqs for h in range(H_B)]
    x_hs = [xbc[g][:, h * P_B:(h + 1) * P_B] for g, h in heads]
    xdts = [x_h * lane_col(sp[g], LANE_GB + h) for (g, h), x_h in zip(heads, x_hs)]
    xdtts = [xdt.T for xdt in xdts]
    cdecs = [cgs[(g, grp_of(h))] * lane_col(eG_all[g], LANE_GB + h) for g, h in heads]
    bdecs = [bgs[(g, grp_of(h))] * lane_col(eG_rem[g], LANE_GB + h) for g, h in heads]
    decs_b = [decay_matrix(g, LANE_GB + h) for g, h in heads]
    tinvs = _unit_lower_inverse_offdiag(nmats, row, col, C)
    rhss = [jnp.concatenate([vb, kb * eg], axis=1) for vb, kb, eg in zip(vs, kbs, egs)]
    sols = [rhs + _dot(t, rhs) for t, rhs in zip(tinvs, rhss)]
    s_olds = [sa[g, h] for g, h in chains]
    wss = [_dot(jnp.concatenate([sol[:, DV_A:], q * eg], axis=0), s_old)
           for sol, q, eg, s_old in zip(sols, qs, egs, s_olds)]
    ws_ = [sol[:, :DV_A] - ws[:C] for sol, ws in zip(sols, wss)]
    os_ = [ws[C:] + _dot(qk, w) for ws, qk, w in zip(wss, qks, ws_)]
    for (g, h), kdt, w, s_old, o in zip(chains, kdts, ws_, s_olds, os_):
        sa[g, h] = lane_col(eG_last[g], LANE_GA + h) * s_old + _dot(kdt, w)
        gate = _silu(za_ref[g, :, h * DV_A:(h + 1) * DV_A])
        mixed_ref[g, :, h * DV_A:(h + 1) * DV_A] = _rmsnorm(o, ona_ref[...]) * gate

    cbts = {key: _dot_nt(cgs[key], bgs[key]) for key in bgs}
    h_olds = [sb[g, h] for g, h in heads]
    y_offs = [_dot_nt(cdec, h_old) for cdec, h_old in zip(cdecs, h_olds)]
    for (g, h), xdtt, bdec, h_old in zip(heads, xdtts, bdecs, h_olds):
        sb[g, h] = lane_col(eG_last[g], LANE_GB + h) * h_old + _dot(xdtt, bdec)
    ys = [y_off + _dot(cbts[(g, grp_of(h))] * dec, xdt) + hp_ref[2:3, LANE_GB + h:LANE_GB + h + 1] * x_h
          for (g, h), y_off, dec, xdt, x_h in zip(heads, y_offs, decs_b, xdts, x_hs)]
    for g in seqs:
        for grp in range(G_B):
            lo = g * H_B + grp * heads_per_group
            yg = jnp.concatenate(ys[lo:lo + heads_per_group], axis=1)
            yg = yg * _silu(zb_ref[g, :, grp * gw:(grp + 1) * gw])
            mixed_ref[g, :, D_A + grp * gw:D_A + (grp + 1) * gw] = _rmsnorm(
                yg, onb_ref[:, grp * gw:(grp + 1) * gw])

    @pl.when(c == nC - 1)
    def _():
        sa_out[...] = sa[...]
        sb_out[...] = sb[...]
        if not preact:
            for o_ref, ext, n in zip(hist_outs, exts, HIST_LENS):
                o_ref[...] = ext[:, HIST_ROWS - n:HIST_ROWS, :]


def _mix_call(acts, hists, s0a, s0b, layer, weights, stacked, out_layer, depth, *, B, nC, C, G, pad, preact):
    L = nC * C
    conv_w, hp, ona, onb = weights
    tok = lambda b, c: (b, c, 0)
    per_b3 = lambda b, c: (b, 0, 0)
    out_b4 = lambda b, c: (out_layer, b, 0, 0)
    out_b5 = lambda b, c: (out_layer, b, 0, 0, 0)
    const = lambda b, c: (0, 0)
    state_specs = [
        pl.BlockSpec((None, G, H_A, DK_A, DV_A), lambda b, c: (layer, b, 0, 0, 0)),
        pl.BlockSpec((None, G, H_B, P_B, N_B), lambda b, c: (layer, b, 0, 0, 0)),
    ]
    head_specs = [pl.BlockSpec((8, D_SMALL), const), pl.BlockSpec((1, DV_A), const), pl.BlockSpec((1, D_B), const)]
    acts3 = [a.reshape(B, L, a.shape[-1]) for a in acts]
    in_specs = [pl.BlockSpec((G, C, a.shape[-1]), tok) for a in acts3]
    operands = list(acts3)
    if not preact:
        in_specs += [pl.BlockSpec((G, HIST_ROWS, w), per_b3) for w in HIST_WIDTHS]
        operands += list(hists)
    in_specs += state_specs
    operands += [s0a, s0b]
    if not preact:
        in_specs += [pl.BlockSpec((CONV_A, D_QKV_A), const), pl.BlockSpec((CONV_B, D_XBC), const),
                     pl.BlockSpec((1, D_XBC), const), pl.BlockSpec((CONV_C, D_C), const)]
        operands += list(conv_w)
    in_specs += head_specs
    operands += [hp, ona, onb]

    out_specs = [pl.BlockSpec((G, C, D_AB), tok)]
    out_shape = [jax.ShapeDtypeStruct((B, L, D_AB), F32)]
    if not preact:
        out_specs.append(pl.BlockSpec((G, C, D_C), tok))
        out_shape.append(jax.ShapeDtypeStruct((B, L, D_C), F32))
    n_plain = len(out_specs)
    out_specs += [pl.BlockSpec((None, G, H_A, DK_A, DV_A), out_b5), pl.BlockSpec((None, G, H_B, P_B, N_B), out_b5)]
    out_shape += [jax.ShapeDtypeStruct((depth, B, H_A, DK_A, DV_A), F32),
                  jax.ShapeDtypeStruct((depth, B, H_B, P_B, N_B), F32)]
    scratch = []
    if not preact:
        out_specs += [pl.BlockSpec((None, G, n, w), out_b4) for n, w in zip(HIST_LENS, HIST_WIDTHS)]
        out_shape += [jax.ShapeDtypeStruct((depth, B, n, w), F32) for n, w in zip(HIST_LENS, HIST_WIDTHS)]
        scratch += [pltpu.VMEM((G, HIST_ROWS, w), F32) for w in HIST_WIDTHS]
    scratch += [pltpu.VMEM((G, H_A, DK_A, DV_A), F32), pltpu.VMEM((G, H_B, P_B, N_B), F32)]

    stacked = list(stacked or [])
    n_in = len(in_specs)
    in_specs += [pl.BlockSpec(memory_space=pl.ANY) for _ in stacked]
    outs = pl.pallas_call(
        functools.partial(_mix_kernel, G=G, C=C, pad=pad, nC=nC, preact=preact),
        grid=(B // G, nC),
        in_specs=in_specs,
        out_specs=out_specs,
        out_shape=out_shape,
        scratch_shapes=scratch,
        input_output_aliases={n_in + i: n_plain + i for i in range(len(stacked))},
        compiler_params=pltpu.CompilerParams(
            dimension_semantics=("parallel", "arbitrary"), vmem_limit_bytes=VMEM_LIMIT_BYTES),
        name="mix",
    )(*operands, *stacked)
    plain = [o.reshape(B * L, o.shape[-1]) for o in outs[:n_plain]]
    return plain, list(outs[n_plain:])


def _route(logits):
    lg = [logits[:, g:g + 1] for g in range(N_EGROUPS)]
    m = functools.reduce(jnp.maximum, lg)
    ex = [jnp.exp(l - m) for l in lg]
    den = functools.reduce(lambda a, b: a + b, ex)
    p = [e / den for e in ex]
    p_top, g_idx = p[0], jnp.zeros_like(p[0], dtype=jnp.int32)
    for g in range(1, N_EGROUPS):
        better = p[g] > p_top
        g_idx = jnp.where(better, g, g_idx)
        p_top = jnp.where(better, p[g], p_top)
    g_hot = [g_idx == g for g in range(N_EGROUPS)]
    le_sel = []
    for j in range(EXP_PER_GROUP):
        acc = None
        for g in range(N_EGROUPS):
            lane = N_EGROUPS + g * EXP_PER_GROUP + j
            t = jnp.where(g_hot[g], logits[:, lane:lane + 1], 0.0)
            acc = t if acc is None else acc + t
        le_sel.append(acc)
    m2 = functools.reduce(jnp.maximum, le_sel)
    ex2 = [jnp.exp(l - m2) for l in le_sel]
    den2 = functools.reduce(lambda a, b: a + b, ex2)
    s = [e / den2 for e in ex2]
    v1, i1 = s[0], jnp.zeros_like(g_idx)
    for j in range(1, EXP_PER_GROUP):
        better = s[j] > v1
        i1 = jnp.where(better, j, i1)
        v1 = jnp.where(better, s[j], v1)
    v2, i2 = jnp.full_like(v1, -1.0), jnp.zeros_like(g_idx)
    for j in range(EXP_PER_GROUP):
        s_rest = jnp.where(i1 == j, -1.0, s[j])
        better = s_rest > v2
        i2 = jnp.where(better, j, i2)
        v2 = jnp.where(better, s_rest, v2)
    tot = v1 + v2
    w1 = v1 / tot * p_top
    w2 = v2 / tot * p_top
    comb = []
    for g in range(N_EGROUPS):
        for j in range(EXP_PER_GROUP):
            inner = jnp.where(i1 == j, w1, jnp.where(i2 == j, w2, 0.0))
            comb.append(jnp.where(g_hot[g], inner, 0.0))
    return comb


def _post_kernel(x_ref, mab_ref, yc_ref, wout_ref, n2_ref, wr_ref, wg_ref, wu_ref, wd_ref, fn_ref,
                 o_ref, *, final):
    mixed = jnp.concatenate([mab_ref[...].astype(BF16), yc_ref[...].astype(BF16)], axis=1)
    x1 = x_ref[...] + jnp.dot(mixed, wout_ref[...], preferred_element_type=F32)
    hb = _rmsnorm(x1, n2_ref[...]).astype(BF16)
    comb = _route(jnp.dot(hb, wr_ref[...], preferred_element_type=F32))

    def activations(grp):
        acts = []
        for e in range(grp * EXPERT_BATCH, (grp + 1) * EXPERT_BATCH):
            hg = jnp.dot(hb, wg_ref[e], preferred_element_type=F32)
            hu = jnp.dot(hb, wu_ref[e], preferred_element_type=F32)
            acts.append((_silu(hg) * hu * comb[e]).astype(BF16))
        return jnp.concatenate(acts, axis=1)

    n_batches = N_EXPERTS // EXPERT_BATCH
    kb = EXPERT_BATCH * D_FF_E
    acts = [activations(0)]
    acc = x1
    for grp in range(n_batches):
        if grp + 1 < n_batches:
            acts.append(activations(grp + 1))
        acc = acc + jnp.dot(acts[grp], wd_ref[grp * kb:(grp + 1) * kb, :], preferred_element_type=F32)
    if final:
        acc = _rmsnorm(acc, fn_ref[...])
    o_ref[...] = acc


def _post_call(x, mixed_ab, y_c, layer, w_out, n2, w_router, w_gate, w_up, w_down, fn, *, tm, final, window=None):
    single = dict(pipeline_mode=pl.Buffered(1))
    const2 = lambda s: (0, 0)
    layer3 = lambda s: (layer, 0, 0)
    layer4 = lambda s: (layer, 0, 0, 0)
    if window is None:
        n_tiles = x.shape[0] // tm
        tok_specs = [pl.BlockSpec((tm, w), lambda s: (s, 0)) for w in (D_MODEL, D_AB, D_C)]
    else:
        B, L, start, length = window
        per_seq = length // tm
        n_tiles = B * per_seq
        window_row = lambda s: (pl.multiple_of((s // per_seq) * L + start + (s % per_seq) * tm, 8), 0)
        tok_specs = [pl.BlockSpec((pl.Element(tm), pl.Element(w)), window_row) for w in (D_MODEL, D_AB, D_C)]
    return pl.pallas_call(
        functools.partial(_post_kernel, final=final),
        grid=(n_tiles,),
        in_specs=tok_specs + [
            pl.BlockSpec((None, D_MIX, D_MODEL), layer3, **single),
            pl.BlockSpec((1, D_MODEL), const2),
            pl.BlockSpec((None, D_MODEL, D_SMALL), layer3, **single),
            pl.BlockSpec((None, N_EXPERTS, D_MODEL, D_FF_E), layer4, **single),
            pl.BlockSpec((None, N_EXPERTS, D_MODEL, D_FF_E), layer4, **single),
            pl.BlockSpec((None, N_EXPERTS * D_FF_E, D_MODEL), layer3, **single),
            pl.BlockSpec((1, D_MODEL), const2),
        ],
        out_specs=pl.BlockSpec((tm, D_MODEL), lambda s: (s, 0)),
        out_shape=jax.ShapeDtypeStruct((n_tiles * tm, D_MODEL), F32),
        compiler_params=pltpu.CompilerParams(
            dimension_semantics=("parallel",), vmem_limit_bytes=VMEM_LIMIT_BYTES),
        name="post",
    )(x, mixed_ab, y_c, w_out, n2, w_router, w_gate, w_up, w_down, fn)


def _pack_w_in(w):
    o = [0]
    for s in (D_QKV_A, D_A, H_A, H_A, D_XBC, D_B, H_B, D_C, D_C, D_C):
        o.append(o[-1] + s)
    cols = [w[..., o[0]:o[2]], w[..., o[4]:o[6]], w[..., o[7]:o[10]], w[..., o[2]:o[4]], w[..., o[6]:o[7]],
            jnp.zeros(w.shape[:2] + (D_SMALL - 2 * H_A - H_B,), w.dtype)]
    return jnp.concatenate(cols, axis=-1).astype(BF16)


def _head_params(dt_bias_a, dt_bias_b, a_log_a, a_log_b, d_skip_b):
    z4 = jnp.zeros((H_A,), F32)
    tail = jnp.zeros((D_SMALL - 2 * H_A - H_B,), F32)
    rows = [
        jnp.concatenate([z4, dt_bias_a, dt_bias_b, tail]),
        jnp.concatenate([z4, a_log_a, a_log_b, tail]),
        jnp.concatenate([z4, z4, d_skip_b, tail]),
    ]
    return jnp.concatenate([jnp.stack(rows), jnp.zeros((8 - len(rows), D_SMALL), F32)], axis=0)


def _pad_hist(h):
    return jnp.pad(h, ((0, 0), (0, 0), (HIST_ROWS - h.shape[2], 0), (0, 0)))


def kernel(x_prompt, x_sample, state_gdn, state_gdn_conv, state_ssd, state_ssd_conv, state_sconv,
           meta_tokens, norm1_w, w_in, conv_a_w, a_log_a, dt_bias_a, onorm_a_w, conv_b_w, conv_b_b,
           a_log_b, dt_bias_b, d_skip_b, onorm_b_w, conv_c_w, w_out, norm2_w, w_router_g,
           w_router_e, w_gate, w_up, w_down, final_norm_w):
    bp, seq, _ = x_prompt.shape
    bs, dseq, _ = x_sample.shape
    depth = w_in.shape[0]
    pad = CHUNK - N_META
    lp = pad + N_META + seq
    ncp = lp // CHUNK

    meta = jnp.broadcast_to(meta_tokens[None], (bp, N_META, D_MODEL))
    x_p = jnp.concatenate([jnp.zeros((bp, pad, D_MODEL), F32), meta, x_prompt], axis=1)
    x_p = x_p.reshape(bp * lp, D_MODEL)
    x_s = x_sample.reshape(bs * dseq, D_MODEL)

    zero_sa = jnp.zeros((1, bp, H_A, DK_A, DV_A), F32)
    zero_sb = jnp.zeros((1, bp, H_B, P_B, N_B), F32)
    hists_s = [_pad_hist(state_gdn_conv), _pad_hist(state_ssd_conv), _pad_hist(state_sconv)]

    fn = final_norm_w.reshape(1, D_MODEL)
    w_packed = _pack_w_in(w_in)
    w_router = jnp.concatenate(
        [w_router_g, w_router_e, jnp.zeros((depth, D_MODEL, D_SMALL - N_EGROUPS - N_EXPERTS), F32)],
        axis=-1).astype(BF16)
    w_out_b, w_gate_b, w_up_b = w_out.astype(BF16), w_gate.astype(BF16), w_up.astype(BF16)
    w_down_b = w_down.reshape(depth, N_EXPERTS * D_FF_E, D_MODEL).astype(BF16)
    hist_p = st_p = st_s = None
    for l in range(depth):
        conv_w = (conv_a_w[l], conv_b_w[l], conv_b_b[l].reshape(1, D_XBC), conv_c_w[l])
        mix_w = (conv_w, _head_params(dt_bias_a[l], dt_bias_b[l], a_log_a[l], a_log_b[l], d_skip_b[l]),
                 onorm_a_w[l].reshape(1, DV_A), onorm_b_w[l].reshape(1, D_B))
        post_w = (l, w_out_b, norm2_w[l].reshape(1, D_MODEL), w_router, w_gate_b, w_up_b, w_down_b, fn)
        final = l == depth - 1

        acts_p, hist_p = _proj_conv_call(x_p, norm1_w[l], w_packed, l, conv_w, hist_p, l, depth, B=bp, pad=pad)
        qkv_act, za, xbc_act, zb, yc_p, small = acts_p
        (mixed_p,), st_p = _mix_call((qkv_act, za, xbc_act, zb, small), None, zero_sa, zero_sb, 0, mix_w,
                                     st_p, l, depth, B=bp, nC=ncp, C=CHUNK, G=G_PROMPT, pad=pad, preact=True)
        window = (bp, lp, pad + N_META, seq) if final else None
        x_p = _post_call(x_p, mixed_p, yc_p, *post_w, tm=TM_POST, final=final, window=window)

        proj_s = _proj_call(x_s, norm1_w[l], w_packed, l, tm=TM_PROJ)
        (mixed_s, yc_s), st_s = _mix_call(proj_s, [h[l] for h in hists_s], state_gdn, state_ssd, l, mix_w,
                                          st_s, l, depth, B=bs, nC=1, C=dseq, G=G_SAMPLE, pad=0, preact=False)
        x_s = _post_call(x_s, mixed_s, yc_s, *post_w, tm=TM_POST, final=final)

    y_sample = x_s.reshape(bs, dseq, D_MODEL)
    return (x_p.reshape(bp, seq, D_MODEL), y_sample, st_p[0], hist_p[0], st_p[1], hist_p[1], hist_p[2],
            st_s[0], st_s[2], st_s[1], st_s[3], st_s[4])
```

```python
import functools

import jax
import jax.numpy as jnp
from jax import lax
from jax.experimental import pallas as pl
from jax.experimental.pallas import tpu as pltpu

F32 = jnp.float32
BF16 = jnp.bfloat16

D_MODEL = 1024
N_META = 16
H_A = 4
DK_A = 128
DV_A = 128
D_QK_A = H_A * DK_A
D_A = H_A * DV_A
D_QKV_A = 2 * D_QK_A + D_A
CONV_A = 4
H_B = 8
P_B = 64
D_B = H_B * P_B
G_B = 2
N_B = 128
D_XBC = D_B + 2 * G_B * N_B
CONV_B = 4
D_C = 512
CONV_C = 3
D_AB = D_A + D_B
D_MIX = D_AB + D_C
N_EGROUPS = 4
EXP_PER_GROUP = 4
N_EXPERTS = N_EGROUPS * EXP_PER_GROUP
D_FF_E = 256
EPS = 1e-6
CHUNK = 128

D_HCB = 3 * D_C
D_SMALL = 128
PROJ_WIDTHS = (D_QKV_A, D_A, D_XBC, D_B, D_HCB, D_SMALL)
D_PROJ_PACKED = sum(PROJ_WIDTHS)
PROJ_OFFSETS = tuple(sum(PROJ_WIDTHS[:i]) for i in range(len(PROJ_WIDTHS)))
ACT_WIDTHS = (D_QKV_A, D_A, D_XBC, D_B, D_C, D_SMALL)
HIST_WIDTHS = (D_QKV_A, D_XBC, D_C)
HIST_LENS = (CONV_A - 1, CONV_B - 1, CONV_C - 1)
LANE_BETA = 0
LANE_GA = H_A
LANE_GB = 2 * H_A
N_GATE_SLABS = 6
HIST_ROWS = 8
G_PROMPT = 2
G_SAMPLE = 16
EXPERT_BATCH = 4
TM_PROJ = 512
TM_POST = 512
TILES_PER_PROMPT = 4
CONV_COLS = 512

VMEM_LIMIT_BYTES = 56 * 1024 * 1024


def _rmsnorm(x, w):
    return x * lax.rsqrt(jnp.mean(x * x, axis=-1, keepdims=True) + EPS) * w


def _silu(x):
    h = 0.5 * x
    return h * jnp.tanh(h) + h


def _softplus(x):
    return jnp.maximum(x, 0.0) + jnp.log1p(jnp.exp(-jnp.abs(x)))


def _dot(a, b):
    return jnp.dot(a.astype(BF16), b.astype(BF16), preferred_element_type=F32)


def _dot_nt(a, b):
    return lax.dot_general(a.astype(BF16), b.astype(BF16), (((1,), (1,)), ((), ())),
                           preferred_element_type=F32)


def _dot_tn(a, b):
    return lax.dot_general(a.astype(BF16), b.astype(BF16), (((0,), (0,)), ((), ())),
                           preferred_element_type=F32)


def _cumsum_rows(g, tril_b):
    g1 = g.astype(BF16)
    r1 = g - g1.astype(F32)
    g2 = r1.astype(BF16)
    g3 = (r1 - g2.astype(F32)).astype(BF16)
    d = functools.partial(jnp.dot, preferred_element_type=F32)
    return d(tril_b, g1) + d(tril_b, g2) + d(tril_b, g3)


def _unit_lower_inverse_offdiag(nmats, row, col, C):
    ds = [-jnp.where((row >> 1) == (col >> 1), n, 0.0) for n in nmats]
    s, ls = 2, 1
    while s < C:
        same_2s = (row >> (ls + 1)) == (col >> (ls + 1))
        same_s = (row >> ls) == (col >> ls)
        es = [jnp.where(same_2s, jnp.where(same_s, 0.0, n), 0.0) for n in nmats]
        ys = [e + _dot(d, e) for d, e in zip(ds, es)]
        ds = [d - (y + _dot(y, d)) for d, y in zip(ds, ys)]
        s, ls = 2 * s, ls + 1
    return ds


def _causal_conv(hist_ref, cur, w_ref, width, C, col0=0):
    n_cols = cur.shape[1]
    x3 = jnp.concatenate([hist_ref[...], cur], axis=0).reshape((C + HIST_ROWS) // 8, 8, n_cols)
    sub = lax.broadcasted_iota(jnp.int32, x3.shape, 1)

    def shift(a3, k):
        r = pltpu.roll(a3, k, axis=1)
        from_prev_tile = jnp.concatenate([r[:1], r[:-1]], axis=0)
        return jnp.where(sub < k, from_prev_tile, r)

    w = lambda i: w_ref[i:i + 1, col0:col0 + n_cols]
    if width == 4:
        x1 = shift(x3, 1)
        y3 = (x3 * w(3) + x1 * w(2)) + shift(x3 * w(1) + x1 * w(0), 2)
    else:
        assert width == 3
        y3 = x3 * w(2) + shift(x3, 1) * w(1) + shift(x3, 2) * w(0)
    hist_ref[...] = cur[C - HIST_ROWS:C]
    return y3.reshape(C + HIST_ROWS, n_cols)[HIST_ROWS:]


def _conv_front_end(raw_qkv, raw_xbc, hcb, exts, conv_w, keep, C):
    exta, extb, extc = exts
    cwa_ref, cwb_ref, cbb_ref, cwc_ref = conv_w
    qkv_act = _silu(_causal_conv(exta, keep(raw_qkv), cwa_ref, CONV_A, C))
    xbc_act = _silu(_causal_conv(extb, keep(raw_xbc), cwb_ref, CONV_B, C) + cbb_ref[...])
    u_c = keep(hcb[:, D_C:2 * D_C] * hcb[:, 0:D_C])
    y_c = hcb[:, 2 * D_C:3 * D_C] * _causal_conv(extc, u_c, cwc_ref, CONV_C, C)
    return qkv_act, xbc_act, y_c


def _proj_kernel(x_ref, nw_ref, w_ref, *out_refs):
    hn = _rmsnorm(x_ref[...], nw_ref[...]).astype(BF16)
    for o_ref, off, width in zip(out_refs, PROJ_OFFSETS, PROJ_WIDTHS):
        o_ref[...] = jnp.dot(hn, w_ref[:, off:off + width], preferred_element_type=F32)


def _proj_call(x, norm_w, w_packed, layer, tm):
    T = x.shape[0]
    const = lambda i: (0, 0)
    return pl.pallas_call(
        _proj_kernel,
        grid=(T // tm,),
        in_specs=[
            pl.BlockSpec((tm, D_MODEL), lambda i: (i, 0)),
            pl.BlockSpec((1, D_MODEL), const),
            pl.BlockSpec((None, D_MODEL, D_PROJ_PACKED), lambda i: (layer, 0, 0), pipeline_mode=pl.Buffered(1)),
        ],
        out_specs=[pl.BlockSpec((tm, w), lambda i: (i, 0)) for w in PROJ_WIDTHS],
        out_shape=[jax.ShapeDtypeStruct((T, w), F32) for w in PROJ_WIDTHS],
        compiler_params=pltpu.CompilerParams(
            dimension_semantics=("parallel",), vmem_limit_bytes=VMEM_LIMIT_BYTES),
        name="proj",
    )(x, norm_w.reshape(1, D_MODEL), w_packed)


def _proj_conv_kernel(x_ref, nw_ref, w_ref, cwa_ref, cwb_ref, cbb_ref, cwc_ref, *rest, tm, pad):
    n_ext = (D_QKV_A + D_XBC + D_C) // CONV_COLS
    (qkv_o, za_o, xbc_o, zb_o, yc_o, small_o, ha_o, hb_o, hc_o) = rest[-9 - n_ext:-n_ext]
    exts = rest[-n_ext:]
    j = lax.rem(pl.program_id(0), TILES_PER_PROMPT)

    @pl.when(j == 0)
    def _():
        for ext in exts:
            ext[...] = jnp.zeros((HIST_ROWS, CONV_COLS), F32)

    valid = j * tm + lax.broadcasted_iota(jnp.int32, (tm, 1), 0) >= pad
    hn = _rmsnorm(jnp.where(valid, x_ref[...], 0.0), nw_ref[...]).astype(BF16)
    mm = lambda off, width: jnp.dot(hn, w_ref[:, off:off + width], preferred_element_type=F32)
    off_qkv, off_za, off_xbc, off_zb, off_hcb, off_small = PROJ_OFFSETS

    def carry(ext, hist_ref, n, col0):
        hist_ref[:, col0:col0 + CONV_COLS] = ext[HIST_ROWS - n:HIST_ROWS, :]

    jobs = []
    for k in range(D_QKV_A // CONV_COLS):
        def tail(raw, k=k, ext=exts[k]):
            col0 = k * CONV_COLS
            qkv_o[:, col0:col0 + CONV_COLS] = _silu(_causal_conv(ext, raw, cwa_ref, CONV_A, tm, col0))
            carry(ext, ha_o, CONV_A - 1, col0)
        jobs.append((functools.partial(mm, off_qkv + k * CONV_COLS, CONV_COLS), tail))
    for k in range(D_XBC // CONV_COLS):
        def tail(raw, k=k, ext=exts[D_QKV_A // CONV_COLS + k]):
            col0 = k * CONV_COLS
            conv = _causal_conv(ext, raw, cwb_ref, CONV_B, tm, col0)
            xbc_o[:, col0:col0 + CONV_COLS] = _silu(conv + cbb_ref[:, col0:col0 + CONV_COLS])
            carry(ext, hb_o, CONV_B - 1, col0)
        jobs.append((functools.partial(mm, off_xbc + k * CONV_COLS, CONV_COLS), tail))

    def tail_c(hcb, ext=exts[-1]):
        u_c = hcb[:, D_C:2 * D_C] * hcb[:, 0:D_C]
        yc_o[...] = hcb[:, 2 * D_C:3 * D_C] * _causal_conv(ext, u_c, cwc_ref, CONV_C, tm)
        carry(ext, hc_o, CONV_C - 1, 0)
    jobs.append((functools.partial(mm, off_hcb, D_HCB), tail_c))

    def tail_plain(raws):
        za_o[...], zb_o[...], small_o[...] = raws
    jobs.append((lambda: (mm(off_za, D_A), mm(off_zb, D_B), mm(off_small, D_SMALL)), tail_plain))

    pending = None
    for matmuls, tail in jobs:
        raw = matmuls()
        if pending is not None:
            pending[1](pending[0])
        pending = (raw, tail)
    pending[1](pending[0])


def _proj_conv_call(x, norm_w, w_packed, layer, conv_w, stacked, out_layer, depth, *, B, pad):
    T = x.shape[0]
    tm = T // (B * TILES_PER_PROMPT)
    const = lambda i: (0, 0)
    row = lambda i: (i, 0)
    hist = lambda i: (out_layer, i // TILES_PER_PROMPT, 0, 0)
    cwa, cwb, cbb, cwc = conv_w
    in_specs = [
        pl.BlockSpec((tm, D_MODEL), row),
        pl.BlockSpec((1, D_MODEL), const),
        pl.BlockSpec((None, D_MODEL, D_PROJ_PACKED), lambda i: (layer, 0, 0), pipeline_mode=pl.Buffered(1)),
        pl.BlockSpec((CONV_A, D_QKV_A), const),
        pl.BlockSpec((CONV_B, D_XBC), const),
        pl.BlockSpec((1, D_XBC), const),
        pl.BlockSpec((CONV_C, D_C), const),
    ]
    stacked = list(stacked or [])
    n_in = len(in_specs)
    in_specs += [pl.BlockSpec(memory_space=pl.ANY) for _ in stacked]
    n_act = len(ACT_WIDTHS)
    outs = pl.pallas_call(
        functools.partial(_proj_conv_kernel, tm=tm, pad=pad),
        grid=(T // tm,),
        in_specs=in_specs,
        out_specs=([pl.BlockSpec((tm, w), row) for w in ACT_WIDTHS]
                   + [pl.BlockSpec((None, None, n, w), hist) for n, w in zip(HIST_LENS, HIST_WIDTHS)]),
        out_shape=([jax.ShapeDtypeStruct((T, w), F32) for w in ACT_WIDTHS]
                   + [jax.ShapeDtypeStruct((depth, B, n, w), F32) for n, w in zip(HIST_LENS, HIST_WIDTHS)]),
        scratch_shapes=[pltpu.VMEM((HIST_ROWS, CONV_COLS), F32)
                        for _ in range(sum(HIST_WIDTHS) // CONV_COLS)],
        input_output_aliases={n_in + i: n_act + i for i in range(len(stacked))},
        compiler_params=pltpu.CompilerParams(
            dimension_semantics=("arbitrary",), vmem_limit_bytes=VMEM_LIMIT_BYTES),
        name="proj_conv",
    )(x, norm_w.reshape(1, D_MODEL), w_packed, cwa, cwb, cbb, cwc, *stacked)
    return outs[:n_act], outs[n_act:]


def _mix_kernel(*refs, G, C, pad, nC, preact):
    if preact:
        (qkv_ref, za_ref, xbc_ref, zb_ref, small_ref, small_next_ref,
         s0a_ref, s0b_ref, hp_ref, ona_ref, onb_ref) = refs[:11]
        mixed_ref, sa_out, sb_out, sa, sb, gate_s = refs[-6:]
    else:
        (qkv_ref, za_ref, xbc_ref, zb_ref, hcb_ref, small_ref, hista_ref, histb_ref, histc_ref,
         s0a_ref, s0b_ref, cwa_ref, cwb_ref, cbb_ref, cwc_ref, hp_ref, ona_ref, onb_ref) = refs[:18]
        (mixed_ref, yc_ref, sa_out, sb_out, ha_out, hb_out, hc_out, exta, extb, extc, sa, sb) = refs[-12:]
        exts, hist_refs, hist_outs = (exta, extb, extc), (hista_ref, histb_ref, histc_ref), (ha_out, hb_out, hc_out)
    c = pl.program_id(1)

    @pl.when(c == 0)
    def _():
        sa[...] = s0a_ref[...]
        sb[...] = s0b_ref[...]
        if not preact:
            for ext, h_ref in zip(exts, hist_refs):
                ext[...] = h_ref[...]

    row = lax.broadcasted_iota(jnp.int32, (C, C), 0)
    col = lax.broadcasted_iota(jnp.int32, (C, C), 1)
    tril = row >= col
    strict = row > col
    tril_b = jnp.where(tril, 1.0, 0.0).astype(BF16)

    if pad:
        tok = c * C + lax.broadcasted_iota(jnp.int32, (C, 1), 0)
        valid = tok >= pad
        keep = lambda v: jnp.where(valid, v, 0.0)
    else:
        keep = lambda v: v

    seqs = range(G)
    lane_col = lambda a, lane: a[:, lane:lane + 1]

    def gates(small, keep_rows):
        sp_ = keep_rows(_softplus(small + hp_ref[0:1, :]))
        G_ = _cumsum_rows(-jnp.exp(hp_ref[1:2, :]) * sp_, tril_b)
        return sp_, jax.nn.sigmoid(small), G_, G_.T, jnp.exp(G_), jnp.exp(G_[C - 1:C, :] - G_)

    if preact:
        @pl.when(c == 0)
        def _():
            for g in seqs:
                for i, a in enumerate(gates(small_ref[g], keep)):
                    gate_s[g, i] = a
        gate_vals = [[gate_s[g, i] for i in range(N_GATE_SLABS)] for g in seqs]
    else:
        gate_vals = [gates(small_ref[g], keep) for g in seqs]
    sp, beta_all, G_all, GT_all, eG_all, eG_rem = (list(v) for v in zip(*gate_vals))
    eG_last = [x[C - 1:C, :] for x in eG_all]

    def decay_matrix(g, lane):
        dlog = lane_col(G_all[g], lane) - GT_all[g][lane:lane + 1, :]
        return jnp.where(tril, jnp.exp(jnp.minimum(dlog, 0.0)), 0.0)

    if preact:
        qkv = [qkv_ref[g] for g in seqs]
        xbc = [xbc_ref[g] for g in seqs]
    else:
        qkv, xbc = [], []
        for g in seqs:
            qkv_act, xbc_act, y_c = _conv_front_end(
                qkv_ref[g], xbc_ref[g], hcb_ref[g], [ext.at[g] for ext in exts],
                (cwa_ref, cwb_ref, cbb_ref, cwc_ref), keep, C)
            qkv.append(qkv_act)
            xbc.append(xbc_act)
            yc_ref[g] = y_c

    chains = [(g, h) for g in seqs for h in range(H_A)]
    qs, ks, vs, kbs, egs, kdts = [], [], [], [], [], []
    for g, h in chains:
        q = qkv[g][:, h * DK_A:(h + 1) * DK_A]
        k = qkv[g][:, D_QK_A + h * DK_A:D_QK_A + (h + 1) * DK_A]
        v = qkv[g][:, 2 * D_QK_A + h * DV_A:2 * D_QK_A + (h + 1) * DV_A]
        q = q * lax.rsqrt(jnp.sum(q * q, axis=-1, keepdims=True) + EPS) * (DK_A ** -0.5)
        k = k * lax.rsqrt(jnp.sum(k * k, axis=-1, keepdims=True) + EPS)
        beta = lane_col(beta_all[g], LANE_BETA + h)
        qs.append(q)
        ks.append(k)
        vs.append(v * beta)
        kbs.append(k * beta)
        egs.append(lane_col(eG_all[g], LANE_GA + h))
        kdts.append((k * lane_col(eG_rem[g], LANE_GA + h)).T)
    kqs = [_dot_nt(jnp.concatenate([kb, q], axis=0), k) for kb, q, k in zip(kbs, qs, ks)]
    decs = [decay_matrix(g, LANE_GA + h) for g, h in chains]
    nmats = [jnp.where(strict, kq[:C] * dec, 0.0) for kq, dec in zip(kqs, decs)]
    qks = [kq[C:] * dec for kq, dec in zip(kqs, decs)]
    heads_per_group = H_B // G_B
    gw = D_B // G_B
    grp_of = lambda h: h // heads_per_group
    bgs = {(g, grp): xbc[g][:, D_B + grp * N_B:D_B + (grp + 1) * N_B] for g in seqs for grp in range(G_B)}
    cgs = {(g, grp): xbc[g][:, D_B + (G_B + grp) * N_B:D_B + (G_B + grp + 1) * N_B]
           for g in seqs for grp in range(G_B)}
    heads = [(g, h) for g in seqs for h in range(H_B)]
    x_hs = [xbc[g][:, h * P_B:(h + 1) * P_B] for g, h in heads]
    xdts = [x_h * lane_col(sp[g], LANE_GB + h) for (g, h), x_h in zip(heads, x_hs)]
    xdtts = [xdt.T for xdt in xdts]
    cdecs = [cgs[(g, grp_of(h))] * lane_col(eG_all[g], LANE_GB + h) for g, h in heads]
    bdecs = [bgs[(g, grp_of(h))] * lane_col(eG_rem[g], LANE_GB + h) for g, h in heads]
    decs_b = [decay_matrix(g, LANE_GB + h) for g, h in heads]
    tinvs = _unit_lower_inverse_offdiag(nmats, row, col, C)
    rhss = [jnp.concatenate([vb, kb * eg], axis=1) for vb, kb, eg in zip(vs, kbs, egs)]
    sols = [rhs + _dot(t, rhs) for t, rhs in zip(tinvs, rhss)]
    if preact:
        for g in seqs:
            for i, a in enumerate(gates(small_next_ref[g], lambda v: v)):
                gate_s[g, i] = a
    s_olds = [sa[g, h] for g, h in chains]
    wss = [_dot(jnp.concatenate([sol[:, DV_A:], q * eg], axis=0), s_old)
           for sol, q, eg, s_old in zip(sols, qs, egs, s_olds)]
    ws_ = [sol[:, :DV_A] - ws[:C] for sol, ws in zip(sols, wss)]
    os_ = [ws[C:] + _dot(qk, w) for ws, qk, w in zip(wss, qks, ws_)]
    for (g, h), kdt, w, s_old, o in zip(chains, kdts, ws_, s_olds, os_):
        sa[g, h] = lane_col(eG_last[g], LANE_GA + h) * s_old + _dot(kdt, w)
        gate = _silu(za_ref[g, :, h * DV_A:(h + 1) * DV_A])
        mixed_ref[g, :, h * DV_A:(h + 1) * DV_A] = _rmsnorm(o, ona_ref[...]) * gate

    cbts = {key: _dot_nt(cgs[key], bgs[key]) for key in bgs}
    h_olds = [sb[g, h] for g, h in heads]
    y_offs = [_dot_nt(cdec, h_old) for cdec, h_old in zip(cdecs, h_olds)]
    for (g, h), xdtt, bdec, h_old in zip(heads, xdtts, bdecs, h_olds):
        sb[g, h] = lane_col(eG_last[g], LANE_GB + h) * h_old + _dot(xdtt, bdec)
    ys = [y_off + _dot(cbts[(g, grp_of(h))] * dec, xdt) + hp_ref[2:3, LANE_GB + h:LANE_GB + h + 1] * x_h
          for (g, h), y_off, dec, xdt, x_h in zip(heads, y_offs, decs_b, xdts, x_hs)]
    for g in seqs:
        for grp in range(G_B):
            lo = g * H_B + grp * heads_per_group
            yg = jnp.concatenate(ys[lo:lo + heads_per_group], axis=1)
            yg = yg * _silu(zb_ref[g, :, grp * gw:(grp + 1) * gw])
            mixed_ref[g, :, D_A + grp * gw:D_A + (grp + 1) * gw] = _rmsnorm(
                yg, onb_ref[:, grp * gw:(grp + 1) * gw])

    @pl.when(c == nC - 1)
    def _():
        sa_out[...] = sa[...]
        sb_out[...] = sb[...]
        if not preact:
            for o_ref, ext, n in zip(hist_outs, exts, HIST_LENS):
                o_ref[...] = ext[:, HIST_ROWS - n:HIST_ROWS, :]


def _mix_call(acts, hists, s0a, s0b, layer, weights, stacked, out_layer, depth, *, B, nC, C, G, pad, preact):
    L = nC * C
    conv_w, hp, ona, onb = weights
    tok = lambda b, c: (b, c, 0)
    per_b3 = lambda b, c: (b, 0, 0)
    out_b4 = lambda b, c: (out_layer, b, 0, 0)
    out_b5 = lambda b, c: (out_layer, b, 0, 0, 0)
    const = lambda b, c: (0, 0)
    state_specs = [
        pl.BlockSpec((None, G, H_A, DK_A, DV_A), lambda b, c: (layer, b, 0, 0, 0)),
        pl.BlockSpec((None, G, H_B, P_B, N_B), lambda b, c: (layer, b, 0, 0, 0)),
    ]
    head_specs = [pl.BlockSpec((8, D_SMALL), const), pl.BlockSpec((1, DV_A), const), pl.BlockSpec((1, D_B), const)]
    acts3 = [a.reshape(B, L, a.shape[-1]) for a in acts]
    in_specs = [pl.BlockSpec((G, C, a.shape[-1]), tok) for a in acts3]
    operands = list(acts3)
    if preact:
        assert C == D_SMALL, "the carried gate slabs assume (C, 128) == (128, C)"
        in_specs.append(pl.BlockSpec((G, C, D_SMALL), lambda b, c: (b, jnp.minimum(c + 1, nC - 1), 0)))
        operands.append(acts3[-1])
    else:
        in_specs += [pl.BlockSpec((G, HIST_ROWS, w), per_b3) for w in HIST_WIDTHS]
        operands += list(hists)
    in_specs += state_specs
    operands += [s0a, s0b]
    if not preact:
        in_specs += [pl.BlockSpec((CONV_A, D_QKV_A), const), pl.BlockSpec((CONV_B, D_XBC), const),
                     pl.BlockSpec((1, D_XBC), const), pl.BlockSpec((CONV_C, D_C), const)]
        operands += list(conv_w)
    in_specs += head_specs
    operands += [hp, ona, onb]

    out_specs = [pl.BlockSpec((G, C, D_AB), tok)]
    out_shape = [jax.ShapeDtypeStruct((B, L, D_AB), F32)]
    if not preact:
        out_specs.append(pl.BlockSpec((G, C, D_C), tok))
        out_shape.append(jax.ShapeDtypeStruct((B, L, D_C), F32))
    n_plain = len(out_specs)
    out_specs += [pl.BlockSpec((None, G, H_A, DK_A, DV_A), out_b5), pl.BlockSpec((None, G, H_B, P_B, N_B), out_b5)]
    out_shape += [jax.ShapeDtypeStruct((depth, B, H_A, DK_A, DV_A), F32),
                  jax.ShapeDtypeStruct((depth, B, H_B, P_B, N_B), F32)]
    scratch = []
    if not preact:
        out_specs += [pl.BlockSpec((None, G, n, w), out_b4) for n, w in zip(HIST_LENS, HIST_WIDTHS)]
        out_shape += [jax.ShapeDtypeStruct((depth, B, n, w), F32) for n, w in zip(HIST_LENS, HIST_WIDTHS)]
        scratch += [pltpu.VMEM((G, HIST_ROWS, w), F32) for w in HIST_WIDTHS]
    scratch += [pltpu.VMEM((G, H_A, DK_A, DV_A), F32), pltpu.VMEM((G, H_B, P_B, N_B), F32)]
    if preact:
        scratch.append(pltpu.VMEM((G, N_GATE_SLABS, C, D_SMALL), F32))

    stacked = list(stacked or [])
    n_in = len(in_specs)
    in_specs += [pl.BlockSpec(memory_space=pl.ANY) for _ in stacked]
    outs = pl.pallas_call(
        functools.partial(_mix_kernel, G=G, C=C, pad=pad, nC=nC, preact=preact),
        grid=(B // G, nC),
        in_specs=in_specs,
        out_specs=out_specs,
        out_shape=out_shape,
        scratch_shapes=scratch,
        input_output_aliases={n_in + i: n_plain + i for i in range(len(stacked))},
        compiler_params=pltpu.CompilerParams(
            dimension_semantics=("parallel", "arbitrary"), vmem_limit_bytes=VMEM_LIMIT_BYTES),
        name="mix",
    )(*operands, *stacked)
    plain = [o.reshape(B * L, o.shape[-1]) for o in outs[:n_plain]]
    return plain, list(outs[n_plain:])


def _route(logits):
    lg = [logits[:, g:g + 1] for g in range(N_EGROUPS)]
    m = functools.reduce(jnp.maximum, lg)
    ex = [jnp.exp(l - m) for l in lg]
    den = functools.reduce(lambda a, b: a + b, ex)
    p = [e / den for e in ex]
    p_top, g_idx = p[0], jnp.zeros_like(p[0], dtype=jnp.int32)
    for g in range(1, N_EGROUPS):
        better = p[g] > p_top
        g_idx = jnp.where(better, g, g_idx)
        p_top = jnp.where(better, p[g], p_top)
    g_hot = [g_idx == g for g in range(N_EGROUPS)]
    le_sel = []
    for j in range(EXP_PER_GROUP):
        acc = None
        for g in range(N_EGROUPS):
            lane = N_EGROUPS + g * EXP_PER_GROUP + j
            t = jnp.where(g_hot[g], logits[:, lane:lane + 1], 0.0)
            acc = t if acc is None else acc + t
        le_sel.append(acc)
    m2 = functools.reduce(jnp.maximum, le_sel)
    ex2 = [jnp.exp(l - m2) for l in le_sel]
    den2 = functools.reduce(lambda a, b: a + b, ex2)
    s = [e / den2 for e in ex2]
    v1, i1 = s[0], jnp.zeros_like(g_idx)
    for j in range(1, EXP_PER_GROUP):
        better = s[j] > v1
        i1 = jnp.where(better, j, i1)
        v1 = jnp.where(better, s[j], v1)
    v2, i2 = jnp.full_like(v1, -1.0), jnp.zeros_like(g_idx)
    for j in range(EXP_PER_GROUP):
        s_rest = jnp.where(i1 == j, -1.0, s[j])
        better = s_rest > v2
        i2 = jnp.where(better, j, i2)
        v2 = jnp.where(better, s_rest, v2)
    tot = v1 + v2
    w1 = v1 / tot * p_top
    w2 = v2 / tot * p_top
    comb = []
    for g in range(N_EGROUPS):
        for j in range(EXP_PER_GROUP):
            inner = jnp.where(i1 == j, w1, jnp.where(i2 == j, w2, 0.0))
            comb.append(jnp.where(g_hot[g], inner, 0.0))
    return comb


def _post_kernel(x_ref, mab_ref, yc_ref, wout_ref, n2_ref, wr_ref, wg_ref, wu_ref, wd_ref, fn_ref,
                 o_ref, *, final):
    mixed = jnp.concatenate([mab_ref[...].astype(BF16), yc_ref[...].astype(BF16)], axis=1)
    x1 = x_ref[...] + jnp.dot(mixed, wout_ref[...], preferred_element_type=F32)
    hb = _rmsnorm(x1, n2_ref[...]).astype(BF16)
    comb = _route(jnp.dot(hb, wr_ref[...], preferred_element_type=F32))

    def activations(grp):
        acts = []
        for e in range(grp * EXPERT_BATCH, (grp + 1) * EXPERT_BATCH):
            hg = jnp.dot(hb, wg_ref[e], preferred_element_type=F32)
            hu = jnp.dot(hb, wu_ref[e], preferred_element_type=F32)
            acts.append((_silu(hg) * hu * comb[e]).astype(BF16))
        return jnp.concatenate(acts, axis=1)

    n_batches = N_EXPERTS // EXPERT_BATCH
    kb = EXPERT_BATCH * D_FF_E
    acts = [activations(0)]
    acc = x1
    for grp in range(n_batches):
        if grp + 1 < n_batches:
            acts.append(activations(grp + 1))
        acc = acc + jnp.dot(acts[grp], wd_ref[grp * kb:(grp + 1) * kb, :], preferred_element_type=F32)
    if final:
        acc = _rmsnorm(acc, fn_ref[...])
    o_ref[...] = acc


def _post_call(x, mixed_ab, y_c, layer, w_out, n2, w_router, w_gate, w_up, w_down, fn, *, tm, final, window=None):
    single = dict(pipeline_mode=pl.Buffered(1))
    const2 = lambda s: (0, 0)
    layer3 = lambda s: (layer, 0, 0)
    layer4 = lambda s: (layer, 0, 0, 0)
    if window is None:
        n_tiles = x.shape[0] // tm
        tok_specs = [pl.BlockSpec((tm, w), lambda s: (s, 0)) for w in (D_MODEL, D_AB, D_C)]
    else:
        B, L, start, length = window
        per_seq = length // tm
        n_tiles = B * per_seq
        window_row = lambda s: (pl.multiple_of((s // per_seq) * L + start + (s % per_seq) * tm, 8), 0)
        tok_specs = [pl.BlockSpec((pl.Element(tm), pl.Element(w)), window_row) for w in (D_MODEL, D_AB, D_C)]
    return pl.pallas_call(
        functools.partial(_post_kernel, final=final),
        grid=(n_tiles,),
        in_specs=tok_specs + [
            pl.BlockSpec((None, D_MIX, D_MODEL), layer3, **single),
            pl.BlockSpec((1, D_MODEL), const2),
            pl.BlockSpec((None, D_MODEL, D_SMALL), layer3, **single),
            pl.BlockSpec((None, N_EXPERTS, D_MODEL, D_FF_E), layer4, **single),
            pl.BlockSpec((None, N_EXPERTS, D_MODEL, D_FF_E), layer4, **single),
            pl.BlockSpec((None, N_EXPERTS * D_FF_E, D_MODEL), layer3, **single),
            pl.BlockSpec((1, D_MODEL), const2),
        ],
        out_specs=pl.BlockSpec((tm, D_MODEL), lambda s: (s, 0)),
        out_shape=jax.ShapeDtypeStruct((n_tiles * tm, D_MODEL), F32),
        compiler_params=pltpu.CompilerParams(
            dimension_semantics=("parallel",), vmem_limit_bytes=VMEM_LIMIT_BYTES),
        name="post",
    )(x, mixed_ab, y_c, w_out, n2, w_router, w_gate, w_up, w_down, fn)


def _pack_w_in(w):
    o = [0]
    for s in (D_QKV_A, D_A, H_A, H_A, D_XBC, D_B, H_B, D_C, D_C, D_C):
        o.append(o[-1] + s)
    cols = [w[..., o[0]:o[2]], w[..., o[4]:o[6]], w[..., o[7]:o[10]], w[..., o[2]:o[4]], w[..., o[6]:o[7]],
            jnp.zeros(w.shape[:2] + (D_SMALL - 2 * H_A - H_B,), w.dtype)]
    return jnp.concatenate(cols, axis=-1).astype(BF16)


def _head_params(dt_bias_a, dt_bias_b, a_log_a, a_log_b, d_skip_b):
    z4 = jnp.zeros((H_A,), F32)
    tail = jnp.zeros((D_SMALL - 2 * H_A - H_B,), F32)
    rows = [
        jnp.concatenate([z4, dt_bias_a, dt_bias_b, tail]),
        jnp.concatenate([z4, a_log_a, a_log_b, tail]),
        jnp.concatenate([z4, z4, d_skip_b, tail]),
    ]
    return jnp.concatenate([jnp.stack(rows), jnp.zeros((8 - len(rows), D_SMALL), F32)], axis=0)


def _pad_hist(h):
    return jnp.pad(h, ((0, 0), (0, 0), (HIST_ROWS - h.shape[2], 0), (0, 0)))


def kernel(x_prompt, x_sample, state_gdn, state_gdn_conv, state_ssd, state_ssd_conv, state_sconv,
           meta_tokens, norm1_w, w_in, conv_a_w, a_log_a, dt_bias_a, onorm_a_w, conv_b_w, conv_b_b,
           a_log_b, dt_bias_b, d_skip_b, onorm_b_w, conv_c_w, w_out, norm2_w, w_router_g,
           w_router_e, w_gate, w_up, w_down, final_norm_w):
    bp, seq, _ = x_prompt.shape
    bs, dseq, _ = x_sample.shape
    depth = w_in.shape[0]
    pad = CHUNK - N_META
    lp = pad + N_META + seq
    ncp = lp // CHUNK

    meta = jnp.broadcast_to(meta_tokens[None], (bp, N_META, D_MODEL))
    x_p = jnp.concatenate([jnp.zeros((bp, pad, D_MODEL), F32), meta, x_prompt], axis=1)
    x_p = x_p.reshape(bp * lp, D_MODEL)
    x_s = x_sample.reshape(bs * dseq, D_MODEL)

    zero_sa = jnp.zeros((1, bp, H_A, DK_A, DV_A), F32)
    zero_sb = jnp.zeros((1, bp, H_B, P_B, N_B), F32)
    hists_s = [_pad_hist(state_gdn_conv), _pad_hist(state_ssd_conv), _pad_hist(state_sconv)]

    fn = final_norm_w.reshape(1, D_MODEL)
    w_packed = _pack_w_in(w_in)
    w_router = jnp.concatenate(
        [w_router_g, w_router_e, jnp.zeros((depth, D_MODEL, D_SMALL - N_EGROUPS - N_EXPERTS), F32)],
        axis=-1).astype(BF16)
    w_out_b, w_gate_b, w_up_b = w_out.astype(BF16), w_gate.astype(BF16), w_up.astype(BF16)
    w_down_b = w_down.reshape(depth, N_EXPERTS * D_FF_E, D_MODEL).astype(BF16)
    hist_p = st_p = st_s = None
    for l in range(depth):
        conv_w = (conv_a_w[l], conv_b_w[l], conv_b_b[l].reshape(1, D_XBC), conv_c_w[l])
        mix_w = (conv_w, _head_params(dt_bias_a[l], dt_bias_b[l], a_log_a[l], a_log_b[l], d_skip_b[l]),
                 onorm_a_w[l].reshape(1, DV_A), onorm_b_w[l].reshape(1, D_B))
        post_w = (l, w_out_b, norm2_w[l].reshape(1, D_MODEL), w_router, w_gate_b, w_up_b, w_down_b, fn)
        final = l == depth - 1

        acts_p, hist_p = _proj_conv_call(x_p, norm1_w[l], w_packed, l, conv_w, hist_p, l, depth, B=bp, pad=pad)
        qkv_act, za, xbc_act, zb, yc_p, small = acts_p
        (mixed_p,), st_p = _mix_call((qkv_act, za, xbc_act, zb, small), None, zero_sa, zero_sb, 0, mix_w,
                                     st_p, l, depth, B=bp, nC=ncp, C=CHUNK, G=G_PROMPT, pad=pad, preact=True)
        window = (bp, lp, pad + N_META, seq) if final else None
        x_p = _post_call(x_p, mixed_p, yc_p, *post_w, tm=TM_POST, final=final, window=window)

        proj_s = _proj_call(x_s, norm1_w[l], w_packed, l, tm=TM_PROJ)
        (mixed_s, yc_s), st_s = _mix_call(proj_s, [h[l] for h in hists_s], state_gdn, state_ssd, l, mix_w,
                                          st_s, l, depth, B=bs, nC=1, C=dseq, G=G_SAMPLE, pad=0, preact=False)
        x_s = _post_call(x_s, mixed_s, yc_s, *post_w, tm=TM_POST, final=final)

    y_sample = x_s.reshape(bs, dseq, D_MODEL)
    return (x_p.reshape(bp, seq, D_MODEL), y_sample, st_p[0], hist_p[0], st_p[1], hist_p[1], hist_p[2],
            st_s[0], st_s[2], st_s[1], st_s[3], st_s[4])
```

```python
import functools

import jax
import jax.numpy as jnp
from jax import lax
from jax.experimental import pallas as pl
from jax.experimental.pallas import tpu as pltpu

F32 = jnp.float32
BF16 = jnp.bfloat16

D_MODEL = 1024
N_META = 16
H_A = 4
DK_A = 128
DV_A = 128
D_QK_A = H_A * DK_A
D_A = H_A * DV_A
D_QKV_A = 2 * D_QK_A + D_A
CONV_A = 4
H_B = 8
P_B = 64
D_B = H_B * P_B
G_B = 2
N_B = 128
D_XBC = D_B + 2 * G_B * N_B
CONV_B = 4
D_C = 512
CONV_C = 3
D_AB = D_A + D_B
D_MIX = D_AB + D_C
N_EGROUPS = 4
EXP_PER_GROUP = 4
N_EXPERTS = N_EGROUPS * EXP_PER_GROUP
D_FF_E = 256
EPS = 1e-6
CHUNK = 128

D_HCB = 3 * D_C
D_SMALL = 128
PROJ_WIDTHS = (D_QKV_A, D_A, D_XBC, D_B, D_HCB, D_SMALL)
D_PROJ_PACKED = sum(PROJ_WIDTHS)
PROJ_OFFSETS = tuple(sum(PROJ_WIDTHS[:i]) for i in range(len(PROJ_WIDTHS)))
ACT_WIDTHS = (D_QKV_A, D_A, D_XBC, D_B, D_C, D_SMALL)
HIST_WIDTHS = (D_QKV_A, D_XBC, D_C)
HIST_LENS = (CONV_A - 1, CONV_B - 1, CONV_C - 1)
LANE_BETA = 0
LANE_GA = H_A
LANE_GB = 2 * H_A
N_GATE_SLABS = 6
HIST_ROWS = 8
G_PROMPT = 2
G_SAMPLE = 16
EXPERT_BATCH = 4
TM_PROJ = 512
TM_POST = 512
TILES_PER_PROMPT = 4
CONV_COLS = 512

VMEM_LIMIT_BYTES = 56 * 1024 * 1024


def _rmsnorm(x, w):
    return x * lax.rsqrt(jnp.mean(x * x, axis=-1, keepdims=True) + EPS) * w


def _silu(x):
    h = 0.5 * x
    return h * jnp.tanh(h) + h


def _softplus(x):
    return jnp.maximum(x, 0.0) + jnp.log1p(jnp.exp(-jnp.abs(x)))


def _dot(a, b):
    return jnp.dot(a.astype(BF16), b.astype(BF16), preferred_element_type=F32)


def _dot_nt(a, b):
    return lax.dot_general(a.astype(BF16), b.astype(BF16), (((1,), (1,)), ((), ())),
                           preferred_element_type=F32)


def _dot_tn(a, b):
    return lax.dot_general(a.astype(BF16), b.astype(BF16), (((0,), (0,)), ((), ())),
                           preferred_element_type=F32)


def _cumsum_rows(g, tril_b):
    g1 = g.astype(BF16)
    r1 = g - g1.astype(F32)
    g2 = r1.astype(BF16)
    g3 = (r1 - g2.astype(F32)).astype(BF16)
    d = functools.partial(jnp.dot, preferred_element_type=F32)
    return d(tril_b, g1) + d(tril_b, g2) + d(tril_b, g3)


def _unit_lower_inverse_offdiag(nmats, row, col, C):
    ds = [-jnp.where((row >> 1) == (col >> 1), n, 0.0) for n in nmats]
    s, ls = 2, 1
    while s < C:
        same_2s = (row >> (ls + 1)) == (col >> (ls + 1))
        same_s = (row >> ls) == (col >> ls)
        es = [jnp.where(same_2s, jnp.where(same_s, 0.0, n), 0.0) for n in nmats]
        ys = [e + _dot(d, e) for d, e in zip(ds, es)]
        ds = [d - (y + _dot(y, d)) for d, y in zip(ds, ys)]
        s, ls = 2 * s, ls + 1
    return ds


def _causal_conv(hist_ref, cur, w_ref, width, C, col0=0):
    n_cols = cur.shape[1]
    x3 = jnp.concatenate([hist_ref[...], cur], axis=0).reshape((C + HIST_ROWS) // 8, 8, n_cols)
    sub = lax.broadcasted_iota(jnp.int32, x3.shape, 1)

    def shift(a3, k):
        r = pltpu.roll(a3, k, axis=1)
        from_prev_tile = jnp.concatenate([r[:1], r[:-1]], axis=0)
        return jnp.where(sub < k, from_prev_tile, r)

    w = lambda i: w_ref[i:i + 1, col0:col0 + n_cols]
    if width == 4:
        x1 = shift(x3, 1)
        y3 = (x3 * w(3) + x1 * w(2)) + shift(x3 * w(1) + x1 * w(0), 2)
    else:
        assert width == 3
        y3 = x3 * w(2) + shift(x3, 1) * w(1) + shift(x3, 2) * w(0)
    hist_ref[...] = cur[C - HIST_ROWS:C]
    return y3.reshape(C + HIST_ROWS, n_cols)[HIST_ROWS:]


def _conv_front_end(raw_qkv, raw_xbc, hcb, exts, conv_w, keep, C):
    exta, extb, extc = exts
    cwa_ref, cwb_ref, cbb_ref, cwc_ref = conv_w
    qkv_act = _silu(_causal_conv(exta, keep(raw_qkv), cwa_ref, CONV_A, C))
    xbc_act = _silu(_causal_conv(extb, keep(raw_xbc), cwb_ref, CONV_B, C) + cbb_ref[...])
    u_c = keep(hcb[:, D_C:2 * D_C] * hcb[:, 0:D_C])
    y_c = hcb[:, 2 * D_C:3 * D_C] * _causal_conv(extc, u_c, cwc_ref, CONV_C, C)
    return qkv_act, xbc_act, y_c


def _proj_kernel(x_ref, nw_ref, w_ref, *out_refs):
    hn = _rmsnorm(x_ref[...], nw_ref[...]).astype(BF16)
    for o_ref, off, width in zip(out_refs, PROJ_OFFSETS, PROJ_WIDTHS):
        o_ref[...] = jnp.dot(hn, w_ref[:, off:off + width], preferred_element_type=F32)


def _proj_call(x, norm_w, w_packed, layer, tm):
    T = x.shape[0]
    const = lambda i: (0, 0)
    return pl.pallas_call(
        _proj_kernel,
        grid=(T // tm,),
        in_specs=[
            pl.BlockSpec((tm, D_MODEL), lambda i: (i, 0)),
            pl.BlockSpec((1, D_MODEL), const),
            pl.BlockSpec((None, D_MODEL, D_PROJ_PACKED), lambda i: (layer, 0, 0), pipeline_mode=pl.Buffered(1)),
        ],
        out_specs=[pl.BlockSpec((tm, w), lambda i: (i, 0)) for w in PROJ_WIDTHS],
        out_shape=[jax.ShapeDtypeStruct((T, w), F32) for w in PROJ_WIDTHS],
        compiler_params=pltpu.CompilerParams(
            dimension_semantics=("parallel",), vmem_limit_bytes=VMEM_LIMIT_BYTES),
        name="proj",
    )(x, norm_w.reshape(1, D_MODEL), w_packed)


def _proj_conv_kernel(x_ref, nw_ref, w_ref, cwa_ref, cwb_ref, cbb_ref, cwc_ref, *rest, tm, pad):
    n_ext = (D_QKV_A + D_XBC + D_C) // CONV_COLS
    (qkv_o, za_o, xbc_o, zb_o, yc_o, small_o, ha_o, hb_o, hc_o) = rest[-9 - n_ext:-n_ext]
    exts = rest[-n_ext:]
    j = lax.rem(pl.program_id(0), TILES_PER_PROMPT)

    @pl.when(j == 0)
    def _():
        for ext in exts:
            ext[...] = jnp.zeros((HIST_ROWS, CONV_COLS), F32)

    valid = j * tm + lax.broadcasted_iota(jnp.int32, (tm, 1), 0) >= pad
    hn = _rmsnorm(jnp.where(valid, x_ref[...], 0.0), nw_ref[...]).astype(BF16)
    mm = lambda off, width: jnp.dot(hn, w_ref[:, off:off + width], preferred_element_type=F32)
    off_qkv, off_za, off_xbc, off_zb, off_hcb, off_small = PROJ_OFFSETS

    def carry(ext, hist_ref, n, col0):
        hist_ref[:, col0:col0 + CONV_COLS] = ext[HIST_ROWS - n:HIST_ROWS, :]

    jobs = []
    for k in range(D_QKV_A // CONV_COLS):
        def tail(raw, k=k, ext=exts[k]):
            col0 = k * CONV_COLS
            qkv_o[:, col0:col0 + CONV_COLS] = _silu(_causal_conv(ext, raw, cwa_ref, CONV_A, tm, col0))
            carry(ext, ha_o, CONV_A - 1, col0)
        jobs.append((functools.partial(mm, off_qkv + k * CONV_COLS, CONV_COLS), tail))
    for k in range(D_XBC // CONV_COLS):
        def tail(raw, k=k, ext=exts[D_QKV_A // CONV_COLS + k]):
            col0 = k * CONV_COLS
            conv = _causal_conv(ext, raw, cwb_ref, CONV_B, tm, col0)
            xbc_o[:, col0:col0 + CONV_COLS] = _silu(conv + cbb_ref[:, col0:col0 + CONV_COLS])
            carry(ext, hb_o, CONV_B - 1, col0)
        jobs.append((functools.partial(mm, off_xbc + k * CONV_COLS, CONV_COLS), tail))

    def tail_c(hcb):
        u_c = hcb[:, D_C:2 * D_C] * hcb[:, 0:D_C]
        for k in range(D_C // CONV_COLS):
            col0, ext = k * CONV_COLS, exts[(D_QKV_A + D_XBC) // CONV_COLS + k]
            conv = _causal_conv(ext, u_c[:, col0:col0 + CONV_COLS], cwc_ref, CONV_C, tm, col0)
            yc_o[:, col0:col0 + CONV_COLS] = hcb[:, 2 * D_C + col0:2 * D_C + col0 + CONV_COLS] * conv
            carry(ext, hc_o, CONV_C - 1, col0)
    jobs.append((functools.partial(mm, off_hcb, D_HCB), tail_c))

    def tail_plain(raws):
        za_o[...], zb_o[...], small_o[...] = raws
    jobs.append((lambda: (mm(off_za, D_A), mm(off_zb, D_B), mm(off_small, D_SMALL)), tail_plain))

    pending = None
    for matmuls, tail in jobs:
        raw = matmuls()
        if pending is not None:
            pending[1](pending[0])
        pending = (raw, tail)
    pending[1](pending[0])


def _proj_conv_call(x, norm_w, w_packed, layer, conv_w, stacked, out_layer, depth, *, B, pad):
    T = x.shape[0]
    tm = T // (B * TILES_PER_PROMPT)
    const = lambda i: (0, 0)
    row = lambda i: (i, 0)
    hist = lambda i: (out_layer, i // TILES_PER_PROMPT, 0, 0)
    cwa, cwb, cbb, cwc = conv_w
    in_specs = [
        pl.BlockSpec((tm, D_MODEL), row),
        pl.BlockSpec((1, D_MODEL), const),
        pl.BlockSpec((None, D_MODEL, D_PROJ_PACKED), lambda i: (layer, 0, 0), pipeline_mode=pl.Buffered(1)),
        pl.BlockSpec((CONV_A, D_QKV_A), const),
        pl.BlockSpec((CONV_B, D_XBC), const),
        pl.BlockSpec((1, D_XBC), const),
        pl.BlockSpec((CONV_C, D_C), const),
    ]
    stacked = list(stacked or [])
    n_in = len(in_specs)
    in_specs += [pl.BlockSpec(memory_space=pl.ANY) for _ in stacked]
    n_act = len(ACT_WIDTHS)
    outs = pl.pallas_call(
        functools.partial(_proj_conv_kernel, tm=tm, pad=pad),
        grid=(T // tm,),
        in_specs=in_specs,
        out_specs=([pl.BlockSpec((tm, w), row) for w in ACT_WIDTHS]
                   + [pl.BlockSpec((None, None, n, w), hist) for n, w in zip(HIST_LENS, HIST_WIDTHS)]),
        out_shape=([jax.ShapeDtypeStruct((T, w), F32) for w in ACT_WIDTHS]
                   + [jax.ShapeDtypeStruct((depth, B, n, w), F32) for n, w in zip(HIST_LENS, HIST_WIDTHS)]),
        scratch_shapes=[pltpu.VMEM((HIST_ROWS, CONV_COLS), F32)
                        for _ in range(sum(HIST_WIDTHS) // CONV_COLS)],
        input_output_aliases={n_in + i: n_act + i for i in range(len(stacked))},
        compiler_params=pltpu.CompilerParams(
            dimension_semantics=("arbitrary",), vmem_limit_bytes=VMEM_LIMIT_BYTES),
        name="proj_conv",
    )(x, norm_w.reshape(1, D_MODEL), w_packed, cwa, cwb, cbb, cwc, *stacked)
    return outs[:n_act], outs[n_act:]


def _mix_kernel(*refs, G, C, pad, nC, preact):
    if preact:
        (qkv_ref, za_ref, xbc_ref, zb_ref, small_ref, small_next_ref,
         s0a_ref, s0b_ref, hp_ref, ona_ref, onb_ref) = refs[:11]
        mixed_ref, sa_out, sb_out, sa, sb, gate_s = refs[-6:]
    else:
        (qkv_ref, za_ref, xbc_ref, zb_ref, hcb_ref, small_ref, hista_ref, histb_ref, histc_ref,
         s0a_ref, s0b_ref, cwa_ref, cwb_ref, cbb_ref, cwc_ref, hp_ref, ona_ref, onb_ref) = refs[:18]
        (mixed_ref, yc_ref, sa_out, sb_out, ha_out, hb_out, hc_out, exta, extb, extc, sa, sb) = refs[-12:]
        exts, hist_refs, hist_outs = (exta, extb, extc), (hista_ref, histb_ref, histc_ref), (ha_out, hb_out, hc_out)
    c = pl.program_id(1)

    @pl.when(c == 0)
    def _():
        sa[...] = s0a_ref[...]
        sb[...] = s0b_ref[...]
        if not preact:
            for ext, h_ref in zip(exts, hist_refs):
                ext[...] = h_ref[...]

    row = lax.broadcasted_iota(jnp.int32, (C, C), 0)
    col = lax.broadcasted_iota(jnp.int32, (C, C), 1)
    tril = row >= col
    strict = row > col
    tril_b = jnp.where(tril, 1.0, 0.0).astype(BF16)

    if pad:
        tok = c * C + lax.broadcasted_iota(jnp.int32, (C, 1), 0)
        valid = tok >= pad
        keep = lambda v: jnp.where(valid, v, 0.0)
    else:
        keep = lambda v: v

    seqs = range(G)
    lane_col = lambda a, lane: a[:, lane:lane + 1]

    def gates(small, keep_rows):
        sp_ = keep_rows(_softplus(small + hp_ref[0:1, :]))
        G_ = _cumsum_rows(-jnp.exp(hp_ref[1:2, :]) * sp_, tril_b)
        return sp_, jax.nn.sigmoid(small), G_, G_.T, jnp.exp(G_), jnp.exp(G_[C - 1:C, :] - G_)

    if preact:
        @pl.when(c == 0)
        def _():
            for g in seqs:
                for i, a in enumerate(gates(small_ref[g], keep)):
                    gate_s[g, i] = a
        gate_vals = [[gate_s[g, i] for i in range(N_GATE_SLABS)] for g in seqs]
    else:
        gate_vals = [gates(small_ref[g], keep) for g in seqs]
    sp, beta_all, G_all, GT_all, eG_all, eG_rem = (list(v) for v in zip(*gate_vals))
    eG_last = [x[C - 1:C, :] for x in eG_all]

    def decay_matrix(g, lane):
        dlog = lane_col(G_all[g], lane) - GT_all[g][lane:lane + 1, :]
        return jnp.where(tril, jnp.exp(jnp.minimum(dlog, 0.0)), 0.0)

    if preact:
        qkv = [qkv_ref[g] for g in seqs]
        xbc = [xbc_ref[g] for g in seqs]
    else:
        qkv, xbc = [], []
        for g in seqs:
            qkv_act, xbc_act, y_c = _conv_front_end(
                qkv_ref[g], xbc_ref[g], hcb_ref[g], [ext.at[g] for ext in exts],
                (cwa_ref, cwb_ref, cbb_ref, cwc_ref), keep, C)
            qkv.append(qkv_act)
            xbc.append(xbc_act)
            yc_ref[g] = y_c

    chains = [(g, h) for g in seqs for h in range(H_A)]
    qs, ks, vs, kbs, egs, kdts = [], [], [], [], [], []
    for g, h in chains:
        q = qkv[g][:, h * DK_A:(h + 1) * DK_A]
        k = qkv[g][:, D_QK_A + h * DK_A:D_QK_A + (h + 1) * DK_A]
        v = qkv[g][:, 2 * D_QK_A + h * DV_A:2 * D_QK_A + (h + 1) * DV_A]
        q = q * lax.rsqrt(jnp.sum(q * q, axis=-1, keepdims=True) + EPS) * (DK_A ** -0.5)
        k = k * lax.rsqrt(jnp.sum(k * k, axis=-1, keepdims=True) + EPS)
        beta = lane_col(beta_all[g], LANE_BETA + h)
        qs.append(q)
        ks.append(k)
        vs.append(v * beta)
        kbs.append(k * beta)
        egs.append(lane_col(eG_all[g], LANE_GA + h))
        kdts.append((k * lane_col(eG_rem[g], LANE_GA + h)).T)
    kqs = [_dot_nt(jnp.concatenate([kb, q], axis=0), k) for kb, q, k in zip(kbs, qs, ks)]
    decs = [decay_matrix(g, LANE_GA + h) for g, h in chains]
    nmats = [jnp.where(strict, kq[:C] * dec, 0.0) for kq, dec in zip(kqs, decs)]
    qks = [kq[C:] * dec for kq, dec in zip(kqs, decs)]
    heads_per_group = H_B // G_B
    gw = D_B // G_B
    grp_of = lambda h: h // heads_per_group
    bgs = {(g, grp): xbc[g][:, D_B + grp * N_B:D_B + (grp + 1) * N_B] for g in seqs for grp in range(G_B)}
    cgs = {(g, grp): xbc[g][:, D_B + (G_B + grp) * N_B:D_B + (G_B + grp + 1) * N_B]
           for g in seqs for grp in range(G_B)}
    heads = [(g, h) for g in seqs for h in range(H_B)]
    x_hs = [xbc[g][:, h * P_B:(h + 1) * P_B] for g, h in heads]
    xdts = [x_h * lane_col(sp[g], LANE_GB + h) for (g, h), x_h in zip(heads, x_hs)]
    xdtts = [xdt.T for xdt in xdts]
    cdecs = [cgs[(g, grp_of(h))] * lane_col(eG_all[g], LANE_GB + h) for g, h in heads]
    bdecs = [bgs[(g, grp_of(h))] * lane_col(eG_rem[g], LANE_GB + h) for g, h in heads]
    decs_b = [decay_matrix(g, LANE_GB + h) for g, h in heads]
    tinvs = _unit_lower_inverse_offdiag(nmats, row, col, C)
    rhss = [jnp.concatenate([vb, kb * eg], axis=1) for vb, kb, eg in zip(vs, kbs, egs)]
    sols = [rhs + _dot(t, rhs) for t, rhs in zip(tinvs, rhss)]
    if preact:
        for g in seqs:
            for i, a in enumerate(gates(small_next_ref[g], lambda v: v)):
                gate_s[g, i] = a
    s_olds = [sa[g, h] for g, h in chains]
    wss = [_dot(jnp.concatenate([sol[:, DV_A:], q * eg], axis=0), s_old)
           for sol, q, eg, s_old in zip(sols, qs, egs, s_olds)]
    ws_ = [sol[:, :DV_A] - ws[:C] for sol, ws in zip(sols, wss)]
    os_ = [ws[C:] + _dot(qk, w) for ws, qk, w in zip(wss, qks, ws_)]
    for (g, h), kdt, w, s_old, o in zip(chains, kdts, ws_, s_olds, os_):
        sa[g, h] = lane_col(eG_last[g], LANE_GA + h) * s_old + _dot(kdt, w)
        gate = _silu(za_ref[g, :, h * DV_A:(h + 1) * DV_A])
        mixed_ref[g, :, h * DV_A:(h + 1) * DV_A] = _rmsnorm(o, ona_ref[...]) * gate

    cbts = {key: _dot_nt(cgs[key], bgs[key]) for key in bgs}
    h_olds = [sb[g, h] for g, h in heads]
    y_offs = [_dot_nt(cdec, h_old) for cdec, h_old in zip(cdecs, h_olds)]
    for (g, h), xdtt, bdec, h_old in zip(heads, xdtts, bdecs, h_olds):
        sb[g, h] = lane_col(eG_last[g], LANE_GB + h) * h_old + _dot(xdtt, bdec)
    ys = [y_off + _dot(cbts[(g, grp_of(h))] * dec, xdt) + hp_ref[2:3, LANE_GB + h:LANE_GB + h + 1] * x_h
          for (g, h), y_off, dec, xdt, x_h in zip(heads, y_offs, decs_b, xdts, x_hs)]
    for g in seqs:
        for grp in range(G_B):
            lo = g * H_B + grp * heads_per_group
            yg = jnp.concatenate(ys[lo:lo + heads_per_group], axis=1)
            yg = yg * _silu(zb_ref[g, :, grp * gw:(grp + 1) * gw])
            mixed_ref[g, :, D_A + grp * gw:D_A + (grp + 1) * gw] = _rmsnorm(
                yg, onb_ref[:, grp * gw:(grp + 1) * gw])

    @pl.when(c == nC - 1)
    def _():
        sa_out[...] = sa[...]
        sb_out[...] = sb[...]
        if not preact:
            for o_ref, ext, n in zip(hist_outs, exts, HIST_LENS):
                o_ref[...] = ext[:, HIST_ROWS - n:HIST_ROWS, :]


def _mix_call(acts, hists, s0a, s0b, layer, weights, stacked, out_layer, depth, *, B, nC, C, G, pad, preact):
    L = nC * C
    conv_w, hp, ona, onb = weights
    tok = lambda b, c: (b, c, 0)
    per_b3 = lambda b, c: (b, 0, 0)
    out_b4 = lambda b, c: (out_layer, b, 0, 0)
    out_b5 = lambda b, c: (out_layer, b, 0, 0, 0)
    const = lambda b, c: (0, 0)
    state_specs = [
        pl.BlockSpec((None, G, H_A, DK_A, DV_A), lambda b, c: (layer, b, 0, 0, 0)),
        pl.BlockSpec((None, G, H_B, P_B, N_B), lambda b, c: (layer, b, 0, 0, 0)),
    ]
    head_specs = [pl.BlockSpec((8, D_SMALL), const), pl.BlockSpec((1, DV_A), const), pl.BlockSpec((1, D_B), const)]
    acts3 = [a.reshape(B, L, a.shape[-1]) for a in acts]
    in_specs = [pl.BlockSpec((G, C, a.shape[-1]), tok) for a in acts3]
    operands = list(acts3)
    if preact:
        assert C == D_SMALL, "the carried gate slabs assume (C, 128) == (128, C)"
        in_specs.append(pl.BlockSpec((G, C, D_SMALL), lambda b, c: (b, jnp.minimum(c + 1, nC - 1), 0)))
        operands.append(acts3[-1])
    else:
        in_specs += [pl.BlockSpec((G, HIST_ROWS, w), per_b3) for w in HIST_WIDTHS]
        operands += list(hists)
    in_specs += state_specs
    operands += [s0a, s0b]
    if not preact:
        in_specs += [pl.BlockSpec((CONV_A, D_QKV_A), const), pl.BlockSpec((CONV_B, D_XBC), const),
                     pl.BlockSpec((1, D_XBC), const), pl.BlockSpec((CONV_C, D_C), const)]
        operands += list(conv_w)
    in_specs += head_specs
    operands += [hp, ona, onb]

    out_specs = [pl.BlockSpec((G, C, D_AB), tok)]
    out_shape = [jax.ShapeDtypeStruct((B, L, D_AB), F32)]
    if not preact:
        out_specs.append(pl.BlockSpec((G, C, D_C), tok))
        out_shape.append(jax.ShapeDtypeStruct((B, L, D_C), F32))
    n_plain = len(out_specs)
    out_specs += [pl.BlockSpec((None, G, H_A, DK_A, DV_A), out_b5), pl.BlockSpec((None, G, H_B, P_B, N_B), out_b5)]
    out_shape += [jax.ShapeDtypeStruct((depth, B, H_A, DK_A, DV_A), F32),
                  jax.ShapeDtypeStruct((depth, B, H_B, P_B, N_B), F32)]
    scratch = []
    if not preact:
        out_specs += [pl.BlockSpec((None, G, n, w), out_b4) for n, w in zip(HIST_LENS, HIST_WIDTHS)]
        out_shape += [jax.ShapeDtypeStruct((depth, B, n, w), F32) for n, w in zip(HIST_LENS, HIST_WIDTHS)]
        scratch += [pltpu.VMEM((G, HIST_ROWS, w), F32) for w in HIST_WIDTHS]
    scratch += [pltpu.VMEM((G, H_A, DK_A, DV_A), F32), pltpu.VMEM((G, H_B, P_B, N_B), F32)]
    if preact:
        scratch.append(pltpu.VMEM((G, N_GATE_SLABS, C, D_SMALL), F32))

    stacked = list(stacked or [])
    n_in = len(in_specs)
    in_specs += [pl.BlockSpec(memory_space=pl.ANY) for _ in stacked]
    outs = pl.pallas_call(
        functools.partial(_mix_kernel, G=G, C=C, pad=pad, nC=nC, preact=preact),
        grid=(B // G, nC),
        in_specs=in_specs,
        out_specs=out_specs,
        out_shape=out_shape,
        scratch_shapes=scratch,
        input_output_aliases={n_in + i: n_plain + i for i in range(len(stacked))},
        compiler_params=pltpu.CompilerParams(
            dimension_semantics=("parallel", "arbitrary"), vmem_limit_bytes=VMEM_LIMIT_BYTES),
        name="mix",
    )(*operands, *stacked)
    plain = [o.reshape(B * L, o.shape[-1]) for o in outs[:n_plain]]
    return plain, list(outs[n_plain:])


def _route(logits):
    lg = [logits[:, g:g + 1] for g in range(N_EGROUPS)]
    m = functools.reduce(jnp.maximum, lg)
    ex = [jnp.exp(l - m) for l in lg]
    den = functools.reduce(lambda a, b: a + b, ex)
    p = [e / den for e in ex]
    p_top, g_idx = p[0], jnp.zeros_like(p[0], dtype=jnp.int32)
    for g in range(1, N_EGROUPS):
        better = p[g] > p_top
        g_idx = jnp.where(better, g, g_idx)
        p_top = jnp.where(better, p[g], p_top)
    g_hot = [g_idx == g for g in range(N_EGROUPS)]
    le_sel = []
    for j in range(EXP_PER_GROUP):
        acc = None
        for g in range(N_EGROUPS):
            lane = N_EGROUPS + g * EXP_PER_GROUP + j
            t = jnp.where(g_hot[g], logits[:, lane:lane + 1], 0.0)
            acc = t if acc is None else acc + t
        le_sel.append(acc)
    m2 = functools.reduce(jnp.maximum, le_sel)
    ex2 = [jnp.exp(l - m2) for l in le_sel]
    den2 = functools.reduce(lambda a, b: a + b, ex2)
    s = [e / den2 for e in ex2]
    v1, i1 = s[0], jnp.zeros_like(g_idx)
    for j in range(1, EXP_PER_GROUP):
        better = s[j] > v1
        i1 = jnp.where(better, j, i1)
        v1 = jnp.where(better, s[j], v1)
    v2, i2 = jnp.full_like(v1, -1.0), jnp.zeros_like(g_idx)
    for j in range(EXP_PER_GROUP):
        s_rest = jnp.where(i1 == j, -1.0, s[j])
        better = s_rest > v2
        i2 = jnp.where(better, j, i2)
        v2 = jnp.where(better, s_rest, v2)
    tot = v1 + v2
    w1 = v1 / tot * p_top
    w2 = v2 / tot * p_top
    comb = []
    for g in range(N_EGROUPS):
        for j in range(EXP_PER_GROUP):
            inner = jnp.where(i1 == j, w1, jnp.where(i2 == j, w2, 0.0))
            comb.append(jnp.where(g_hot[g], inner, 0.0))
    return comb


def _post_kernel(x_ref, mab_ref, yc_ref, wout_ref, n2_ref, wr_ref, wg_ref, wu_ref, wd_ref, fn_ref,
                 o_ref, *, final):
    mixed = jnp.concatenate([mab_ref[...].astype(BF16), yc_ref[...].astype(BF16)], axis=1)
    x1 = x_ref[...] + jnp.dot(mixed, wout_ref[...], preferred_element_type=F32)
    hb = _rmsnorm(x1, n2_ref[...]).astype(BF16)
    comb = _route(jnp.dot(hb, wr_ref[...], preferred_element_type=F32))

    def activations(grp):
        acts = []
        for e in range(grp * EXPERT_BATCH, (grp + 1) * EXPERT_BATCH):
            hg = jnp.dot(hb, wg_ref[e], preferred_element_type=F32)
            hu = jnp.dot(hb, wu_ref[e], preferred_element_type=F32)
            acts.append((_silu(hg) * hu * comb[e]).astype(BF16))
        return jnp.concatenate(acts, axis=1)

    n_batches = N_EXPERTS // EXPERT_BATCH
    kb = EXPERT_BATCH * D_FF_E
    acts = [activations(0)]
    acc = x1
    for grp in range(n_batches):
        if grp + 1 < n_batches:
            acts.append(activations(grp + 1))
        acc = acc + jnp.dot(acts[grp], wd_ref[grp * kb:(grp + 1) * kb, :], preferred_element_type=F32)
    if final:
        acc = _rmsnorm(acc, fn_ref[...])
    o_ref[...] = acc


def _post_call(x, mixed_ab, y_c, layer, w_out, n2, w_router, w_gate, w_up, w_down, fn, *, tm, final, window=None):
    single = dict(pipeline_mode=pl.Buffered(1))
    const2 = lambda s: (0, 0)
    layer3 = lambda s: (layer, 0, 0)
    layer4 = lambda s: (layer, 0, 0, 0)
    if window is None:
        n_tiles = x.shape[0] // tm
        tok_specs = [pl.BlockSpec((tm, w), lambda s: (s, 0)) for w in (D_MODEL, D_AB, D_C)]
    else:
        B, L, start, length = window
        per_seq = length // tm
        n_tiles = B * per_seq
        window_row = lambda s: (pl.multiple_of((s // per_seq) * L + start + (s % per_seq) * tm, 8), 0)
        tok_specs = [pl.BlockSpec((pl.Element(tm), pl.Element(w)), window_row) for w in (D_MODEL, D_AB, D_C)]
    return pl.pallas_call(
        functools.partial(_post_kernel, final=final),
        grid=(n_tiles,),
        in_specs=tok_specs + [
            pl.BlockSpec((None, D_MIX, D_MODEL), layer3, **single),
            pl.BlockSpec((1, D_MODEL), const2),
            pl.BlockSpec((None, D_MODEL, D_SMALL), layer3, **single),
            pl.BlockSpec((None, N_EXPERTS, D_MODEL, D_FF_E), layer4, **single),
            pl.BlockSpec((None, N_EXPERTS, D_MODEL, D_FF_E), layer4, **single),
            pl.BlockSpec((None, N_EXPERTS * D_FF_E, D_MODEL), layer3, **single),
            pl.BlockSpec((1, D_MODEL), const2),
        ],
        out_specs=pl.BlockSpec((tm, D_MODEL), lambda s: (s, 0)),
        out_shape=jax.ShapeDtypeStruct((n_tiles * tm, D_MODEL), F32),
        compiler_params=pltpu.CompilerParams(
            dimension_semantics=("parallel",), vmem_limit_bytes=VMEM_LIMIT_BYTES),
        name="post",
    )(x, mixed_ab, y_c, w_out, n2, w_router, w_gate, w_up, w_down, fn)


def _pack_w_in_kernel(w_ref, o_ref):
    o = [0]
    for s in (D_QKV_A, D_A, H_A, H_A, D_XBC, D_B, H_B, D_C, D_C, D_C):
        o.append(o[-1] + s)
    pos = 0
    for lo, hi in ((o[0], o[2]), (o[4], o[6]), (o[7], o[10]), (o[2], o[4]), (o[6], o[7])):
        o_ref[:, pos:pos + hi - lo] = w_ref[:, lo:hi].astype(BF16)
        pos += hi - lo
    o_ref[:, pos:] = jnp.zeros((o_ref.shape[0], D_PROJ_PACKED - pos), BF16)


def _pack_w_in(w):
    depth, _, d_proj = w.shape
    rows = 256
    return pl.pallas_call(
        _pack_w_in_kernel,
        grid=(depth, D_MODEL // rows),
        in_specs=[pl.BlockSpec((None, rows, d_proj), lambda l, i: (l, i, 0))],
        out_specs=pl.BlockSpec((None, rows, D_PROJ_PACKED), lambda l, i: (l, i, 0)),
        out_shape=jax.ShapeDtypeStruct((depth, D_MODEL, D_PROJ_PACKED), BF16),
        compiler_params=pltpu.CompilerParams(
            dimension_semantics=("parallel", "parallel"), vmem_limit_bytes=VMEM_LIMIT_BYTES),
        name="pack_w_in",
    )(w)


def _head_params(dt_bias_a, dt_bias_b, a_log_a, a_log_b, d_skip_b):
    z4 = jnp.zeros((H_A,), F32)
    tail = jnp.zeros((D_SMALL - 2 * H_A - H_B,), F32)
    rows = [
        jnp.concatenate([z4, dt_bias_a, dt_bias_b, tail]),
        jnp.concatenate([z4, a_log_a, a_log_b, tail]),
        jnp.concatenate([z4, z4, d_skip_b, tail]),
    ]
    return jnp.concatenate([jnp.stack(rows), jnp.zeros((8 - len(rows), D_SMALL), F32)], axis=0)


def _pad_hist(h):
    return jnp.pad(h, ((0, 0), (0, 0), (HIST_ROWS - h.shape[2], 0), (0, 0)))


def kernel(x_prompt, x_sample, state_gdn, state_gdn_conv, state_ssd, state_ssd_conv, state_sconv,
           meta_tokens, norm1_w, w_in, conv_a_w, a_log_a, dt_bias_a, onorm_a_w, conv_b_w, conv_b_b,
           a_log_b, dt_bias_b, d_skip_b, onorm_b_w, conv_c_w, w_out, norm2_w, w_router_g,
           w_router_e, w_gate, w_up, w_down, final_norm_w):
    bp, seq, _ = x_prompt.shape
    bs, dseq, _ = x_sample.shape
    depth = w_in.shape[0]
    pad = CHUNK - N_META
    lp = pad + N_META + seq
    ncp = lp // CHUNK

    meta = jnp.broadcast_to(meta_tokens[None], (bp, N_META, D_MODEL))
    x_p = jnp.concatenate([jnp.zeros((bp, pad, D_MODEL), F32), meta, x_prompt], axis=1)
    x_p = x_p.reshape(bp * lp, D_MODEL)
    x_s = x_sample.reshape(bs * dseq, D_MODEL)

    zero_sa = jnp.zeros((1, bp, H_A, DK_A, DV_A), F32)
    zero_sb = jnp.zeros((1, bp, H_B, P_B, N_B), F32)
    hists_s = [_pad_hist(state_gdn_conv), _pad_hist(state_ssd_conv), _pad_hist(state_sconv)]

    fn = final_norm_w.reshape(1, D_MODEL)
    w_packed = _pack_w_in(w_in)
    w_router = jnp.concatenate(
        [w_router_g, w_router_e, jnp.zeros((depth, D_MODEL, D_SMALL - N_EGROUPS - N_EXPERTS), F32)],
        axis=-1).astype(BF16)
    w_out_b, w_gate_b, w_up_b = w_out.astype(BF16), w_gate.astype(BF16), w_up.astype(BF16)
    w_down_b = w_down.reshape(depth, N_EXPERTS * D_FF_E, D_MODEL).astype(BF16)
    hist_p = st_p = st_s = None
    for l in range(depth):
        conv_w = (conv_a_w[l], conv_b_w[l], conv_b_b[l].reshape(1, D_XBC), conv_c_w[l])
        mix_w = (conv_w, _head_params(dt_bias_a[l], dt_bias_b[l], a_log_a[l], a_log_b[l], d_skip_b[l]),
                 onorm_a_w[l].reshape(1, DV_A), onorm_b_w[l].reshape(1, D_B))
        post_w = (l, w_out_b, norm2_w[l].reshape(1, D_MODEL), w_router, w_gate_b, w_up_b, w_down_b, fn)
        final = l == depth - 1

        acts_p, hist_p = _proj_conv_call(x_p, norm1_w[l], w_packed, l, conv_w, hist_p, l, depth, B=bp, pad=pad)
        qkv_act, za, xbc_act, zb, yc_p, small = acts_p
        (mixed_p,), st_p = _mix_call((qkv_act, za, xbc_act, zb, small), None, zero_sa, zero_sb, 0, mix_w,
                                     st_p, l, depth, B=bp, nC=ncp, C=CHUNK, G=G_PROMPT, pad=pad, preact=True)
        window = (bp, lp, pad + N_META, seq) if final else None
        x_p = _post_call(x_p, mixed_p, yc_p, *post_w, tm=TM_POST, final=final, window=window)

        proj_s = _proj_call(x_s, norm1_w[l], w_packed, l, tm=TM_PROJ)
        (mixed_s, yc_s), st_s = _mix_call(proj_s, [h[l] for h in hists_s], state_gdn, state_ssd, l, mix_w,
                                          st_s, l, depth, B=bs, nC=1, C=dseq, G=G_SAMPLE, pad=0, preact=False)
        x_s = _post_call(x_s, mixed_s, yc_s, *post_w, tm=TM_POST, final=final)

    y_sample = x_s.reshape(bs, dseq, D_MODEL)
    return (x_p.reshape(bp, seq, D_MODEL), y_sample, st_p[0], hist_p[0], st_p[1], hist_p[1], hist_p[2],
            st_s[0], st_s[2], st_s[1], st_s[3], st_s[4])
```

```python
import functools

import jax
import jax.numpy as jnp
from jax import lax
from jax.experimental import pallas as pl
from jax.experimental.pallas import tpu as pltpu

F32 = jnp.float32
BF16 = jnp.bfloat16

D_MODEL = 1024
N_META = 16
H_A = 4
DK_A = 128
DV_A = 128
D_QK_A = H_A * DK_A
D_A = H_A * DV_A
D_QKV_A = 2 * D_QK_A + D_A
CONV_A = 4
H_B = 8
P_B = 64
D_B = H_B * P_B
G_B = 2
N_B = 128
D_XBC = D_B + 2 * G_B * N_B
CONV_B = 4
D_C = 512
CONV_C = 3
D_AB = D_A + D_B
D_MIX = D_AB + D_C
N_EGROUPS = 4
EXP_PER_GROUP = 4
N_EXPERTS = N_EGROUPS * EXP_PER_GROUP
D_FF_E = 256
EPS = 1e-6
CHUNK = 128

D_HCB = 3 * D_C
D_SMALL = 128
PROJ_WIDTHS = (D_QKV_A, D_A, D_XBC, D_B, D_HCB, D_SMALL)
D_PROJ_PACKED = sum(PROJ_WIDTHS)
PROJ_OFFSETS = tuple(sum(PROJ_WIDTHS[:i]) for i in range(len(PROJ_WIDTHS)))
ACT_WIDTHS = (D_QKV_A, D_A, D_XBC, D_B, D_C, D_SMALL)
HIST_WIDTHS = (D_QKV_A, D_XBC, D_C)
HIST_LENS = (CONV_A - 1, CONV_B - 1, CONV_C - 1)
LANE_BETA = 0
LANE_GA = H_A
LANE_GB = 2 * H_A
N_GATE_SLABS = 4
HIST_ROWS = 8
G_PROMPT = 2
G_SAMPLE = 16
EXPERT_BATCH = 4
TM_PROJ = 512
TM_POST = 512
TILES_PER_PROMPT = 4
CONV_COLS = 512

VMEM_LIMIT_BYTES = 56 * 1024 * 1024


def _rmsnorm(x, w):
    return x * lax.rsqrt(jnp.mean(x * x, axis=-1, keepdims=True) + EPS) * w


def _silu(x):
    h = 0.5 * x
    return h * jnp.tanh(h) + h


def _softplus(x):
    return jnp.maximum(x, 0.0) + jnp.log1p(jnp.exp(-jnp.abs(x)))


def _dot(a, b):
    return jnp.dot(a.astype(BF16), b.astype(BF16), preferred_element_type=F32)


def _dot_nt(a, b):
    return lax.dot_general(a.astype(BF16), b.astype(BF16), (((1,), (1,)), ((), ())),
                           preferred_element_type=F32)


def _dot_tn(a, b):
    return lax.dot_general(a.astype(BF16), b.astype(BF16), (((0,), (0,)), ((), ())),
                           preferred_element_type=F32)


def _cumsum_rows(g, tril_b):
    g1 = g.astype(BF16)
    r1 = g - g1.astype(F32)
    g2 = r1.astype(BF16)
    g3 = (r1 - g2.astype(F32)).astype(BF16)
    d = functools.partial(jnp.dot, preferred_element_type=F32)
    return d(tril_b, g1) + d(tril_b, g2) + d(tril_b, g3)


def _unit_lower_inverse_offdiag(nmats, row, col, C):
    ds = [-jnp.where((row >> 1) == (col >> 1), n, 0.0) for n in nmats]
    s, ls = 2, 1
    while s < C:
        same_2s = (row >> (ls + 1)) == (col >> (ls + 1))
        same_s = (row >> ls) == (col >> ls)
        es = [jnp.where(same_2s, jnp.where(same_s, 0.0, n), 0.0) for n in nmats]
        ys = [e + _dot(d, e) for d, e in zip(ds, es)]
        ds = [d - (y + _dot(y, d)) for d, y in zip(ds, ys)]
        s, ls = 2 * s, ls + 1
    return ds


def _causal_conv(hist_ref, cur, w_ref, width, C, col0=0):
    n_cols = cur.shape[1]
    x3 = jnp.concatenate([hist_ref[...], cur], axis=0).reshape((C + HIST_ROWS) // 8, 8, n_cols)
    sub = lax.broadcasted_iota(jnp.int32, x3.shape, 1)

    def shift(a3, k):
        r = pltpu.roll(a3, k, axis=1)
        from_prev_tile = jnp.concatenate([r[:1], r[:-1]], axis=0)
        return jnp.where(sub < k, from_prev_tile, r)

    w = lambda i: w_ref[i:i + 1, col0:col0 + n_cols]
    if width == 4:
        x1 = shift(x3, 1)
        y3 = (x3 * w(3) + x1 * w(2)) + shift(x3 * w(1) + x1 * w(0), 2)
    else:
        assert width == 3
        y3 = x3 * w(2) + shift(x3, 1) * w(1) + shift(x3, 2) * w(0)
    hist_ref[...] = cur[C - HIST_ROWS:C]
    return y3.reshape(C + HIST_ROWS, n_cols)[HIST_ROWS:]


def _conv_front_end(raw_qkv, raw_xbc, hcb, exts, conv_w, keep, C):
    exta, extb, extc = exts
    cwa_ref, cwb_ref, cbb_ref, cwc_ref = conv_w
    qkv_act = _silu(_causal_conv(exta, keep(raw_qkv), cwa_ref, CONV_A, C))
    xbc_act = _silu(_causal_conv(extb, keep(raw_xbc), cwb_ref, CONV_B, C) + cbb_ref[...])
    u_c = keep(hcb[:, D_C:2 * D_C] * hcb[:, 0:D_C])
    y_c = hcb[:, 2 * D_C:3 * D_C] * _causal_conv(extc, u_c, cwc_ref, CONV_C, C)
    return qkv_act, xbc_act, y_c


def _proj_kernel(x_ref, nw_ref, w_ref, *out_refs):
    hn = _rmsnorm(x_ref[...], nw_ref[...]).astype(BF16)
    for o_ref, off, width in zip(out_refs, PROJ_OFFSETS, PROJ_WIDTHS):
        o_ref[...] = jnp.dot(hn, w_ref[:, off:off + width], preferred_element_type=F32)


def _proj_call(x, norm_w, w_packed, layer, tm):
    T = x.shape[0]
    const = lambda i: (0, 0)
    return pl.pallas_call(
        _proj_kernel,
        grid=(T // tm,),
        in_specs=[
            pl.BlockSpec((tm, D_MODEL), lambda i: (i, 0)),
            pl.BlockSpec((1, D_MODEL), const),
            pl.BlockSpec((None, D_MODEL, D_PROJ_PACKED), lambda i: (layer, 0, 0), pipeline_mode=pl.Buffered(1)),
        ],
        out_specs=[pl.BlockSpec((tm, w), lambda i: (i, 0)) for w in PROJ_WIDTHS],
        out_shape=[jax.ShapeDtypeStruct((T, w), F32) for w in PROJ_WIDTHS],
        compiler_params=pltpu.CompilerParams(
            dimension_semantics=("parallel",), vmem_limit_bytes=VMEM_LIMIT_BYTES),
        name="proj",
    )(x, norm_w.reshape(1, D_MODEL), w_packed)


def _proj_conv_kernel(x_ref, nw_ref, w_ref, cwa_ref, cwb_ref, cbb_ref, cwc_ref, *rest, tm, pad):
    n_ext = (D_QKV_A + D_XBC + D_C) // CONV_COLS
    (qkv_o, za_o, xbc_o, zb_o, yc_o, small_o, ha_o, hb_o, hc_o) = rest[-9 - n_ext:-n_ext]
    exts = rest[-n_ext:]
    j = lax.rem(pl.program_id(0), TILES_PER_PROMPT)

    @pl.when(j == 0)
    def _():
        for ext in exts:
            ext[...] = jnp.zeros((HIST_ROWS, CONV_COLS), F32)

    valid = j * tm + lax.broadcasted_iota(jnp.int32, (tm, 1), 0) >= pad
    hn = _rmsnorm(jnp.where(valid, x_ref[...], 0.0), nw_ref[...]).astype(BF16)
    mm = lambda off, width: jnp.dot(hn, w_ref[:, off:off + width], preferred_element_type=F32)
    off_qkv, off_za, off_xbc, off_zb, off_hcb, off_small = PROJ_OFFSETS

    def carry(ext, hist_ref, n, col0):
        hist_ref[:, col0:col0 + CONV_COLS] = ext[HIST_ROWS - n:HIST_ROWS, :]

    jobs = []
    for k in range(D_QKV_A // CONV_COLS):
        def tail(raw, k=k, ext=exts[k]):
            col0 = k * CONV_COLS
            qkv_o[:, col0:col0 + CONV_COLS] = _silu(_causal_conv(ext, raw, cwa_ref, CONV_A, tm, col0))
            carry(ext, ha_o, CONV_A - 1, col0)
        jobs.append((functools.partial(mm, off_qkv + k * CONV_COLS, CONV_COLS), tail))
    for k in range(D_XBC // CONV_COLS):
        def tail(raw, k=k, ext=exts[D_QKV_A // CONV_COLS + k]):
            col0 = k * CONV_COLS
            conv = _causal_conv(ext, raw, cwb_ref, CONV_B, tm, col0)
            xbc_o[:, col0:col0 + CONV_COLS] = _silu(conv + cbb_ref[:, col0:col0 + CONV_COLS])
            carry(ext, hb_o, CONV_B - 1, col0)
        jobs.append((functools.partial(mm, off_xbc + k * CONV_COLS, CONV_COLS), tail))

    def tail_c(hcb):
        u_c = hcb[:, D_C:2 * D_C] * hcb[:, 0:D_C]
        for k in range(D_C // CONV_COLS):
            col0, ext = k * CONV_COLS, exts[(D_QKV_A + D_XBC) // CONV_COLS + k]
            conv = _causal_conv(ext, u_c[:, col0:col0 + CONV_COLS], cwc_ref, CONV_C, tm, col0)
            yc_o[:, col0:col0 + CONV_COLS] = hcb[:, 2 * D_C + col0:2 * D_C + col0 + CONV_COLS] * conv
            carry(ext, hc_o, CONV_C - 1, col0)
    jobs.append((functools.partial(mm, off_hcb, D_HCB), tail_c))

    def tail_plain(raws):
        za_o[...], zb_o[...], small_o[...] = raws
    jobs.append((lambda: (mm(off_za, D_A), mm(off_zb, D_B), mm(off_small, D_SMALL)), tail_plain))

    pending = None
    for matmuls, tail in jobs:
        raw = matmuls()
        if pending is not None:
            pending[1](pending[0])
        pending = (raw, tail)
    pending[1](pending[0])


def _proj_conv_call(x, norm_w, w_packed, layer, conv_w, stacked, out_layer, depth, *, B, pad):
    T = x.shape[0]
    tm = T // (B * TILES_PER_PROMPT)
    const = lambda i: (0, 0)
    row = lambda i: (i, 0)
    hist = lambda i: (out_layer, i // TILES_PER_PROMPT, 0, 0)
    cwa, cwb, cbb, cwc = conv_w
    in_specs = [
        pl.BlockSpec((tm, D_MODEL), row),
        pl.BlockSpec((1, D_MODEL), const),
        pl.BlockSpec((None, D_MODEL, D_PROJ_PACKED), lambda i: (layer, 0, 0), pipeline_mode=pl.Buffered(1)),
        pl.BlockSpec((CONV_A, D_QKV_A), const),
        pl.BlockSpec((CONV_B, D_XBC), const),
        pl.BlockSpec((1, D_XBC), const),
        pl.BlockSpec((CONV_C, D_C), const),
    ]
    stacked = list(stacked or [])
    n_in = len(in_specs)
    in_specs += [pl.BlockSpec(memory_space=pl.ANY) for _ in stacked]
    n_act = len(ACT_WIDTHS)
    outs = pl.pallas_call(
        functools.partial(_proj_conv_kernel, tm=tm, pad=pad),
        grid=(T // tm,),
        in_specs=in_specs,
        out_specs=([pl.BlockSpec((tm, w), row) for w in ACT_WIDTHS]
                   + [pl.BlockSpec((None, None, n, w), hist) for n, w in zip(HIST_LENS, HIST_WIDTHS)]),
        out_shape=([jax.ShapeDtypeStruct((T, w), F32) for w in ACT_WIDTHS]
                   + [jax.ShapeDtypeStruct((depth, B, n, w), F32) for n, w in zip(HIST_LENS, HIST_WIDTHS)]),
        scratch_shapes=[pltpu.VMEM((HIST_ROWS, CONV_COLS), F32)
                        for _ in range(sum(HIST_WIDTHS) // CONV_COLS)],
        input_output_aliases={n_in + i: n_act + i for i in range(len(stacked))},
        compiler_params=pltpu.CompilerParams(
            dimension_semantics=("arbitrary",), vmem_limit_bytes=VMEM_LIMIT_BYTES),
        name="proj_conv",
    )(x, norm_w.reshape(1, D_MODEL), w_packed, cwa, cwb, cbb, cwc, *stacked)
    return outs[:n_act], outs[n_act:]


def _mix_kernel(*refs, G, C, pad, nC, preact):
    if preact:
        (qkv_ref, za_ref, xbc_ref, zb_ref, small_ref, small_next_ref,
         s0a_ref, s0b_ref, hp_ref, ona_ref, onb_ref) = refs[:11]
        mixed_ref, sa_out, sb_out, sa, sb, gate_s = refs[-6:]
    else:
        (qkv_ref, za_ref, xbc_ref, zb_ref, hcb_ref, small_ref, hista_ref, histb_ref, histc_ref,
         s0a_ref, s0b_ref, cwa_ref, cwb_ref, cbb_ref, cwc_ref, hp_ref, ona_ref, onb_ref) = refs[:18]
        (mixed_ref, yc_ref, sa_out, sb_out, ha_out, hb_out, hc_out, exta, extb, extc, sa, sb) = refs[-12:]
        exts, hist_refs, hist_outs = (exta, extb, extc), (hista_ref, histb_ref, histc_ref), (ha_out, hb_out, hc_out)
    c = pl.program_id(1)

    @pl.when(c == 0)
    def _():
        sa[...] = s0a_ref[...]
        sb[...] = s0b_ref[...]
        if not preact:
            for ext, h_ref in zip(exts, hist_refs):
                ext[...] = h_ref[...]

    row = lax.broadcasted_iota(jnp.int32, (C, C), 0)
    col = lax.broadcasted_iota(jnp.int32, (C, C), 1)
    tril = row >= col
    strict = row > col
    tril_b = jnp.where(tril, 1.0, 0.0).astype(BF16)

    if pad:
        tok = c * C + lax.broadcasted_iota(jnp.int32, (C, 1), 0)
        valid = tok >= pad
        keep = lambda v: jnp.where(valid, v, 0.0)
    else:
        keep = lambda v: v

    seqs = range(G)
    lane_col = lambda a, lane: a[:, lane:lane + 1]

    def gates(small, keep_rows):
        sp_ = keep_rows(_softplus(small + hp_ref[0:1, :]))
        G_ = _cumsum_rows(-jnp.exp(hp_ref[1:2, :]) * sp_, tril_b)
        return sp_, jax.nn.sigmoid(small), G_, G_.T

    if preact:
        @pl.when(c == 0)
        def _():
            for g in seqs:
                for i, a in enumerate(gates(small_ref[g], keep)):
                    gate_s[g, i] = a
        gate_vals = [[gate_s[g, i] for i in range(N_GATE_SLABS)] for g in seqs]
    else:
        gate_vals = [gates(small_ref[g], keep) for g in seqs]
    sp, beta_all, G_all, GT_all = (list(v) for v in zip(*gate_vals))
    eG_last = [jnp.exp(x[C - 1:C, :]) for x in G_all]

    def head_decays(g, lane):
        gb = jnp.broadcast_to(lane_col(G_all[g], lane), (C, D_SMALL))
        dlog = gb[:, :C] - GT_all[g][lane:lane + 1, :]
        dec = jnp.where(tril, jnp.exp(jnp.minimum(dlog, 0.0)), 0.0)
        return jnp.exp(gb), jnp.exp(G_all[g][C - 1:C, lane:lane + 1] - gb), dec

    if preact:
        qkv = [qkv_ref[g] for g in seqs]
        xbc = [xbc_ref[g] for g in seqs]
    else:
        qkv, xbc = [], []
        for g in seqs:
            qkv_act, xbc_act, y_c = _conv_front_end(
                qkv_ref[g], xbc_ref[g], hcb_ref[g], [ext.at[g] for ext in exts],
                (cwa_ref, cwb_ref, cbb_ref, cwc_ref), keep, C)
            qkv.append(qkv_act)
            xbc.append(xbc_act)
            yc_ref[g] = y_c

    chains = [(g, h) for g in seqs for h in range(H_A)]
    qs, ks, vs, kbs, egs, kdts, decs = [], [], [], [], [], [], []
    for g, h in chains:
        q = qkv[g][:, h * DK_A:(h + 1) * DK_A]
        k = qkv[g][:, D_QK_A + h * DK_A:D_QK_A + (h + 1) * DK_A]
        v = qkv[g][:, 2 * D_QK_A + h * DV_A:2 * D_QK_A + (h + 1) * DV_A]
        q = q * lax.rsqrt(jnp.sum(q * q, axis=-1, keepdims=True) + EPS) * (DK_A ** -0.5)
        k = k * lax.rsqrt(jnp.sum(k * k, axis=-1, keepdims=True) + EPS)
        beta = lane_col(beta_all[g], LANE_BETA + h)
        qs.append(q)
        ks.append(k)
        vs.append(v * beta)
        kbs.append(k * beta)
        eg, erem, dec = head_decays(g, LANE_GA + h)
        egs.append(eg)
        decs.append(dec)
        kdts.append((k * erem).T)
    kqs = [_dot_nt(jnp.concatenate([kb, q], axis=0), k) for kb, q, k in zip(kbs, qs, ks)]
    nmats = [jnp.where(strict, kq[:C] * dec, 0.0) for kq, dec in zip(kqs, decs)]
    qks = [kq[C:] * dec for kq, dec in zip(kqs, decs)]
    heads_per_group = H_B // G_B
    gw = D_B // G_B
    grp_of = lambda h: h // heads_per_group
    bgs = {(g, grp): xbc[g][:, D_B + grp * N_B:D_B + (grp + 1) * N_B] for g in seqs for grp in range(G_B)}
    cgs = {(g, grp): xbc[g][:, D_B + (G_B + grp) * N_B:D_B + (G_B + grp + 1) * N_B]
           for g in seqs for grp in range(G_B)}
    heads = [(g, h) for g in seqs for h in range(H_B)]
    x_hs = [xbc[g][:, h * P_B:(h + 1) * P_B] for g, h in heads]
    xdts = [x_h * lane_col(sp[g], LANE_GB + h) for (g, h), x_h in zip(heads, x_hs)]
    xdtts = [xdt.T for xdt in xdts]
    cdecs, bdecs, decs_b = [], [], []
    for g, h in heads:
        eg, erem, dec = head_decays(g, LANE_GB + h)
        cdecs.append(cgs[(g, grp_of(h))] * eg)
        bdecs.append(bgs[(g, grp_of(h))] * erem)
        decs_b.append(dec)
    tinvs = _unit_lower_inverse_offdiag(nmats, row, col, C)
    rhss = [jnp.concatenate([vb, kb * eg], axis=1) for vb, kb, eg in zip(vs, kbs, egs)]
    sols = [rhs + _dot(t, rhs) for t, rhs in zip(tinvs, rhss)]
    if preact:
        for g in seqs:
            for i, a in enumerate(gates(small_next_ref[g], lambda v: v)):
                gate_s[g, i] = a
    s_olds = [sa[g, h] for g, h in chains]
    wss = [_dot(jnp.concatenate([sol[:, DV_A:], q * eg], axis=0), s_old)
           for sol, q, eg, s_old in zip(sols, qs, egs, s_olds)]
    ws_ = [sol[:, :DV_A] - ws[:C] for sol, ws in zip(sols, wss)]
    os_ = [ws[C:] + _dot(qk, w) for ws, qk, w in zip(wss, qks, ws_)]
    for (g, h), kdt, w, s_old, o in zip(chains, kdts, ws_, s_olds, os_):
        sa[g, h] = lane_col(eG_last[g], LANE_GA + h) * s_old + _dot(kdt, w)
        gate = _silu(za_ref[g, :, h * DV_A:(h + 1) * DV_A])
        mixed_ref[g, :, h * DV_A:(h + 1) * DV_A] = _rmsnorm(o, ona_ref[...]) * gate

    cbts = {key: _dot_nt(cgs[key], bgs[key]) for key in bgs}
    h_olds = [sb[g, h] for g, h in heads]
    y_offs = [_dot_nt(cdec, h_old) for cdec, h_old in zip(cdecs, h_olds)]
    for (g, h), xdtt, bdec, h_old in zip(heads, xdtts, bdecs, h_olds):
        sb[g, h] = lane_col(eG_last[g], LANE_GB + h) * h_old + _dot(xdtt, bdec)
    ys = [y_off + _dot(cbts[(g, grp_of(h))] * dec, xdt) + hp_ref[2:3, LANE_GB + h:LANE_GB + h + 1] * x_h
          for (g, h), y_off, dec, xdt, x_h in zip(heads, y_offs, decs_b, xdts, x_hs)]
    for g in seqs:
        for grp in range(G_B):
            lo = g * H_B + grp * heads_per_group
            yg = jnp.concatenate(ys[lo:lo + heads_per_group], axis=1)
            yg = yg * _silu(zb_ref[g, :, grp * gw:(grp + 1) * gw])
            mixed_ref[g, :, D_A + grp * gw:D_A + (grp + 1) * gw] = _rmsnorm(
                yg, onb_ref[:, grp * gw:(grp + 1) * gw])

    @pl.when(c == nC - 1)
    def _():
        sa_out[...] = sa[...]
        sb_out[...] = sb[...]
        if not preact:
            for o_ref, ext, n in zip(hist_outs, exts, HIST_LENS):
                o_ref[...] = ext[:, HIST_ROWS - n:HIST_ROWS, :]


def _mix_call(acts, hists, s0a, s0b, layer, weights, stacked, out_layer, depth, *, B, nC, C, G, pad, preact):
    L = nC * C
    conv_w, hp, ona, onb = weights
    tok = lambda b, c: (b, c, 0)
    per_b3 = lambda b, c: (b, 0, 0)
    out_b4 = lambda b, c: (out_layer, b, 0, 0)
    out_b5 = lambda b, c: (out_layer, b, 0, 0, 0)
    const = lambda b, c: (0, 0)
    state_specs = [
        pl.BlockSpec((None, G, H_A, DK_A, DV_A), lambda b, c: (layer, b, 0, 0, 0)),
        pl.BlockSpec((None, G, H_B, P_B, N_B), lambda b, c: (layer, b, 0, 0, 0)),
    ]
    head_specs = [pl.BlockSpec((8, D_SMALL), const), pl.BlockSpec((1, DV_A), const), pl.BlockSpec((1, D_B), const)]
    acts3 = [a.reshape(B, L, a.shape[-1]) for a in acts]
    in_specs = [pl.BlockSpec((G, C, a.shape[-1]), tok) for a in acts3]
    operands = list(acts3)
    if preact:
        assert C == D_SMALL, "the carried gate slabs assume (C, 128) == (128, C)"
        in_specs.append(pl.BlockSpec((G, C, D_SMALL), lambda b, c: (b, jnp.minimum(c + 1, nC - 1), 0)))
        operands.append(acts3[-1])
    else:
        in_specs += [pl.BlockSpec((G, HIST_ROWS, w), per_b3) for w in HIST_WIDTHS]
        operands += list(hists)
    in_specs += state_specs
    operands += [s0a, s0b]
    if not preact:
        in_specs += [pl.BlockSpec((CONV_A, D_QKV_A), const), pl.BlockSpec((CONV_B, D_XBC), const),
                     pl.BlockSpec((1, D_XBC), const), pl.BlockSpec((CONV_C, D_C), const)]
        operands += list(conv_w)
    in_specs += head_specs
    operands += [hp, ona, onb]

    out_specs = [pl.BlockSpec((G, C, D_AB), tok)]
    out_shape = [jax.ShapeDtypeStruct((B, L, D_AB), F32)]
    if not preact:
        out_specs.append(pl.BlockSpec((G, C, D_C), tok))
        out_shape.append(jax.ShapeDtypeStruct((B, L, D_C), F32))
    n_plain = len(out_specs)
    out_specs += [pl.BlockSpec((None, G, H_A, DK_A, DV_A), out_b5), pl.BlockSpec((None, G, H_B, P_B, N_B), out_b5)]
    out_shape += [jax.ShapeDtypeStruct((depth, B, H_A, DK_A, DV_A), F32),
                  jax.ShapeDtypeStruct((depth, B, H_B, P_B, N_B), F32)]
    scratch = []
    if not preact:
        out_specs += [pl.BlockSpec((None, G, n, w), out_b4) for n, w in zip(HIST_LENS, HIST_WIDTHS)]
        out_shape += [jax.ShapeDtypeStruct((depth, B, n, w), F32) for n, w in zip(HIST_LENS, HIST_WIDTHS)]
        scratch += [pltpu.VMEM((G, HIST_ROWS, w), F32) for w in HIST_WIDTHS]
    scratch += [pltpu.VMEM((G, H_A, DK_A, DV_A), F32), pltpu.VMEM((G, H_B, P_B, N_B), F32)]
    if preact:
        scratch.append(pltpu.VMEM((G, N_GATE_SLABS, C, D_SMALL), F32))

    stacked = list(stacked or [])
    n_in = len(in_specs)
    in_specs += [pl.BlockSpec(memory_space=pl.ANY) for _ in stacked]
    outs = pl.pallas_call(
        functools.partial(_mix_kernel, G=G, C=C, pad=pad, nC=nC, preact=preact),
        grid=(B // G, nC),
        in_specs=in_specs,
        out_specs=out_specs,
        out_shape=out_shape,
        scratch_shapes=scratch,
        input_output_aliases={n_in + i: n_plain + i for i in range(len(stacked))},
        compiler_params=pltpu.CompilerParams(
            dimension_semantics=("parallel", "arbitrary"), vmem_limit_bytes=VMEM_LIMIT_BYTES),
        name="mix",
    )(*operands, *stacked)
    plain = [o.reshape(B * L, o.shape[-1]) for o in outs[:n_plain]]
    return plain, list(outs[n_plain:])


def _route(logits):
    lg = [logits[:, g:g + 1] for g in range(N_EGROUPS)]
    m = functools.reduce(jnp.maximum, lg)
    ex = [jnp.exp(l - m) for l in lg]
    den = functools.reduce(lambda a, b: a + b, ex)
    p = [e / den for e in ex]
    p_top, g_idx = p[0], jnp.zeros_like(p[0], dtype=jnp.int32)
    for g in range(1, N_EGROUPS):
        better = p[g] > p_top
        g_idx = jnp.where(better, g, g_idx)
        p_top = jnp.where(better, p[g], p_top)
    g_hot = [g_idx == g for g in range(N_EGROUPS)]
    le_sel = []
    for j in range(EXP_PER_GROUP):
        acc = None
        for g in range(N_EGROUPS):
            lane = N_EGROUPS + g * EXP_PER_GROUP + j
            t = jnp.where(g_hot[g], logits[:, lane:lane + 1], 0.0)
            acc = t if acc is None else acc + t
        le_sel.append(acc)
    m2 = functools.reduce(jnp.maximum, le_sel)
    ex2 = [jnp.exp(l - m2) for l in le_sel]
    den2 = functools.reduce(lambda a, b: a + b, ex2)
    s = [e / den2 for e in ex2]
    v1, i1 = s[0], jnp.zeros_like(g_idx)
    for j in range(1, EXP_PER_GROUP):
        better = s[j] > v1
        i1 = jnp.where(better, j, i1)
        v1 = jnp.where(better, s[j], v1)
    v2, i2 = jnp.full_like(v1, -1.0), jnp.zeros_like(g_idx)
    for j in range(EXP_PER_GROUP):
        s_rest = jnp.where(i1 == j, -1.0, s[j])
        better = s_rest > v2
        i2 = jnp.where(better, j, i2)
        v2 = jnp.where(better, s_rest, v2)
    tot = v1 + v2
    w1 = v1 / tot * p_top
    w2 = v2 / tot * p_top
    comb = []
    for g in range(N_EGROUPS):
        for j in range(EXP_PER_GROUP):
            inner = jnp.where(i1 == j, w1, jnp.where(i2 == j, w2, 0.0))
            comb.append(jnp.where(g_hot[g], inner, 0.0))
    return comb


def _post_kernel(x_ref, mab_ref, yc_ref, wout_ref, n2_ref, wr_ref, wg_ref, wu_ref, wd_ref, fn_ref,
                 o_ref, *, final):
    mixed = jnp.concatenate([mab_ref[...].astype(BF16), yc_ref[...].astype(BF16)], axis=1)
    x1 = x_ref[...] + jnp.dot(mixed, wout_ref[...], preferred_element_type=F32)
    hb = _rmsnorm(x1, n2_ref[...]).astype(BF16)
    comb = _route(jnp.dot(hb, wr_ref[...], preferred_element_type=F32))

    def activations(grp):
        acts = []
        for e in range(grp * EXPERT_BATCH, (grp + 1) * EXPERT_BATCH):
            hg = jnp.dot(hb, wg_ref[e], preferred_element_type=F32)
            hu = jnp.dot(hb, wu_ref[e], preferred_element_type=F32)
            acts.append((_silu(hg) * hu * comb[e]).astype(BF16))
        return jnp.concatenate(acts, axis=1)

    n_batches = N_EXPERTS // EXPERT_BATCH
    kb = EXPERT_BATCH * D_FF_E
    acts = [activations(0)]
    acc = x1
    for grp in range(n_batches):
        if grp + 1 < n_batches:
            acts.append(activations(grp + 1))
        acc = acc + jnp.dot(acts[grp], wd_ref[grp * kb:(grp + 1) * kb, :], preferred_element_type=F32)
    if final:
        acc = _rmsnorm(acc, fn_ref[...])
    o_ref[...] = acc


def _post_call(x, mixed_ab, y_c, layer, w_out, n2, w_router, w_gate, w_up, w_down, fn, *, tm, final, window=None):
    single = dict(pipeline_mode=pl.Buffered(1))
    const2 = lambda s: (0, 0)
    layer3 = lambda s: (layer, 0, 0)
    layer4 = lambda s: (layer, 0, 0, 0)
    if window is None:
        n_tiles = x.shape[0] // tm
        tok_specs = [pl.BlockSpec((tm, w), lambda s: (s, 0)) for w in (D_MODEL, D_AB, D_C)]
    else:
        B, L, start, length = window
        per_seq = length // tm
        n_tiles = B * per_seq
        window_row = lambda s: (pl.multiple_of((s // per_seq) * L + start + (s % per_seq) * tm, 8), 0)
        tok_specs = [pl.BlockSpec((pl.Element(tm), pl.Element(w)), window_row) for w in (D_MODEL, D_AB, D_C)]
    return pl.pallas_call(
        functools.partial(_post_kernel, final=final),
        grid=(n_tiles,),
        in_specs=tok_specs + [
            pl.BlockSpec((None, D_MIX, D_MODEL), layer3, **single),
            pl.BlockSpec((1, D_MODEL), const2),
            pl.BlockSpec((None, D_MODEL, D_SMALL), layer3, **single),
            pl.BlockSpec((None, N_EXPERTS, D_MODEL, D_FF_E), layer4, **single),
            pl.BlockSpec((None, N_EXPERTS, D_MODEL, D_FF_E), layer4, **single),
            pl.BlockSpec((None, N_EXPERTS * D_FF_E, D_MODEL), layer3, **single),
            pl.BlockSpec((1, D_MODEL), const2),
        ],
        out_specs=pl.BlockSpec((tm, D_MODEL), lambda s: (s, 0)),
        out_shape=jax.ShapeDtypeStruct((n_tiles * tm, D_MODEL), F32),
        compiler_params=pltpu.CompilerParams(
            dimension_semantics=("parallel",), vmem_limit_bytes=VMEM_LIMIT_BYTES),
        name="post",
    )(x, mixed_ab, y_c, w_out, n2, w_router, w_gate, w_up, w_down, fn)


def _pack_w_in_kernel(w_ref, o_ref):
    o = [0]
    for s in (D_QKV_A, D_A, H_A, H_A, D_XBC, D_B, H_B, D_C, D_C, D_C):
        o.append(o[-1] + s)
    pos = 0
    for lo, hi in ((o[0], o[2]), (o[4], o[6]), (o[7], o[10]), (o[2], o[4]), (o[6], o[7])):
        o_ref[:, pos:pos + hi - lo] = w_ref[:, lo:hi]
        pos += hi - lo
    o_ref[:, pos:] = jnp.zeros((o_ref.shape[0], D_PROJ_PACKED - pos), BF16)


def _pack_w_in(w):
    depth, _, d_proj = w.shape
    rows = 256
    return pl.pallas_call(
        _pack_w_in_kernel,
        grid=(depth, D_MODEL // rows),
        in_specs=[pl.BlockSpec((None, rows, d_proj), lambda l, i: (l, i, 0))],
        out_specs=pl.BlockSpec((None, rows, D_PROJ_PACKED), lambda l, i: (l, i, 0)),
        out_shape=jax.ShapeDtypeStruct((depth, D_MODEL, D_PROJ_PACKED), BF16),
        compiler_params=pltpu.CompilerParams(
            dimension_semantics=("parallel", "parallel"), vmem_limit_bytes=VMEM_LIMIT_BYTES),
        name="pack_w_in",
    )(w)


def _head_params(dt_bias_a, dt_bias_b, a_log_a, a_log_b, d_skip_b):
    z4 = jnp.zeros((H_A,), F32)
    tail = jnp.zeros((D_SMALL - 2 * H_A - H_B,), F32)
    rows = [
        jnp.concatenate([z4, dt_bias_a, dt_bias_b, tail]),
        jnp.concatenate([z4, a_log_a, a_log_b, tail]),
        jnp.concatenate([z4, z4, d_skip_b, tail]),
    ]
    return jnp.concatenate([jnp.stack(rows), jnp.zeros((8 - len(rows), D_SMALL), F32)], axis=0)


def _pad_hist(h):
    return jnp.pad(h, ((0, 0), (0, 0), (HIST_ROWS - h.shape[2], 0), (0, 0)))


def kernel(x_prompt, x_sample, state_gdn, state_gdn_conv, state_ssd, state_ssd_conv, state_sconv,
           meta_tokens, norm1_w, w_in, conv_a_w, a_log_a, dt_bias_a, onorm_a_w, conv_b_w, conv_b_b,
           a_log_b, dt_bias_b, d_skip_b, onorm_b_w, conv_c_w, w_out, norm2_w, w_router_g,
           w_router_e, w_gate, w_up, w_down, final_norm_w):
    bp, seq, _ = x_prompt.shape
    bs, dseq, _ = x_sample.shape
    depth = w_in.shape[0]
    pad = CHUNK - N_META
    lp = pad + N_META + seq
    ncp = lp // CHUNK

    meta = jnp.broadcast_to(meta_tokens[None], (bp, N_META, D_MODEL))
    x_p = jnp.concatenate([jnp.zeros((bp, pad, D_MODEL), F32), meta, x_prompt], axis=1)
    x_p = x_p.reshape(bp * lp, D_MODEL)
    x_s = x_sample.reshape(bs * dseq, D_MODEL)

    zero_sa = jnp.zeros((1, bp, H_A, DK_A, DV_A), F32)
    zero_sb = jnp.zeros((1, bp, H_B, P_B, N_B), F32)
    hists_s = [_pad_hist(state_gdn_conv), _pad_hist(state_ssd_conv), _pad_hist(state_sconv)]

    fn = final_norm_w.reshape(1, D_MODEL)
    w_packed = _pack_w_in(w_in.astype(BF16))
    w_router = jnp.concatenate(
        [w_router_g, w_router_e, jnp.zeros((depth, D_MODEL, D_SMALL - N_EGROUPS - N_EXPERTS), F32)],
        axis=-1).astype(BF16)
    w_out_b, w_gate_b, w_up_b = w_out.astype(BF16), w_gate.astype(BF16), w_up.astype(BF16)
    w_down_b = w_down.reshape(depth, N_EXPERTS * D_FF_E, D_MODEL).astype(BF16)
    hist_p = st_p = st_s = None
    for l in range(depth):
        conv_w = (conv_a_w[l], conv_b_w[l], conv_b_b[l].reshape(1, D_XBC), conv_c_w[l])
        mix_w = (conv_w, _head_params(dt_bias_a[l], dt_bias_b[l], a_log_a[l], a_log_b[l], d_skip_b[l]),
                 onorm_a_w[l].reshape(1, DV_A), onorm_b_w[l].reshape(1, D_B))
        post_w = (l, w_out_b, norm2_w[l].reshape(1, D_MODEL), w_router, w_gate_b, w_up_b, w_down_b, fn)
        final = l == depth - 1

        acts_p, hist_p = _proj_conv_call(x_p, norm1_w[l], w_packed, l, conv_w, hist_p, l, depth, B=bp, pad=pad)
        qkv_act, za, xbc_act, zb, yc_p, small = acts_p
        (mixed_p,), st_p = _mix_call((qkv_act, za, xbc_act, zb, small), None, zero_sa, zero_sb, 0, mix_w,
                                     st_p, l, depth, B=bp, nC=ncp, C=CHUNK, G=G_PROMPT, pad=pad, preact=True)
        window = (bp, lp, pad + N_META, seq) if final else None
        x_p = _post_call(x_p, mixed_p, yc_p, *post_w, tm=TM_POST, final=final, window=window)

        proj_s = _proj_call(x_s, norm1_w[l], w_packed, l, tm=TM_PROJ)
        (mixed_s, yc_s), st_s = _mix_call(proj_s, [h[l] for h in hists_s], state_gdn, state_ssd, l, mix_w,
                                          st_s, l, depth, B=bs, nC=1, C=dseq, G=G_SAMPLE, pad=0, preact=False)
        x_s = _post_call(x_s, mixed_s, yc_s, *post_w, tm=TM_POST, final=final)

    y_sample = x_s.reshape(bs, dseq, D_MODEL)
    return (x_p.reshape(bp, seq, D_MODEL), y_sample, st_p[0], hist_p[0], st_p[1], hist_p[1], hist_p[2],
            st_s[0], st_s[2], st_s[1], st_s[3], st_s[4])
```

```python
import functools

import jax
import jax.numpy as jnp
from jax import lax
from jax.experimental import pallas as pl
from jax.experimental.pallas import tpu as pltpu

F32 = jnp.float32
BF16 = jnp.bfloat16

D_MODEL = 1024
N_META = 16
H_A = 4
DK_A = 128
DV_A = 128
D_QK_A = H_A * DK_A
D_A = H_A * DV_A
D_QKV_A = 2 * D_QK_A + D_A
CONV_A = 4
H_B = 8
P_B = 64
D_B = H_B * P_B
G_B = 2
N_B = 128
D_XBC = D_B + 2 * G_B * N_B
CONV_B = 4
D_C = 512
CONV_C = 3
D_AB = D_A + D_B
D_MIX = D_AB + D_C
N_EGROUPS = 4
EXP_PER_GROUP = 4
N_EXPERTS = N_EGROUPS * EXP_PER_GROUP
D_FF_E = 256
EPS = 1e-6
CHUNK = 128

D_HCB = 3 * D_C
D_SMALL = 128
PROJ_WIDTHS = (D_QKV_A, D_A, D_XBC, D_B, D_HCB, D_SMALL)
D_PROJ_PACKED = sum(PROJ_WIDTHS)
PROJ_OFFSETS = tuple(sum(PROJ_WIDTHS[:i]) for i in range(len(PROJ_WIDTHS)))
ACT_WIDTHS = (D_QKV_A, D_A, D_XBC, D_B, D_C, D_SMALL)
HIST_WIDTHS = (D_QKV_A, D_XBC, D_C)
HIST_LENS = (CONV_A - 1, CONV_B - 1, CONV_C - 1)
LANE_BETA = 0
LANE_GA = H_A
LANE_GB = 2 * H_A
N_GATE_SLABS = 4
SUBLANES = 8
HIST_ROWS = SUBLANES
G_PROMPT = 2
G_SAMPLE = 16
EXPERT_BATCH = 4
TM_PROJ = 512
TM_POST = 512
PACK_ROWS = 256
TILES_PER_PROMPT = 4
CONV_COLS = 512
JOB_LOOKAHEAD = 1

VMEM_LIMIT_BYTES = 56 * 1024 * 1024


def _rmsnorm(x, w):
    return x * lax.rsqrt(jnp.mean(x * x, axis=-1, keepdims=True) + EPS) * w


def _silu(x):
    h = 0.5 * x
    return h * jnp.tanh(h) + h


def _softplus(x):
    return jnp.maximum(x, 0.0) + jnp.log1p(jnp.exp(-jnp.abs(x)))


def _dot(a, b):
    return jnp.dot(a.astype(BF16), b.astype(BF16), preferred_element_type=F32)


def _dot_nt(a, b):
    return lax.dot_general(a.astype(BF16), b.astype(BF16), (((1,), (1,)), ((), ())),
                           preferred_element_type=F32)


def _cumsum_rows(g, tril_b):
    g1 = g.astype(BF16)
    r1 = g - g1.astype(F32)
    g2 = r1.astype(BF16)
    g3 = (r1 - g2.astype(F32)).astype(BF16)
    d = functools.partial(jnp.dot, preferred_element_type=F32)
    return d(tril_b, g1) + d(tril_b, g2) + d(tril_b, g3)


def _unit_lower_inverse_offdiag(nmats, row, col, C):
    ds = [-jnp.where((row >> 1) == (col >> 1), n, 0.0) for n in nmats]
    s, ls = 2, 1
    while s < C:
        same_2s = (row >> (ls + 1)) == (col >> (ls + 1))
        same_s = (row >> ls) == (col >> ls)
        es = [jnp.where(same_2s, jnp.where(same_s, 0.0, n), 0.0) for n in nmats]
        ys = [e + _dot(d, e) for d, e in zip(ds, es)]
        ds = [d - (y + _dot(y, d)) for d, y in zip(ds, ys)]
        s, ls = 2 * s, ls + 1
    return ds


def _causal_conv(hist_ref, cur, w_ref, width, C, col0=0):
    n_cols = cur.shape[1]
    x3 = jnp.concatenate([hist_ref[...], cur], axis=0).reshape((C + HIST_ROWS) // SUBLANES, SUBLANES, n_cols)
    sub = lax.broadcasted_iota(jnp.int32, x3.shape, 1)

    def shift(a3, k):
        r = pltpu.roll(a3, k, axis=1)
        from_prev_tile = jnp.concatenate([r[:1], r[:-1]], axis=0)
        return jnp.where(sub < k, from_prev_tile, r)

    w = lambda i: w_ref[i:i + 1, col0:col0 + n_cols]
    if width == 4:
        x1 = shift(x3, 1)
        y3 = (x3 * w(3) + x1 * w(2)) + shift(x3 * w(1) + x1 * w(0), 2)
    else:
        assert width == 3
        y3 = x3 * w(2) + shift(x3, 1) * w(1) + shift(x3, 2) * w(0)
    hist_ref[...] = cur[C - HIST_ROWS:C]
    return y3.reshape(C + HIST_ROWS, n_cols)[HIST_ROWS:]


def _conv_front_end(raw_qkv, raw_xbc, hcb, exts, conv_w, keep, C):
    exta, extb, extc = exts
    cwa_ref, cwb_ref, cbb_ref, cwc_ref = conv_w
    qkv_act = _silu(_causal_conv(exta, keep(raw_qkv), cwa_ref, CONV_A, C))
    xbc_act = _silu(_causal_conv(extb, keep(raw_xbc), cwb_ref, CONV_B, C) + cbb_ref[...])
    u_c = keep(hcb[:, D_C:2 * D_C] * hcb[:, 0:D_C])
    y_c = hcb[:, 2 * D_C:3 * D_C] * _causal_conv(extc, u_c, cwc_ref, CONV_C, C)
    return qkv_act, xbc_act, y_c


def _proj_kernel(x_ref, nw_ref, w_ref, *out_refs):
    hn = _rmsnorm(x_ref[...], nw_ref[...]).astype(BF16)
    for o_ref, off, width in zip(out_refs, PROJ_OFFSETS, PROJ_WIDTHS):
        o_ref[...] = jnp.dot(hn, w_ref[:, off:off + width], preferred_element_type=F32)


def _proj_call(x, norm_w, w_packed, layer, tm):
    T = x.shape[0]
    const = lambda i: (0, 0)
    return pl.pallas_call(
        _proj_kernel,
        grid=(T // tm,),
        in_specs=[
            pl.BlockSpec((tm, D_MODEL), lambda i: (i, 0)),
            pl.BlockSpec((1, D_MODEL), const),
            pl.BlockSpec((None, D_MODEL, D_PROJ_PACKED), lambda i: (layer, 0, 0), pipeline_mode=pl.Buffered(1)),
        ],
        out_specs=[pl.BlockSpec((tm, w), lambda i: (i, 0)) for w in PROJ_WIDTHS],
        out_shape=[jax.ShapeDtypeStruct((T, w), F32) for w in PROJ_WIDTHS],
        compiler_params=pltpu.CompilerParams(
            dimension_semantics=("parallel",), vmem_limit_bytes=VMEM_LIMIT_BYTES),
        name="proj",
    )(x, norm_w.reshape(1, D_MODEL), w_packed)


def _proj_conv_kernel(x_ref, nw_ref, w_ref, cwa_ref, cwb_ref, cbb_ref, cwc_ref, *rest, tm, pad):
    n_ext = (D_QKV_A + D_XBC + D_C) // CONV_COLS
    (qkv_o, za_o, xbc_o, zb_o, yc_o, small_o, ha_o, hb_o, hc_o) = rest[-9 - n_ext:-n_ext]
    exts = rest[-n_ext:]
    j = lax.rem(pl.program_id(0), TILES_PER_PROMPT)

    @pl.when(j == 0)
    def _():
        for ext in exts:
            ext[...] = jnp.zeros((HIST_ROWS, CONV_COLS), F32)

    valid = j * tm + lax.broadcasted_iota(jnp.int32, (tm, 1), 0) >= pad
    hn = _rmsnorm(jnp.where(valid, x_ref[...], 0.0), nw_ref[...]).astype(BF16)
    mm = lambda off, width: jnp.dot(hn, w_ref[:, off:off + width], preferred_element_type=F32)
    off_qkv, off_za, off_xbc, off_zb, off_hcb, off_small = PROJ_OFFSETS

    def carry(ext, hist_ref, n, col0):
        hist_ref[:, col0:col0 + CONV_COLS] = ext[HIST_ROWS - n:HIST_ROWS, :]

    jobs = []
    for k in range(D_QKV_A // CONV_COLS):
        def tail(raw, k=k, ext=exts[k]):
            col0 = k * CONV_COLS
            qkv_o[:, col0:col0 + CONV_COLS] = _silu(_causal_conv(ext, raw, cwa_ref, CONV_A, tm, col0))
            carry(ext, ha_o, CONV_A - 1, col0)
        jobs.append((functools.partial(mm, off_qkv + k * CONV_COLS, CONV_COLS), tail))
    for k in range(D_XBC // CONV_COLS):
        def tail(raw, k=k, ext=exts[D_QKV_A // CONV_COLS + k]):
            col0 = k * CONV_COLS
            conv = _causal_conv(ext, raw, cwb_ref, CONV_B, tm, col0)
            xbc_o[:, col0:col0 + CONV_COLS] = _silu(conv + cbb_ref[:, col0:col0 + CONV_COLS])
            carry(ext, hb_o, CONV_B - 1, col0)
        jobs.append((functools.partial(mm, off_xbc + k * CONV_COLS, CONV_COLS), tail))

    def tail_c(hcb):
        u_c = hcb[:, D_C:2 * D_C] * hcb[:, 0:D_C]
        for k in range(D_C // CONV_COLS):
            col0, ext = k * CONV_COLS, exts[(D_QKV_A + D_XBC) // CONV_COLS + k]
            conv = _causal_conv(ext, u_c[:, col0:col0 + CONV_COLS], cwc_ref, CONV_C, tm, col0)
            yc_o[:, col0:col0 + CONV_COLS] = hcb[:, 2 * D_C + col0:2 * D_C + col0 + CONV_COLS] * conv
            carry(ext, hc_o, CONV_C - 1, col0)
    jobs.append((functools.partial(mm, off_hcb, D_HCB), tail_c))

    def tail_plain(raws):
        za_o[...], zb_o[...], small_o[...] = raws
    jobs.append((lambda: (mm(off_za, D_A), mm(off_zb, D_B), mm(off_small, D_SMALL)), tail_plain))

    pending = []
    for matmuls, tail in jobs:
        pending.append((matmuls(), tail))
        if len(pending) > JOB_LOOKAHEAD:
            raw, ready_tail = pending.pop(0)
            ready_tail(raw)
    for raw, ready_tail in pending:
        ready_tail(raw)


def _proj_conv_call(x, norm_w, w_packed, layer, conv_w, stacked, out_layer, depth, *, B, pad):
    T = x.shape[0]
    tm = T // (B * TILES_PER_PROMPT)
    const = lambda i: (0, 0)
    row = lambda i: (i, 0)
    hist = lambda i: (out_layer, i // TILES_PER_PROMPT, 0, 0)
    cwa, cwb, cbb, cwc = conv_w
    in_specs = [
        pl.BlockSpec((tm, D_MODEL), row),
        pl.BlockSpec((1, D_MODEL), const),
        pl.BlockSpec((None, D_MODEL, D_PROJ_PACKED), lambda i: (layer, 0, 0), pipeline_mode=pl.Buffered(1)),
        pl.BlockSpec((CONV_A, D_QKV_A), const),
        pl.BlockSpec((CONV_B, D_XBC), const),
        pl.BlockSpec((1, D_XBC), const),
        pl.BlockSpec((CONV_C, D_C), const),
    ]
    stacked = list(stacked or [])
    n_in = len(in_specs)
    in_specs += [pl.BlockSpec(memory_space=pl.ANY) for _ in stacked]
    n_act = len(ACT_WIDTHS)
    outs = pl.pallas_call(
        functools.partial(_proj_conv_kernel, tm=tm, pad=pad),
        grid=(T // tm,),
        in_specs=in_specs,
        out_specs=([pl.BlockSpec((tm, w), row) for w in ACT_WIDTHS]
                   + [pl.BlockSpec((None, None, n, w), hist) for n, w in zip(HIST_LENS, HIST_WIDTHS)]),
        out_shape=([jax.ShapeDtypeStruct((T, w), F32) for w in ACT_WIDTHS]
                   + [jax.ShapeDtypeStruct((depth, B, n, w), F32) for n, w in zip(HIST_LENS, HIST_WIDTHS)]),
        scratch_shapes=[pltpu.VMEM((HIST_ROWS, CONV_COLS), F32)
                        for _ in range(sum(HIST_WIDTHS) // CONV_COLS)],
        input_output_aliases={n_in + i: n_act + i for i in range(len(stacked))},
        compiler_params=pltpu.CompilerParams(
            dimension_semantics=("arbitrary",), vmem_limit_bytes=VMEM_LIMIT_BYTES),
        name="proj_conv",
    )(x, norm_w.reshape(1, D_MODEL), w_packed, cwa, cwb, cbb, cwc, *stacked)
    return outs[:n_act], outs[n_act:]


def _mix_kernel(*refs, G, C, pad, nC, preact):
    if preact:
        (qkv_ref, za_ref, xbc_ref, zb_ref, small_ref, small_next_ref,
         s0a_ref, s0b_ref, hp_ref, ona_ref, onb_ref) = refs[:11]
        mixed_ref, sa_out, sb_out, sa, sb, gate_s = refs[-6:]
    else:
        (qkv_ref, za_ref, xbc_ref, zb_ref, hcb_ref, small_ref, hista_ref, histb_ref, histc_ref,
         s0a_ref, s0b_ref, cwa_ref, cwb_ref, cbb_ref, cwc_ref, hp_ref, ona_ref, onb_ref) = refs[:18]
        (mixed_ref, yc_ref, sa_out, sb_out, ha_out, hb_out, hc_out, exta, extb, extc, sa, sb) = refs[-12:]
        exts, hist_refs, hist_outs = (exta, extb, extc), (hista_ref, histb_ref, histc_ref), (ha_out, hb_out, hc_out)
    c = pl.program_id(1)

    @pl.when(c == 0)
    def _():
        sa[...] = s0a_ref[...]
        sb[...] = s0b_ref[...]
        if not preact:
            for ext, h_ref in zip(exts, hist_refs):
                ext[...] = h_ref[...]

    row = lax.broadcasted_iota(jnp.int32, (C, C), 0)
    col = lax.broadcasted_iota(jnp.int32, (C, C), 1)
    tril = row >= col
    strict = row > col
    tril_b = jnp.where(tril, 1.0, 0.0).astype(BF16)

    if pad:
        tok = c * C + lax.broadcasted_iota(jnp.int32, (C, 1), 0)
        valid = tok >= pad
        keep = lambda v: jnp.where(valid, v, 0.0)
    else:
        keep = lambda v: v

    seqs = range(G)
    lane_col = lambda a, lane: a[:, lane:lane + 1]

    def gates(small, keep_rows):
        sp_ = keep_rows(_softplus(small + hp_ref[0:1, :]))
        G_ = _cumsum_rows(-jnp.exp(hp_ref[1:2, :]) * sp_, tril_b)
        return sp_, jax.nn.sigmoid(small), G_, G_.T

    if preact:
        @pl.when(c == 0)
        def _():
            for g in seqs:
                for i, a in enumerate(gates(small_ref[g], keep)):
                    gate_s[g, i] = a
        gate_vals = [[gate_s[g, i] for i in range(N_GATE_SLABS)] for g in seqs]
    else:
        gate_vals = [gates(small_ref[g], keep) for g in seqs]
    sp, beta_all, G_all, GT_all = (list(v) for v in zip(*gate_vals))
    eG_last = [jnp.exp(x[C - 1:C, :]) for x in G_all]

    def head_decays(g, lane):
        gb = jnp.broadcast_to(lane_col(G_all[g], lane), (C, D_SMALL))
        dlog = gb[:, :C] - GT_all[g][lane:lane + 1, :]
        dec = jnp.where(tril, jnp.exp(jnp.minimum(dlog, 0.0)), 0.0)
        return jnp.exp(gb), jnp.exp(G_all[g][C - 1:C, lane:lane + 1] - gb), dec

    if preact:
        qkv = [qkv_ref[g] for g in seqs]
        xbc = [xbc_ref[g] for g in seqs]
    else:
        qkv, xbc = [], []
        for g in seqs:
            qkv_act, xbc_act, y_c = _conv_front_end(
                qkv_ref[g], xbc_ref[g], hcb_ref[g], [ext.at[g] for ext in exts],
                (cwa_ref, cwb_ref, cbb_ref, cwc_ref), keep, C)
            qkv.append(qkv_act)
            xbc.append(xbc_act)
            yc_ref[g] = y_c

    chains = [(g, h) for g in seqs for h in range(H_A)]
    qs, ks, vs, kbs, egs, kdts, decs = [], [], [], [], [], [], []
    for g, h in chains:
        q = qkv[g][:, h * DK_A:(h + 1) * DK_A]
        k = qkv[g][:, D_QK_A + h * DK_A:D_QK_A + (h + 1) * DK_A]
        v = qkv[g][:, 2 * D_QK_A + h * DV_A:2 * D_QK_A + (h + 1) * DV_A]
        q = q * lax.rsqrt(jnp.sum(q * q, axis=-1, keepdims=True) + EPS) * (DK_A ** -0.5)
        k = k * lax.rsqrt(jnp.sum(k * k, axis=-1, keepdims=True) + EPS)
        beta = lane_col(beta_all[g], LANE_BETA + h)
        qs.append(q)
        ks.append(k)
        vs.append(v * beta)
        kbs.append(k * beta)
        eg, erem, dec = head_decays(g, LANE_GA + h)
        egs.append(eg)
        decs.append(dec)
        kdts.append((k * erem).T)
    kqs = [_dot_nt(jnp.concatenate([kb, q], axis=0), k) for kb, q, k in zip(kbs, qs, ks)]
    nmats = [jnp.where(strict, kq[:C] * dec, 0.0) for kq, dec in zip(kqs, decs)]
    qks = [kq[C:] * dec for kq, dec in zip(kqs, decs)]
    heads_per_group = H_B // G_B
    gw = D_B // G_B
    grp_of = lambda h: h // heads_per_group
    bgs = {(g, grp): xbc[g][:, D_B + grp * N_B:D_B + (grp + 1) * N_B] for g in seqs for grp in range(G_B)}
    cgs = {(g, grp): xbc[g][:, D_B + (G_B + grp) * N_B:D_B + (G_B + grp + 1) * N_B]
           for g in seqs for grp in range(G_B)}
    heads = [(g, h) for g in seqs for h in range(H_B)]
    x_hs = [xbc[g][:, h * P_B:(h + 1) * P_B] for g, h in heads]
    xdts = [x_h * lane_col(sp[g], LANE_GB + h) for (g, h), x_h in zip(heads, x_hs)]
    xdtts = [xdt.T for xdt in xdts]
    cdecs, bdecs, decs_b = [], [], []
    for g, h in heads:
        eg, erem, dec = head_decays(g, LANE_GB + h)
        cdecs.append(cgs[(g, grp_of(h))] * eg)
        bdecs.append(bgs[(g, grp_of(h))] * erem)
        decs_b.append(dec)
    tinvs = _unit_lower_inverse_offdiag(nmats, row, col, C)
    rhss = [jnp.concatenate([vb, kb * eg], axis=1) for vb, kb, eg in zip(vs, kbs, egs)]
    sols = [rhs + _dot(t, rhs) for t, rhs in zip(tinvs, rhss)]
    if preact:
        for g in seqs:
            for i, a in enumerate(gates(small_next_ref[g], lambda v: v)):
                gate_s[g, i] = a
    s_olds = [sa[g, h] for g, h in chains]
    wss = [_dot(jnp.concatenate([sol[:, DV_A:], q * eg], axis=0), s_old)
           for sol, q, eg, s_old in zip(sols, qs, egs, s_olds)]
    ws_ = [sol[:, :DV_A] - ws[:C] for sol, ws in zip(sols, wss)]
    os_ = [ws[C:] + _dot(qk, w) for ws, qk, w in zip(wss, qks, ws_)]
    for (g, h), kdt, w, s_old, o in zip(chains, kdts, ws_, s_olds, os_):
        sa[g, h] = lane_col(eG_last[g], LANE_GA + h) * s_old + _dot(kdt, w)
        gate = _silu(za_ref[g, :, h * DV_A:(h + 1) * DV_A])
        mixed_ref[g, :, h * DV_A:(h + 1) * DV_A] = _rmsnorm(o, ona_ref[...]) * gate

    cbts = {key: _dot_nt(cgs[key], bgs[key]) for key in bgs}
    h_olds = [sb[g, h] for g, h in heads]
    y_offs = [_dot_nt(cdec, h_old) for cdec, h_old in zip(cdecs, h_olds)]
    for (g, h), xdtt, bdec, h_old in zip(heads, xdtts, bdecs, h_olds):
        sb[g, h] = lane_col(eG_last[g], LANE_GB + h) * h_old + _dot(xdtt, bdec)
    ys = [y_off + _dot(cbts[(g, grp_of(h))] * dec, xdt) + hp_ref[2:3, LANE_GB + h:LANE_GB + h + 1] * x_h
          for (g, h), y_off, dec, xdt, x_h in zip(heads, y_offs, decs_b, xdts, x_hs)]
    for g in seqs:
        for grp in range(G_B):
            lo = g * H_B + grp * heads_per_group
            yg = jnp.concatenate(ys[lo:lo + heads_per_group], axis=1)
            yg = yg * _silu(zb_ref[g, :, grp * gw:(grp + 1) * gw])
            mixed_ref[g, :, D_A + grp * gw:D_A + (grp + 1) * gw] = _rmsnorm(
                yg, onb_ref[:, grp * gw:(grp + 1) * gw])

    @pl.when(c == nC - 1)
    def _():
        sa_out[...] = sa[...]
        sb_out[...] = sb[...]
        if not preact:
            for o_ref, ext, n in zip(hist_outs, exts, HIST_LENS):
                o_ref[...] = ext[:, HIST_ROWS - n:HIST_ROWS, :]


def _mix_call(acts, hists, s0a, s0b, layer, weights, stacked, out_layer, depth, *, B, nC, C, G, pad, preact):
    L = nC * C
    conv_w, hp, ona, onb = weights
    tok = lambda b, c: (b, c, 0)
    per_b3 = lambda b, c: (b, 0, 0)
    out_b4 = lambda b, c: (out_layer, b, 0, 0)
    out_b5 = lambda b, c: (out_layer, b, 0, 0, 0)
    const = lambda b, c: (0, 0)
    state_specs = [
        pl.BlockSpec((None, G, H_A, DK_A, DV_A), lambda b, c: (layer, b, 0, 0, 0)),
        pl.BlockSpec((None, G, H_B, P_B, N_B), lambda b, c: (layer, b, 0, 0, 0)),
    ]
    head_specs = [pl.BlockSpec((8, D_SMALL), const), pl.BlockSpec((1, DV_A), const), pl.BlockSpec((1, D_B), const)]
    acts3 = [a.reshape(B, L, a.shape[-1]) for a in acts]
    in_specs = [pl.BlockSpec((G, C, a.shape[-1]), tok) for a in acts3]
    operands = list(acts3)
    if preact:
        assert C == D_SMALL, "the carried gate slabs assume (C, 128) == (128, C)"
        in_specs.append(pl.BlockSpec((G, C, D_SMALL), lambda b, c: (b, jnp.minimum(c + 1, nC - 1), 0)))
        operands.append(acts3[-1])
    else:
        in_specs += [pl.BlockSpec((G, HIST_ROWS, w), per_b3) for w in HIST_WIDTHS]
        operands += list(hists)
    in_specs += state_specs
    operands += [s0a, s0b]
    if not preact:
        in_specs += [pl.BlockSpec((CONV_A, D_QKV_A), const), pl.BlockSpec((CONV_B, D_XBC), const),
                     pl.BlockSpec((1, D_XBC), const), pl.BlockSpec((CONV_C, D_C), const)]
        operands += list(conv_w)
    in_specs += head_specs
    operands += [hp, ona, onb]

    out_specs = [pl.BlockSpec((G, C, D_AB), tok)]
    out_shape = [jax.ShapeDtypeStruct((B, L, D_AB), F32)]
    if not preact:
        out_specs.append(pl.BlockSpec((G, C, D_C), tok))
        out_shape.append(jax.ShapeDtypeStruct((B, L, D_C), F32))
    n_plain = len(out_specs)
    out_specs += [pl.BlockSpec((None, G, H_A, DK_A, DV_A), out_b5), pl.BlockSpec((None, G, H_B, P_B, N_B), out_b5)]
    out_shape += [jax.ShapeDtypeStruct((depth, B, H_A, DK_A, DV_A), F32),
                  jax.ShapeDtypeStruct((depth, B, H_B, P_B, N_B), F32)]
    scratch = []
    if not preact:
        out_specs += [pl.BlockSpec((None, G, n, w), out_b4) for n, w in zip(HIST_LENS, HIST_WIDTHS)]
        out_shape += [jax.ShapeDtypeStruct((depth, B, n, w), F32) for n, w in zip(HIST_LENS, HIST_WIDTHS)]
        scratch += [pltpu.VMEM((G, HIST_ROWS, w), F32) for w in HIST_WIDTHS]
    scratch += [pltpu.VMEM((G, H_A, DK_A, DV_A), F32), pltpu.VMEM((G, H_B, P_B, N_B), F32)]
    if preact:
        scratch.append(pltpu.VMEM((G, N_GATE_SLABS, C, D_SMALL), F32))

    stacked = list(stacked or [])
    n_in = len(in_specs)
    in_specs += [pl.BlockSpec(memory_space=pl.ANY) for _ in stacked]
    outs = pl.pallas_call(
        functools.partial(_mix_kernel, G=G, C=C, pad=pad, nC=nC, preact=preact),
        grid=(B // G, nC),
        in_specs=in_specs,
        out_specs=out_specs,
        out_shape=out_shape,
        scratch_shapes=scratch,
        input_output_aliases={n_in + i: n_plain + i for i in range(len(stacked))},
        compiler_params=pltpu.CompilerParams(
            dimension_semantics=("parallel", "arbitrary"), vmem_limit_bytes=VMEM_LIMIT_BYTES),
        name="mix",
    )(*operands, *stacked)
    plain = [o.reshape(B * L, o.shape[-1]) for o in outs[:n_plain]]
    return plain, list(outs[n_plain:])


def _route(logits):
    lg = [logits[:, g:g + 1] for g in range(N_EGROUPS)]
    m = functools.reduce(jnp.maximum, lg)
    ex = [jnp.exp(l - m) for l in lg]
    den = functools.reduce(lambda a, b: a + b, ex)
    p = [e / den for e in ex]
    p_top, g_idx = p[0], jnp.zeros_like(p[0], dtype=jnp.int32)
    for g in range(1, N_EGROUPS):
        better = p[g] > p_top
        g_idx = jnp.where(better, g, g_idx)
        p_top = jnp.where(better, p[g], p_top)
    g_hot = [g_idx == g for g in range(N_EGROUPS)]
    le_sel = []
    for j in range(EXP_PER_GROUP):
        acc = None
        for g in range(N_EGROUPS):
            lane = N_EGROUPS + g * EXP_PER_GROUP + j
            t = jnp.where(g_hot[g], logits[:, lane:lane + 1], 0.0)
            acc = t if acc is None else acc + t
        le_sel.append(acc)
    m2 = functools.reduce(jnp.maximum, le_sel)
    ex2 = [jnp.exp(l - m2) for l in le_sel]
    den2 = functools.reduce(lambda a, b: a + b, ex2)
    s = [e / den2 for e in ex2]
    v1, i1 = s[0], jnp.zeros_like(g_idx)
    for j in range(1, EXP_PER_GROUP):
        better = s[j] > v1
        i1 = jnp.where(better, j, i1)
        v1 = jnp.where(better, s[j], v1)
    v2, i2 = jnp.full_like(v1, -1.0), jnp.zeros_like(g_idx)
    for j in range(EXP_PER_GROUP):
        s_rest = jnp.where(i1 == j, -1.0, s[j])
        better = s_rest > v2
        i2 = jnp.where(better, j, i2)
        v2 = jnp.where(better, s_rest, v2)
    tot = v1 + v2
    w1 = v1 / tot * p_top
    w2 = v2 / tot * p_top
    comb = []
    for g in range(N_EGROUPS):
        for j in range(EXP_PER_GROUP):
            inner = jnp.where(i1 == j, w1, jnp.where(i2 == j, w2, 0.0))
            comb.append(jnp.where(g_hot[g], inner, 0.0))
    return comb


def _post_kernel(x_ref, mab_ref, yc_ref, wout_ref, n2_ref, wr_ref, wg_ref, wu_ref, wd_ref, fn_ref,
                 o_ref, *, final):
    mixed = jnp.concatenate([mab_ref[...].astype(BF16), yc_ref[...].astype(BF16)], axis=1)
    x1 = x_ref[...] + jnp.dot(mixed, wout_ref[...], preferred_element_type=F32)
    hb = _rmsnorm(x1, n2_ref[...]).astype(BF16)
    comb = _route(jnp.dot(hb, wr_ref[...], preferred_element_type=F32))

    def activations(grp):
        acts = []
        for e in range(grp * EXPERT_BATCH, (grp + 1) * EXPERT_BATCH):
            hg = jnp.dot(hb, wg_ref[e], preferred_element_type=F32)
            hu = jnp.dot(hb, wu_ref[e], preferred_element_type=F32)
            acts.append((_silu(hg) * hu * comb[e]).astype(BF16))
        return jnp.concatenate(acts, axis=1)

    n_batches = N_EXPERTS // EXPERT_BATCH
    kb = EXPERT_BATCH * D_FF_E
    acts = [activations(0)]
    acc = x1
    for grp in range(n_batches):
        if grp + 1 < n_batches:
            acts.append(activations(grp + 1))
        acc = acc + jnp.dot(acts[grp], wd_ref[grp * kb:(grp + 1) * kb, :], preferred_element_type=F32)
    if final:
        acc = _rmsnorm(acc, fn_ref[...])
    o_ref[...] = acc


def _post_call(x, mixed_ab, y_c, layer, w_out, n2, w_router, w_gate, w_up, w_down, fn, *, tm, final, window=None):
    single = dict(pipeline_mode=pl.Buffered(1))
    const2 = lambda s: (0, 0)
    layer3 = lambda s: (layer, 0, 0)
    layer4 = lambda s: (layer, 0, 0, 0)
    if window is None:
        n_tiles = x.shape[0] // tm
        tok_specs = [pl.BlockSpec((tm, w), lambda s: (s, 0)) for w in (D_MODEL, D_AB, D_C)]
    else:
        B, L, start, length = window
        per_seq = length // tm
        n_tiles = B * per_seq
        window_row = lambda s: (pl.multiple_of((s // per_seq) * L + start + (s % per_seq) * tm, 8), 0)
        tok_specs = [pl.BlockSpec((pl.Element(tm), pl.Element(w)), window_row) for w in (D_MODEL, D_AB, D_C)]
    return pl.pallas_call(
        functools.partial(_post_kernel, final=final),
        grid=(n_tiles,),
        in_specs=tok_specs + [
            pl.BlockSpec((None, D_MIX, D_MODEL), layer3, **single),
            pl.BlockSpec((1, D_MODEL), const2),
            pl.BlockSpec((None, D_MODEL, D_SMALL), layer3, **single),
            pl.BlockSpec((None, N_EXPERTS, D_MODEL, D_FF_E), layer4, **single),
            pl.BlockSpec((None, N_EXPERTS, D_MODEL, D_FF_E), layer4, **single),
            pl.BlockSpec((None, N_EXPERTS * D_FF_E, D_MODEL), layer3, **single),
            pl.BlockSpec((1, D_MODEL), const2),
        ],
        out_specs=pl.BlockSpec((tm, D_MODEL), lambda s: (s, 0)),
        out_shape=jax.ShapeDtypeStruct((n_tiles * tm, D_MODEL), F32),
        compiler_params=pltpu.CompilerParams(
            dimension_semantics=("parallel",), vmem_limit_bytes=VMEM_LIMIT_BYTES),
        name="post",
    )(x, mixed_ab, y_c, w_out, n2, w_router, w_gate, w_up, w_down, fn)


def _pack_w_in_kernel(w_ref, o_ref):
    o = [0]
    for s in (D_QKV_A, D_A, H_A, H_A, D_XBC, D_B, H_B, D_C, D_C, D_C):
        o.append(o[-1] + s)
    pos = 0
    for lo, hi in ((o[0], o[2]), (o[4], o[6]), (o[7], o[10]), (o[2], o[4]), (o[6], o[7])):
        o_ref[:, pos:pos + hi - lo] = w_ref[:, lo:hi]
        pos += hi - lo
    o_ref[:, pos:] = jnp.zeros((o_ref.shape[0], D_PROJ_PACKED - pos), BF16)


def _pack_w_in(w):
    depth, _, d_proj = w.shape
    rows = PACK_ROWS
    return pl.pallas_call(
        _pack_w_in_kernel,
        grid=(depth, D_MODEL // rows),
        in_specs=[pl.BlockSpec((None, rows, d_proj), lambda l, i: (l, i, 0))],
        out_specs=pl.BlockSpec((None, rows, D_PROJ_PACKED), lambda l, i: (l, i, 0)),
        out_shape=jax.ShapeDtypeStruct((depth, D_MODEL, D_PROJ_PACKED), BF16),
        compiler_params=pltpu.CompilerParams(
            dimension_semantics=("parallel", "parallel"), vmem_limit_bytes=VMEM_LIMIT_BYTES),
        name="pack_w_in",
    )(w)


def _head_params(dt_bias_a, dt_bias_b, a_log_a, a_log_b, d_skip_b):
    z4 = jnp.zeros((H_A,), F32)
    tail = jnp.zeros((D_SMALL - 2 * H_A - H_B,), F32)
    rows = [
        jnp.concatenate([z4, dt_bias_a, dt_bias_b, tail]),
        jnp.concatenate([z4, a_log_a, a_log_b, tail]),
        jnp.concatenate([z4, z4, d_skip_b, tail]),
    ]
    return jnp.concatenate([jnp.stack(rows), jnp.zeros((8 - len(rows), D_SMALL), F32)], axis=0)


def _pad_hist(h):
    return jnp.pad(h, ((0, 0), (0, 0), (HIST_ROWS - h.shape[2], 0), (0, 0)))


def kernel(x_prompt, x_sample, state_gdn, state_gdn_conv, state_ssd, state_ssd_conv, state_sconv,
           meta_tokens, norm1_w, w_in, conv_a_w, a_log_a, dt_bias_a, onorm_a_w, conv_b_w, conv_b_b,
           a_log_b, dt_bias_b, d_skip_b, onorm_b_w, conv_c_w, w_out, norm2_w, w_router_g,
           w_router_e, w_gate, w_up, w_down, final_norm_w):
    bp, seq, _ = x_prompt.shape
    bs, dseq, _ = x_sample.shape
    depth = w_in.shape[0]
    pad = CHUNK - N_META
    lp = pad + N_META + seq
    ncp = lp // CHUNK

    meta = jnp.broadcast_to(meta_tokens[None], (bp, N_META, D_MODEL))
    x_p = jnp.concatenate([jnp.zeros((bp, pad, D_MODEL), F32), meta, x_prompt], axis=1)
    x_p = x_p.reshape(bp * lp, D_MODEL)
    x_s = x_sample.reshape(bs * dseq, D_MODEL)

    zero_sa = jnp.zeros((1, bp, H_A, DK_A, DV_A), F32)
    zero_sb = jnp.zeros((1, bp, H_B, P_B, N_B), F32)
    hists_s = [_pad_hist(state_gdn_conv), _pad_hist(state_ssd_conv), _pad_hist(state_sconv)]

    fn = final_norm_w.reshape(1, D_MODEL)
    w_packed = _pack_w_in(w_in.astype(BF16))
    w_router = jnp.concatenate(
        [w_router_g, w_router_e, jnp.zeros((depth, D_MODEL, D_SMALL - N_EGROUPS - N_EXPERTS), F32)],
        axis=-1).astype(BF16)
    w_out_b, w_gate_b, w_up_b = w_out.astype(BF16), w_gate.astype(BF16), w_up.astype(BF16)
    w_down_b = w_down.reshape(depth, N_EXPERTS * D_FF_E, D_MODEL).astype(BF16)
    hist_p = st_p = st_s = None
    for l in range(depth):
        conv_w = (conv_a_w[l], conv_b_w[l], conv_b_b[l].reshape(1, D_XBC), conv_c_w[l])
        mix_w = (conv_w, _head_params(dt_bias_a[l], dt_bias_b[l], a_log_a[l], a_log_b[l], d_skip_b[l]),
                 onorm_a_w[l].reshape(1, DV_A), onorm_b_w[l].reshape(1, D_B))
        post_w = (l, w_out_b, norm2_w[l].reshape(1, D_MODEL), w_router, w_gate_b, w_up_b, w_down_b, fn)
        final = l == depth - 1

        acts_p, hist_p = _proj_conv_call(x_p, norm1_w[l], w_packed, l, conv_w, hist_p, l, depth, B=bp, pad=pad)
        qkv_act, za, xbc_act, zb, yc_p, small = acts_p
        (mixed_p,), st_p = _mix_call((qkv_act, za, xbc_act, zb, small), None, zero_sa, zero_sb, 0, mix_w,
                                     st_p, l, depth, B=bp, nC=ncp, C=CHUNK, G=G_PROMPT, pad=pad, preact=True)
        window = (bp, lp, pad + N_META, seq) if final else None
        x_p = _post_call(x_p, mixed_p, yc_p, *post_w, tm=TM_POST, final=final, window=window)

        proj_s = _proj_call(x_s, norm1_w[l], w_packed, l, tm=TM_PROJ)
        (mixed_s, yc_s), st_s = _mix_call(proj_s, [h[l] for h in hists_s], state_gdn, state_ssd, l, mix_w,
                                          st_s, l, depth, B=bs, nC=1, C=dseq, G=G_SAMPLE, pad=0, preact=False)
        x_s = _post_call(x_s, mixed_s, yc_s, *post_w, tm=TM_POST, final=final)

    y_sample = x_s.reshape(bs, dseq, D_MODEL)
    return (x_p.reshape(bp, seq, D_MODEL), y_sample, st_p[0], hist_p[0], st_p[1], hist_p[1], hist_p[2],
            st_s[0], st_s[2], st_s[1], st_s[3], st_s[4])
```

```python
import functools

import jax
import jax.numpy as jnp
from jax import lax
from jax.experimental import pallas as pl
from jax.experimental.pallas import tpu as pltpu

F32 = jnp.float32
BF16 = jnp.bfloat16

D_MODEL = 1024
N_META = 16
H_A = 4
DK_A = 128
DV_A = 128
D_QK_A = H_A * DK_A
D_A = H_A * DV_A
D_QKV_A = 2 * D_QK_A + D_A
CONV_A = 4
H_B = 8
P_B = 64
D_B = H_B * P_B
G_B = 2
N_B = 128
D_XBC = D_B + 2 * G_B * N_B
CONV_B = 4
D_C = 512
CONV_C = 3
D_AB = D_A + D_B
D_MIX = D_AB + D_C
N_EGROUPS = 4
EXP_PER_GROUP = 4
N_EXPERTS = N_EGROUPS * EXP_PER_GROUP
D_FF_E = 256
EPS = 1e-6
CHUNK = 128

D_HCB = 3 * D_C
D_SMALL = 128
PROJ_WIDTHS = (D_QKV_A, D_A, D_XBC, D_B, D_HCB, D_SMALL)
D_PROJ_PACKED = sum(PROJ_WIDTHS)
PROJ_OFFSETS = tuple(sum(PROJ_WIDTHS[:i]) for i in range(len(PROJ_WIDTHS)))
ACT_WIDTHS = (D_QKV_A, D_A, D_XBC, D_B, D_C, D_SMALL)
HIST_WIDTHS = (D_QKV_A, D_XBC, D_C)
HIST_LENS = (CONV_A - 1, CONV_B - 1, CONV_C - 1)
LANE_BETA = 0
LANE_GA = H_A
LANE_GB = 2 * H_A
N_GATE_SLABS = 4
SUBLANES = 8
HIST_ROWS = SUBLANES
G_PROMPT = 2
G_SAMPLE = 16
EXPERT_BATCH = 4
TM_PROJ = 512
TM_POST = 512
PACK_ROWS = 256
TILES_PER_PROMPT = 4
CONV_COLS = 512
JOB_LOOKAHEAD = 1

VMEM_LIMIT_BYTES = 56 * 1024 * 1024


def _rmsnorm(x, w):
    return x * lax.rsqrt(jnp.mean(x * x, axis=-1, keepdims=True) + EPS) * w


def _silu(x):
    h = 0.5 * x
    return h * jnp.tanh(h) + h


def _softplus(x):
    return jnp.maximum(x, 0.0) + jnp.log1p(jnp.exp(-jnp.abs(x)))


def _dot(a, b):
    return jnp.dot(a.astype(BF16), b.astype(BF16), preferred_element_type=F32)


def _dot_nt(a, b):
    return lax.dot_general(a.astype(BF16), b.astype(BF16), (((1,), (1,)), ((), ())),
                           preferred_element_type=F32)


def _cumsum_rows(g, tril_b):
    g1 = g.astype(BF16)
    r1 = g - g1.astype(F32)
    g2 = r1.astype(BF16)
    g3 = (r1 - g2.astype(F32)).astype(BF16)
    d = functools.partial(jnp.dot, preferred_element_type=F32)
    return d(tril_b, g1) + d(tril_b, g2) + d(tril_b, g3)


def _unit_lower_inverse_offdiag(nmats, row, col, C):
    ds = [-jnp.where((row >> 1) == (col >> 1), n, 0.0) for n in nmats]
    s, ls = 2, 1
    while s < C:
        same_2s = (row >> (ls + 1)) == (col >> (ls + 1))
        same_s = (row >> ls) == (col >> ls)
        es = [jnp.where(same_2s, jnp.where(same_s, 0.0, n), 0.0) for n in nmats]
        ys = [e + _dot(d, e) for d, e in zip(ds, es)]
        ds = [d - (y + _dot(y, d)) for d, y in zip(ds, ys)]
        s, ls = 2 * s, ls + 1
    return ds


def _causal_conv(hist_ref, cur, w_ref, width, C, col0=0):
    n_cols = cur.shape[1]
    x3 = jnp.concatenate([hist_ref[...], cur], axis=0).reshape((C + HIST_ROWS) // SUBLANES, SUBLANES, n_cols)
    sub = lax.broadcasted_iota(jnp.int32, x3.shape, 1)

    def shift(a3, k):
        r = pltpu.roll(a3, k, axis=1)
        from_prev_tile = jnp.concatenate([r[:1], r[:-1]], axis=0)
        return jnp.where(sub < k, from_prev_tile, r)

    w = lambda i: w_ref[i:i + 1, col0:col0 + n_cols]
    if width == 4:
        x1 = shift(x3, 1)
        y3 = (x3 * w(3) + x1 * w(2)) + shift(x3 * w(1) + x1 * w(0), 2)
    else:
        assert width == 3
        y3 = x3 * w(2) + shift(x3, 1) * w(1) + shift(x3, 2) * w(0)
    hist_ref[...] = cur[C - HIST_ROWS:C]
    return y3.reshape(C + HIST_ROWS, n_cols)[HIST_ROWS:]


def _conv_front_end(raw_qkv, raw_xbc, hcb, exts, conv_w, keep, C):
    exta, extb, extc = exts
    cwa_ref, cwb_ref, cbb_ref, cwc_ref = conv_w
    qkv_act = _silu(_causal_conv(exta, keep(raw_qkv), cwa_ref, CONV_A, C))
    xbc_act = _silu(_causal_conv(extb, keep(raw_xbc), cwb_ref, CONV_B, C) + cbb_ref[...])
    u_c = keep(hcb[:, D_C:2 * D_C] * hcb[:, 0:D_C])
    y_c = hcb[:, 2 * D_C:3 * D_C] * _causal_conv(extc, u_c, cwc_ref, CONV_C, C)
    return qkv_act, xbc_act, y_c


def _proj_kernel(x_ref, nw_ref, w_ref, *out_refs):
    hn = _rmsnorm(x_ref[...], nw_ref[...]).astype(BF16)
    for o_ref, off, width in zip(out_refs, PROJ_OFFSETS, PROJ_WIDTHS):
        o_ref[...] = jnp.dot(hn, w_ref[:, off:off + width], preferred_element_type=F32)


def _proj_call(x, norm_w, w_packed, layer, tm):
    T = x.shape[0]
    const = lambda i: (0, 0)
    return pl.pallas_call(
        _proj_kernel,
        grid=(T // tm,),
        in_specs=[
            pl.BlockSpec((tm, D_MODEL), lambda i: (i, 0)),
            pl.BlockSpec((1, D_MODEL), const),
            pl.BlockSpec((None, D_MODEL, D_PROJ_PACKED), lambda i: (layer, 0, 0), pipeline_mode=pl.Buffered(1)),
        ],
        out_specs=[pl.BlockSpec((tm, w), lambda i: (i, 0)) for w in PROJ_WIDTHS],
        out_shape=[jax.ShapeDtypeStruct((T, w), F32) for w in PROJ_WIDTHS],
        compiler_params=pltpu.CompilerParams(
            dimension_semantics=("parallel",), vmem_limit_bytes=VMEM_LIMIT_BYTES),
        name="proj",
    )(x, norm_w.reshape(1, D_MODEL), w_packed)


def _proj_conv_kernel(x_ref, nw_ref, w_ref, cwa_ref, cwb_ref, cbb_ref, cwc_ref, *rest, tm, pad, head_rows):
    n_ext = (D_QKV_A + D_XBC + D_C) // CONV_COLS
    (qkv_o, za_o, xbc_o, zb_o, yc_o, small_o, ha_o, hb_o, hc_o) = rest[-9 - n_ext:-n_ext]
    exts = rest[-n_ext:]
    j = lax.rem(pl.program_id(0), TILES_PER_PROMPT)
    x = x_ref[...]
    if head_rows:
        head_ref, xpad_o = rest[0], rest[-10 - n_ext]
        x = jnp.where(j == 0, jnp.concatenate([head_ref[...], x[:tm - head_rows]], axis=0), x)
        xpad_o[...] = x

    @pl.when(j == 0)
    def _():
        for ext in exts:
            ext[...] = jnp.zeros((HIST_ROWS, CONV_COLS), F32)

    valid = j * tm + lax.broadcasted_iota(jnp.int32, (tm, 1), 0) >= pad
    hn = _rmsnorm(jnp.where(valid, x, 0.0), nw_ref[...]).astype(BF16)
    mm = lambda off, width: jnp.dot(hn, w_ref[:, off:off + width], preferred_element_type=F32)
    off_qkv, off_za, off_xbc, off_zb, off_hcb, off_small = PROJ_OFFSETS

    def carry(ext, hist_ref, n, col0):
        hist_ref[:, col0:col0 + CONV_COLS] = ext[HIST_ROWS - n:HIST_ROWS, :]

    jobs = []
    for k in range(D_QKV_A // CONV_COLS):
        def tail(raw, k=k, ext=exts[k]):
            col0 = k * CONV_COLS
            qkv_o[:, col0:col0 + CONV_COLS] = _silu(_causal_conv(ext, raw, cwa_ref, CONV_A, tm, col0))
            carry(ext, ha_o, CONV_A - 1, col0)
        jobs.append((functools.partial(mm, off_qkv + k * CONV_COLS, CONV_COLS), tail))
    for k in range(D_XBC // CONV_COLS):
        def tail(raw, k=k, ext=exts[D_QKV_A // CONV_COLS + k]):
            col0 = k * CONV_COLS
            conv = _causal_conv(ext, raw, cwb_ref, CONV_B, tm, col0)
            xbc_o[:, col0:col0 + CONV_COLS] = _silu(conv + cbb_ref[:, col0:col0 + CONV_COLS])
            carry(ext, hb_o, CONV_B - 1, col0)
        jobs.append((functools.partial(mm, off_xbc + k * CONV_COLS, CONV_COLS), tail))

    def tail_c(hcb):
        u_c = hcb[:, D_C:2 * D_C] * hcb[:, 0:D_C]
        for k in range(D_C // CONV_COLS):
            col0, ext = k * CONV_COLS, exts[(D_QKV_A + D_XBC) // CONV_COLS + k]
            conv = _causal_conv(ext, u_c[:, col0:col0 + CONV_COLS], cwc_ref, CONV_C, tm, col0)
            yc_o[:, col0:col0 + CONV_COLS] = hcb[:, 2 * D_C + col0:2 * D_C + col0 + CONV_COLS] * conv
            carry(ext, hc_o, CONV_C - 1, col0)
    jobs.append((functools.partial(mm, off_hcb, D_HCB), tail_c))

    def tail_plain(raws):
        za_o[...], zb_o[...], small_o[...] = raws
    jobs.append((lambda: (mm(off_za, D_A), mm(off_zb, D_B), mm(off_small, D_SMALL)), tail_plain))

    pending = []
    for matmuls, tail in jobs:
        pending.append((matmuls(), tail))
        if len(pending) > JOB_LOOKAHEAD:
            raw, ready_tail = pending.pop(0)
            ready_tail(raw)
    for raw, ready_tail in pending:
        ready_tail(raw)


def _proj_conv_call(x, norm_w, w_packed, layer, conv_w, stacked, out_layer, depth, *, B, L, pad, head=None):
    T = B * L
    tm = L // TILES_PER_PROMPT
    const = lambda i: (0, 0)
    row = lambda i: (i, 0)
    hist = lambda i: (out_layer, i // TILES_PER_PROMPT, 0, 0)
    cwa, cwb, cbb, cwc = conv_w
    if head is None:
        x_spec = pl.BlockSpec((tm, D_MODEL), row)
    else:
        seq, head_rows = x.shape[0] // B, head.shape[0]

        def prompt_window(i):
            start = jnp.maximum(lax.rem(i, TILES_PER_PROMPT) * tm - head_rows, 0)
            return (pl.multiple_of((i // TILES_PER_PROMPT) * seq + start, SUBLANES), 0)
        x_spec = pl.BlockSpec((pl.Element(tm), pl.Element(D_MODEL)), prompt_window)
    in_specs = [
        x_spec,
        pl.BlockSpec((1, D_MODEL), const),
        pl.BlockSpec((None, D_MODEL, D_PROJ_PACKED), lambda i: (layer, 0, 0), pipeline_mode=pl.Buffered(1)),
        pl.BlockSpec((CONV_A, D_QKV_A), const),
        pl.BlockSpec((CONV_B, D_XBC), const),
        pl.BlockSpec((1, D_XBC), const),
        pl.BlockSpec((CONV_C, D_C), const),
    ]
    stacked = list(stacked or [])
    heads = [] if head is None else [head]
    in_specs += [pl.BlockSpec(head.shape, const) for head in heads]
    n_in = len(in_specs)
    in_specs += [pl.BlockSpec(memory_space=pl.ANY) for _ in stacked]
    widths = (D_MODEL,) * len(heads) + ACT_WIDTHS
    n_act = len(widths)
    outs = pl.pallas_call(
        functools.partial(_proj_conv_kernel, tm=tm, pad=pad, head_rows=0 if head is None else head.shape[0]),
        grid=(T // tm,),
        in_specs=in_specs,
        out_specs=([pl.BlockSpec((tm, w), row) for w in widths]
                   + [pl.BlockSpec((None, None, n, w), hist) for n, w in zip(HIST_LENS, HIST_WIDTHS)]),
        out_shape=([jax.ShapeDtypeStruct((T, w), F32) for w in widths]
                   + [jax.ShapeDtypeStruct((depth, B, n, w), F32) for n, w in zip(HIST_LENS, HIST_WIDTHS)]),
        scratch_shapes=[pltpu.VMEM((HIST_ROWS, CONV_COLS), F32)
                        for _ in range(sum(HIST_WIDTHS) // CONV_COLS)],
        input_output_aliases={n_in + i: n_act + i for i in range(len(stacked))},
        compiler_params=pltpu.CompilerParams(
            dimension_semantics=("arbitrary",), vmem_limit_bytes=VMEM_LIMIT_BYTES),
        name="proj_conv",
    )(x, norm_w.reshape(1, D_MODEL), w_packed, cwa, cwb, cbb, cwc, *heads, *stacked)
    return outs[:n_act], outs[n_act:]


def _mix_kernel(*refs, G, C, pad, nC, preact):
    if preact:
        (qkv_ref, za_ref, xbc_ref, zb_ref, small_ref, small_next_ref,
         s0a_ref, s0b_ref, hp_ref, ona_ref, onb_ref) = refs[:11]
        mixed_ref, sa_out, sb_out, sa, sb, gate_s = refs[-6:]
    else:
        (qkv_ref, za_ref, xbc_ref, zb_ref, hcb_ref, small_ref, hista_ref, histb_ref, histc_ref,
         s0a_ref, s0b_ref, cwa_ref, cwb_ref, cbb_ref, cwc_ref, hp_ref, ona_ref, onb_ref) = refs[:18]
        (mixed_ref, yc_ref, sa_out, sb_out, ha_out, hb_out, hc_out, exta, extb, extc, sa, sb) = refs[-12:]
        exts, hist_refs, hist_outs = (exta, extb, extc), (hista_ref, histb_ref, histc_ref), (ha_out, hb_out, hc_out)
    c = pl.program_id(1)

    @pl.when(c == 0)
    def _():
        sa[...] = s0a_ref[...]
        sb[...] = s0b_ref[...]
        if not preact:
            for ext, h_ref in zip(exts, hist_refs):
                ext[...] = h_ref[...]

    row = lax.broadcasted_iota(jnp.int32, (C, C), 0)
    col = lax.broadcasted_iota(jnp.int32, (C, C), 1)
    tril = row >= col
    strict = row > col
    tril_b = jnp.where(tril, 1.0, 0.0).astype(BF16)

    if pad:
        tok = c * C + lax.broadcasted_iota(jnp.int32, (C, 1), 0)
        valid = tok >= pad
        keep = lambda v: jnp.where(valid, v, 0.0)
    else:
        keep = lambda v: v

    seqs = range(G)
    lane_col = lambda a, lane: a[:, lane:lane + 1]

    def gates(small, keep_rows):
        sp_ = keep_rows(_softplus(small + hp_ref[0:1, :]))
        G_ = _cumsum_rows(-jnp.exp(hp_ref[1:2, :]) * sp_, tril_b)
        return sp_, jax.nn.sigmoid(small), G_, G_.T

    if preact:
        @pl.when(c == 0)
        def _():
            for g in seqs:
                for i, a in enumerate(gates(small_ref[g], keep)):
                    gate_s[g, i] = a
        gate_vals = [[gate_s[g, i] for i in range(N_GATE_SLABS)] for g in seqs]
    else:
        gate_vals = [gates(small_ref[g], keep) for g in seqs]
    sp, beta_all, G_all, GT_all = (list(v) for v in zip(*gate_vals))
    eG_last = [jnp.exp(x[C - 1:C, :]) for x in G_all]

    def head_decays(g, lane):
        gb = jnp.broadcast_to(lane_col(G_all[g], lane), (C, D_SMALL))
        dlog = gb[:, :C] - GT_all[g][lane:lane + 1, :]
        dec = jnp.where(tril, jnp.exp(jnp.minimum(dlog, 0.0)), 0.0)
        return jnp.exp(gb), jnp.exp(G_all[g][C - 1:C, lane:lane + 1] - gb), dec

    if preact:
        qkv = [qkv_ref[g] for g in seqs]
        xbc = [xbc_ref[g] for g in seqs]
    else:
        qkv, xbc = [], []
        for g in seqs:
            qkv_act, xbc_act, y_c = _conv_front_end(
                qkv_ref[g], xbc_ref[g], hcb_ref[g], [ext.at[g] for ext in exts],
                (cwa_ref, cwb_ref, cbb_ref, cwc_ref), keep, C)
            qkv.append(qkv_act)
            xbc.append(xbc_act)
            yc_ref[g] = y_c

    chains = [(g, h) for g in seqs for h in range(H_A)]
    qs, ks, vs, kbs, egs, kdts, decs = [], [], [], [], [], [], []
    for g, h in chains:
        q = qkv[g][:, h * DK_A:(h + 1) * DK_A]
        k = qkv[g][:, D_QK_A + h * DK_A:D_QK_A + (h + 1) * DK_A]
        v = qkv[g][:, 2 * D_QK_A + h * DV_A:2 * D_QK_A + (h + 1) * DV_A]
        q = q * lax.rsqrt(jnp.sum(q * q, axis=-1, keepdims=True) + EPS) * (DK_A ** -0.5)
        k = k * lax.rsqrt(jnp.sum(k * k, axis=-1, keepdims=True) + EPS)
        beta = lane_col(beta_all[g], LANE_BETA + h)
        qs.append(q)
        ks.append(k)
        vs.append(v * beta)
        kbs.append(k * beta)
        eg, erem, dec = head_decays(g, LANE_GA + h)
        egs.append(eg)
        decs.append(dec)
        kdts.append((k * erem).T)
    kqs = [_dot_nt(jnp.concatenate([kb, q], axis=0), k) for kb, q, k in zip(kbs, qs, ks)]
    nmats = [jnp.where(strict, kq[:C] * dec, 0.0) for kq, dec in zip(kqs, decs)]
    qks = [kq[C:] * dec for kq, dec in zip(kqs, decs)]
    heads_per_group = H_B // G_B
    gw = D_B // G_B
    grp_of = lambda h: h // heads_per_group
    bgs = {(g, grp): xbc[g][:, D_B + grp * N_B:D_B + (grp + 1) * N_B] for g in seqs for grp in range(G_B)}
    cgs = {(g, grp): xbc[g][:, D_B + (G_B + grp) * N_B:D_B + (G_B + grp + 1) * N_B]
           for g in seqs for grp in range(G_B)}
    heads = [(g, h) for g in seqs for h in range(H_B)]
    x_hs = [xbc[g][:, h * P_B:(h + 1) * P_B] for g, h in heads]
    xdts = [x_h * lane_col(sp[g], LANE_GB + h) for (g, h), x_h in zip(heads, x_hs)]
    xdtts = [xdt.T for xdt in xdts]
    cdecs, bdecs, decs_b = [], [], []
    for g, h in heads:
        eg, erem, dec = head_decays(g, LANE_GB + h)
        cdecs.append(cgs[(g, grp_of(h))] * eg)
        bdecs.append(bgs[(g, grp_of(h))] * erem)
        decs_b.append(dec)
    tinvs = _unit_lower_inverse_offdiag(nmats, row, col, C)
    rhss = [jnp.concatenate([vb, kb * eg], axis=1) for vb, kb, eg in zip(vs, kbs, egs)]
    sols = [rhs + _dot(t, rhs) for t, rhs in zip(tinvs, rhss)]
    if preact:
        for g in seqs:
            for i, a in enumerate(gates(small_next_ref[g], lambda v: v)):
                gate_s[g, i] = a
    s_olds = [sa[g, h] for g, h in chains]
    wss = [_dot(jnp.concatenate([sol[:, DV_A:], q * eg], axis=0), s_old)
           for sol, q, eg, s_old in zip(sols, qs, egs, s_olds)]
    ws_ = [sol[:, :DV_A] - ws[:C] for sol, ws in zip(sols, wss)]
    os_ = [ws[C:] + _dot(qk, w) for ws, qk, w in zip(wss, qks, ws_)]
    for (g, h), kdt, w, s_old, o in zip(chains, kdts, ws_, s_olds, os_):
        sa[g, h] = lane_col(eG_last[g], LANE_GA + h) * s_old + _dot(kdt, w)
        gate = _silu(za_ref[g, :, h * DV_A:(h + 1) * DV_A])
        mixed_ref[g, :, h * DV_A:(h + 1) * DV_A] = _rmsnorm(o, ona_ref[...]) * gate

    cbts = {key: _dot_nt(cgs[key], bgs[key]) for key in bgs}
    h_olds = [sb[g, h] for g, h in heads]
    y_offs = [_dot_nt(cdec, h_old) for cdec, h_old in zip(cdecs, h_olds)]
    for (g, h), xdtt, bdec, h_old in zip(heads, xdtts, bdecs, h_olds):
        sb[g, h] = lane_col(eG_last[g], LANE_GB + h) * h_old + _dot(xdtt, bdec)
    ys = [y_off + _dot(cbts[(g, grp_of(h))] * dec, xdt) + hp_ref[2:3, LANE_GB + h:LANE_GB + h + 1] * x_h
          for (g, h), y_off, dec, xdt, x_h in zip(heads, y_offs, decs_b, xdts, x_hs)]
    for g in seqs:
        for grp in range(G_B):
            lo = g * H_B + grp * heads_per_group
            yg = jnp.concatenate(ys[lo:lo + heads_per_group], axis=1)
            yg = yg * _silu(zb_ref[g, :, grp * gw:(grp + 1) * gw])
            mixed_ref[g, :, D_A + grp * gw:D_A + (grp + 1) * gw] = _rmsnorm(
                yg, onb_ref[:, grp * gw:(grp + 1) * gw])

    @pl.when(c == nC - 1)
    def _():
        sa_out[...] = sa[...]
        sb_out[...] = sb[...]
        if not preact:
            for o_ref, ext, n in zip(hist_outs, exts, HIST_LENS):
                o_ref[...] = ext[:, HIST_ROWS - n:HIST_ROWS, :]


def _mix_call(acts, hists, s0a, s0b, layer, weights, stacked, out_layer, depth, *, B, nC, C, G, pad, preact):
    L = nC * C
    conv_w, hp, ona, onb = weights
    tok = lambda b, c: (b, c, 0)
    per_b3 = lambda b, c: (b, 0, 0)
    out_b4 = lambda b, c: (out_layer, b, 0, 0)
    out_b5 = lambda b, c: (out_layer, b, 0, 0, 0)
    const = lambda b, c: (0, 0)
    state_specs = [
        pl.BlockSpec((None, G, H_A, DK_A, DV_A), lambda b, c: (layer, b, 0, 0, 0)),
        pl.BlockSpec((None, G, H_B, P_B, N_B), lambda b, c: (layer, b, 0, 0, 0)),
    ]
    head_specs = [pl.BlockSpec((8, D_SMALL), const), pl.BlockSpec((1, DV_A), const), pl.BlockSpec((1, D_B), const)]
    acts3 = [a.reshape(B, L, a.shape[-1]) for a in acts]
    in_specs = [pl.BlockSpec((G, C, a.shape[-1]), tok) for a in acts3]
    operands = list(acts3)
    if preact:
        assert C == D_SMALL, "the carried gate slabs assume (C, 128) == (128, C)"
        in_specs.append(pl.BlockSpec((G, C, D_SMALL), lambda b, c: (b, jnp.minimum(c + 1, nC - 1), 0)))
        operands.append(acts3[-1])
    else:
        in_specs += [pl.BlockSpec((G, HIST_ROWS, w), per_b3) for w in HIST_WIDTHS]
        operands += list(hists)
    in_specs += state_specs
    operands += [s0a, s0b]
    if not preact:
        in_specs += [pl.BlockSpec((CONV_A, D_QKV_A), const), pl.BlockSpec((CONV_B, D_XBC), const),
                     pl.BlockSpec((1, D_XBC), const), pl.BlockSpec((CONV_C, D_C), const)]
        operands += list(conv_w)
    in_specs += head_specs
    operands += [hp, ona, onb]

    out_specs = [pl.BlockSpec((G, C, D_AB), tok)]
    out_shape = [jax.ShapeDtypeStruct((B, L, D_AB), F32)]
    if not preact:
        out_specs.append(pl.BlockSpec((G, C, D_C), tok))
        out_shape.append(jax.ShapeDtypeStruct((B, L, D_C), F32))
    n_plain = len(out_specs)
    out_specs += [pl.BlockSpec((None, G, H_A, DK_A, DV_A), out_b5), pl.BlockSpec((None, G, H_B, P_B, N_B), out_b5)]
    out_shape += [jax.ShapeDtypeStruct((depth, B, H_A, DK_A, DV_A), F32),
                  jax.ShapeDtypeStruct((depth, B, H_B, P_B, N_B), F32)]
    scratch = []
    if not preact:
        out_specs += [pl.BlockSpec((None, G, n, w), out_b4) for n, w in zip(HIST_LENS, HIST_WIDTHS)]
        out_shape += [jax.ShapeDtypeStruct((depth, B, n, w), F32) for n, w in zip(HIST_LENS, HIST_WIDTHS)]
        scratch += [pltpu.VMEM((G, HIST_ROWS, w), F32) for w in HIST_WIDTHS]
    scratch += [pltpu.VMEM((G, H_A, DK_A, DV_A), F32), pltpu.VMEM((G, H_B, P_B, N_B), F32)]
    if preact:
        scratch.append(pltpu.VMEM((G, N_GATE_SLABS, C, D_SMALL), F32))

    stacked = list(stacked or [])
    n_in = len(in_specs)
    in_specs += [pl.BlockSpec(memory_space=pl.ANY) for _ in stacked]
    outs = pl.pallas_call(
        functools.partial(_mix_kernel, G=G, C=C, pad=pad, nC=nC, preact=preact),
        grid=(B // G, nC),
        in_specs=in_specs,
        out_specs=out_specs,
        out_shape=out_shape,
        scratch_shapes=scratch,
        input_output_aliases={n_in + i: n_plain + i for i in range(len(stacked))},
        compiler_params=pltpu.CompilerParams(
            dimension_semantics=("parallel", "arbitrary"), vmem_limit_bytes=VMEM_LIMIT_BYTES),
        name="mix",
    )(*operands, *stacked)
    plain = [o.reshape(B * L, o.shape[-1]) for o in outs[:n_plain]]
    return plain, list(outs[n_plain:])


def _route(logits):
    lg = [logits[:, g:g + 1] for g in range(N_EGROUPS)]
    m = functools.reduce(jnp.maximum, lg)
    ex = [jnp.exp(l - m) for l in lg]
    den = functools.reduce(lambda a, b: a + b, ex)
    p = [e / den for e in ex]
    p_top, g_idx = p[0], jnp.zeros_like(p[0], dtype=jnp.int32)
    for g in range(1, N_EGROUPS):
        better = p[g] > p_top
        g_idx = jnp.where(better, g, g_idx)
        p_top = jnp.where(better, p[g], p_top)
    g_hot = [g_idx == g for g in range(N_EGROUPS)]
    le_sel = []
    for j in range(EXP_PER_GROUP):
        acc = None
        for g in range(N_EGROUPS):
            lane = N_EGROUPS + g * EXP_PER_GROUP + j
            t = jnp.where(g_hot[g], logits[:, lane:lane + 1], 0.0)
            acc = t if acc is None else acc + t
        le_sel.append(acc)
    m2 = functools.reduce(jnp.maximum, le_sel)
    ex2 = [jnp.exp(l - m2) for l in le_sel]
    den2 = functools.reduce(lambda a, b: a + b, ex2)
    s = [e / den2 for e in ex2]
    v1, i1 = s[0], jnp.zeros_like(g_idx)
    for j in range(1, EXP_PER_GROUP):
        better = s[j] > v1
        i1 = jnp.where(better, j, i1)
        v1 = jnp.where(better, s[j], v1)
    v2, i2 = jnp.full_like(v1, -1.0), jnp.zeros_like(g_idx)
    for j in range(EXP_PER_GROUP):
        s_rest = jnp.where(i1 == j, -1.0, s[j])
        better = s_rest > v2
        i2 = jnp.where(better, j, i2)
        v2 = jnp.where(better, s_rest, v2)
    tot = v1 + v2
    w1 = v1 / tot * p_top
    w2 = v2 / tot * p_top
    comb = []
    for g in range(N_EGROUPS):
        for j in range(EXP_PER_GROUP):
            inner = jnp.where(i1 == j, w1, jnp.where(i2 == j, w2, 0.0))
            comb.append(jnp.where(g_hot[g], inner, 0.0))
    return comb


def _post_kernel(x_ref, mab_ref, yc_ref, wout_ref, n2_ref, wr_ref, wg_ref, wu_ref, wd_ref, fn_ref,
                 o_ref, *, final):
    mixed = jnp.concatenate([mab_ref[...].astype(BF16), yc_ref[...].astype(BF16)], axis=1)
    x1 = x_ref[...] + jnp.dot(mixed, wout_ref[...], preferred_element_type=F32)
    hb = _rmsnorm(x1, n2_ref[...]).astype(BF16)
    comb = _route(jnp.dot(hb, wr_ref[...], preferred_element_type=F32))

    def activations(grp):
        acts = []
        for e in range(grp * EXPERT_BATCH, (grp + 1) * EXPERT_BATCH):
            hg = jnp.dot(hb, wg_ref[e], preferred_element_type=F32)
            hu = jnp.dot(hb, wu_ref[e], preferred_element_type=F32)
            acts.append((_silu(hg) * hu * comb[e]).astype(BF16))
        return jnp.concatenate(acts, axis=1)

    n_batches = N_EXPERTS // EXPERT_BATCH
    kb = EXPERT_BATCH * D_FF_E
    acts = [activations(0)]
    acc = x1
    for grp in range(n_batches):
        if grp + 1 < n_batches:
            acts.append(activations(grp + 1))
        acc = acc + jnp.dot(acts[grp], wd_ref[grp * kb:(grp + 1) * kb, :], preferred_element_type=F32)
    if final:
        acc = _rmsnorm(acc, fn_ref[...])
    o_ref[...] = acc


def _post_call(x, mixed_ab, y_c, layer, w_out, n2, w_router, w_gate, w_up, w_down, fn, *, tm, final, window=None):
    single = dict(pipeline_mode=pl.Buffered(1))
    const2 = lambda s: (0, 0)
    layer3 = lambda s: (layer, 0, 0)
    layer4 = lambda s: (layer, 0, 0, 0)
    if window is None:
        n_tiles = x.shape[0] // tm
        tok_specs = [pl.BlockSpec((tm, w), lambda s: (s, 0)) for w in (D_MODEL, D_AB, D_C)]
    else:
        B, L, start, length = window
        per_seq = length // tm
        n_tiles = B * per_seq
        window_row = lambda s: (pl.multiple_of((s // per_seq) * L + start + (s % per_seq) * tm, 8), 0)
        tok_specs = [pl.BlockSpec((pl.Element(tm), pl.Element(w)), window_row) for w in (D_MODEL, D_AB, D_C)]
    return pl.pallas_call(
        functools.partial(_post_kernel, final=final),
        grid=(n_tiles,),
        in_specs=tok_specs + [
            pl.BlockSpec((None, D_MIX, D_MODEL), layer3, **single),
            pl.BlockSpec((1, D_MODEL), const2),
            pl.BlockSpec((None, D_MODEL, D_SMALL), layer3, **single),
            pl.BlockSpec((None, N_EXPERTS, D_MODEL, D_FF_E), layer4, **single),
            pl.BlockSpec((None, N_EXPERTS, D_MODEL, D_FF_E), layer4, **single),
            pl.BlockSpec((None, N_EXPERTS * D_FF_E, D_MODEL), layer3, **single),
            pl.BlockSpec((1, D_MODEL), const2),
        ],
        out_specs=pl.BlockSpec((tm, D_MODEL), lambda s: (s, 0)),
        out_shape=jax.ShapeDtypeStruct((n_tiles * tm, D_MODEL), F32),
        compiler_params=pltpu.CompilerParams(
            dimension_semantics=("parallel",), vmem_limit_bytes=VMEM_LIMIT_BYTES),
        name="post",
    )(x, mixed_ab, y_c, w_out, n2, w_router, w_gate, w_up, w_down, fn)


def _pack_w_in_kernel(w_ref, o_ref):
    o = [0]
    for s in (D_QKV_A, D_A, H_A, H_A, D_XBC, D_B, H_B, D_C, D_C, D_C):
        o.append(o[-1] + s)
    pos = 0
    for lo, hi in ((o[0], o[2]), (o[4], o[6]), (o[7], o[10]), (o[2], o[4]), (o[6], o[7])):
        o_ref[:, pos:pos + hi - lo] = w_ref[:, lo:hi]
        pos += hi - lo
    o_ref[:, pos:] = jnp.zeros((o_ref.shape[0], D_PROJ_PACKED - pos), BF16)


def _pack_w_in(w):
    depth, _, d_proj = w.shape
    rows = PACK_ROWS
    return pl.pallas_call(
        _pack_w_in_kernel,
        grid=(depth, D_MODEL // rows),
        in_specs=[pl.BlockSpec((None, rows, d_proj), lambda l, i: (l, i, 0))],
        out_specs=pl.BlockSpec((None, rows, D_PROJ_PACKED), lambda l, i: (l, i, 0)),
        out_shape=jax.ShapeDtypeStruct((depth, D_MODEL, D_PROJ_PACKED), BF16),
        compiler_params=pltpu.CompilerParams(
            dimension_semantics=("parallel", "parallel"), vmem_limit_bytes=VMEM_LIMIT_BYTES),
        name="pack_w_in",
    )(w)


def _head_params(dt_bias_a, dt_bias_b, a_log_a, a_log_b, d_skip_b):
    z4 = jnp.zeros((H_A,), F32)
    tail = jnp.zeros((D_SMALL - 2 * H_A - H_B,), F32)
    rows = [
        jnp.concatenate([z4, dt_bias_a, dt_bias_b, tail]),
        jnp.concatenate([z4, a_log_a, a_log_b, tail]),
        jnp.concatenate([z4, z4, d_skip_b, tail]),
    ]
    return jnp.concatenate([jnp.stack(rows), jnp.zeros((8 - len(rows), D_SMALL), F32)], axis=0)


def _pad_hist(h):
    return jnp.pad(h, ((0, 0), (0, 0), (HIST_ROWS - h.shape[2], 0), (0, 0)))


def kernel(x_prompt, x_sample, state_gdn, state_gdn_conv, state_ssd, state_ssd_conv, state_sconv,
           meta_tokens, norm1_w, w_in, conv_a_w, a_log_a, dt_bias_a, onorm_a_w, conv_b_w, conv_b_b,
           a_log_b, dt_bias_b, d_skip_b, onorm_b_w, conv_c_w, w_out, norm2_w, w_router_g,
           w_router_e, w_gate, w_up, w_down, final_norm_w):
    bp, seq, _ = x_prompt.shape
    bs, dseq, _ = x_sample.shape
    depth = w_in.shape[0]
    pad = CHUNK - N_META
    lp = pad + N_META + seq
    ncp = lp // CHUNK

    head = jnp.concatenate([jnp.zeros((pad, D_MODEL), F32), meta_tokens], axis=0)
    x_p = x_prompt.reshape(bp * seq, D_MODEL)
    x_s = x_sample.reshape(bs * dseq, D_MODEL)

    zero_sa = jnp.zeros((1, bp, H_A, DK_A, DV_A), F32)
    zero_sb = jnp.zeros((1, bp, H_B, P_B, N_B), F32)
    hists_s = [_pad_hist(state_gdn_conv), _pad_hist(state_ssd_conv), _pad_hist(state_sconv)]

    fn = final_norm_w.reshape(1, D_MODEL)
    w_packed = _pack_w_in(w_in.astype(BF16))
    w_router = jnp.concatenate(
        [w_router_g, w_router_e, jnp.zeros((depth, D_MODEL, D_SMALL - N_EGROUPS - N_EXPERTS), F32)],
        axis=-1).astype(BF16)
    w_out_b, w_gate_b, w_up_b = w_out.astype(BF16), w_gate.astype(BF16), w_up.astype(BF16)
    w_down_b = w_down.reshape(depth, N_EXPERTS * D_FF_E, D_MODEL).astype(BF16)
    hist_p = st_p = st_s = None
    for l in range(depth):
        conv_w = (conv_a_w[l], conv_b_w[l], conv_b_b[l].reshape(1, D_XBC), conv_c_w[l])
        mix_w = (conv_w, _head_params(dt_bias_a[l], dt_bias_b[l], a_log_a[l], a_log_b[l], d_skip_b[l]),
                 onorm_a_w[l].reshape(1, DV_A), onorm_b_w[l].reshape(1, D_B))
        post_w = (l, w_out_b, norm2_w[l].reshape(1, D_MODEL), w_router, w_gate_b, w_up_b, w_down_b, fn)
        final = l == depth - 1

        acts_p, hist_p = _proj_conv_call(x_p, norm1_w[l], w_packed, l, conv_w, hist_p, l, depth,
                                         B=bp, L=lp, pad=pad, head=head if l == 0 else None)
        if l == 0:
            x_p, *acts_p = acts_p
        qkv_act, za, xbc_act, zb, yc_p, small = acts_p
        (mixed_p,), st_p = _mix_call((qkv_act, za, xbc_act, zb, small), None, zero_sa, zero_sb, 0, mix_w,
                                     st_p, l, depth, B=bp, nC=ncp, C=CHUNK, G=G_PROMPT, pad=pad, preact=True)
        window = (bp, lp, pad + N_META, seq) if final else None
        x_p = _post_call(x_p, mixed_p, yc_p, *post_w, tm=TM_POST, final=final, window=window)

        proj_s = _proj_call(x_s, norm1_w[l], w_packed, l, tm=TM_PROJ)
        (mixed_s, yc_s), st_s = _mix_call(proj_s, [h[l] for h in hists_s], state_gdn, state_ssd, l, mix_w,
                                          st_s, l, depth, B=bs, nC=1, C=dseq, G=G_SAMPLE, pad=0, preact=False)
        x_s = _post_call(x_s, mixed_s, yc_s, *post_w, tm=TM_POST, final=final)

    y_sample = x_s.reshape(bs, dseq, D_MODEL)
    return (x_p.reshape(bp, seq, D_MODEL), y_sample, st_p[0], hist_p[0], st_p[1], hist_p[1], hist_p[2],
            st_s[0], st_s[2], st_s[1], st_s[3], st_s[4])
```

```python
import functools

import jax
import jax.numpy as jnp
from jax import lax
from jax.experimental import pallas as pl
from jax.experimental.pallas import tpu as pltpu

F32 = jnp.float32
BF16 = jnp.bfloat16

D_MODEL = 1024
N_META = 16
H_A = 4
DK_A = 128
DV_A = 128
D_QK_A = H_A * DK_A
D_A = H_A * DV_A
D_QKV_A = 2 * D_QK_A + D_A
CONV_A = 4
H_B = 8
P_B = 64
D_B = H_B * P_B
G_B = 2
N_B = 128
D_XBC = D_B + 2 * G_B * N_B
CONV_B = 4
D_C = 512
CONV_C = 3
D_AB = D_A + D_B
D_MIX = D_AB + D_C
N_EGROUPS = 4
EXP_PER_GROUP = 4
N_EXPERTS = N_EGROUPS * EXP_PER_GROUP
D_FF_E = 256
EPS = 1e-6
CHUNK = 128

D_HCB = 3 * D_C
D_SMALL = 128
PROJ_WIDTHS = (D_QKV_A, D_A, D_XBC, D_B, D_HCB, D_SMALL)
D_PROJ_PACKED = sum(PROJ_WIDTHS)
PROJ_OFFSETS = tuple(sum(PROJ_WIDTHS[:i]) for i in range(len(PROJ_WIDTHS)))
ACT_WIDTHS = (D_QKV_A, D_A, D_XBC, D_B, D_C, D_SMALL)
HIST_WIDTHS = (D_QKV_A, D_XBC, D_C)
HIST_LENS = (CONV_A - 1, CONV_B - 1, CONV_C - 1)
LANE_BETA = 0
LANE_GA = H_A
LANE_GB = 2 * H_A
N_GATE_SLABS = 4
SUBLANES = 8
HIST_ROWS = SUBLANES
G_PROMPT = 2
G_SAMPLE = 16
EXPERT_BATCH = 4
TM_PROJ = 512
TM_POST = 512
PACK_ROWS = 256
TILES_PER_PROMPT = 4
CONV_COLS = 512
JOB_LOOKAHEAD = 1

VMEM_LIMIT_BYTES = 56 * 1024 * 1024


def _rmsnorm(x, w):
    return x * lax.rsqrt(jnp.mean(x * x, axis=-1, keepdims=True) + EPS) * w


def _silu(x):
    h = 0.5 * x
    return h * jnp.tanh(h) + h


def _softplus(x):
    return jnp.maximum(x, 0.0) + jnp.log1p(jnp.exp(-jnp.abs(x)))


def _dot(a, b):
    return jnp.dot(a.astype(BF16), b.astype(BF16), preferred_element_type=F32)


def _dot_nt(a, b):
    return lax.dot_general(a.astype(BF16), b.astype(BF16), (((1,), (1,)), ((), ())),
                           preferred_element_type=F32)


def _cumsum_rows(g, tril_b):
    g1 = g.astype(BF16)
    r1 = g - g1.astype(F32)
    g2 = r1.astype(BF16)
    g3 = (r1 - g2.astype(F32)).astype(BF16)
    d = functools.partial(jnp.dot, preferred_element_type=F32)
    return d(tril_b, g1) + d(tril_b, g2) + d(tril_b, g3)


def _unit_lower_inverse_offdiag(nmats, row, col, C):
    ds = [-jnp.where((row >> 1) == (col >> 1), n, 0.0) for n in nmats]
    s, ls = 2, 1
    while s < C:
        same_2s = (row >> (ls + 1)) == (col >> (ls + 1))
        same_s = (row >> ls) == (col >> ls)
        es = [jnp.where(same_2s, jnp.where(same_s, 0.0, n), 0.0) for n in nmats]
        ys = [e + _dot(d, e) for d, e in zip(ds, es)]
        ds = [d - (y + _dot(y, d)) for d, y in zip(ds, ys)]
        s, ls = 2 * s, ls + 1
    return ds


def _causal_conv(hist_ref, cur, w_ref, width, C, col0=0):
    n_cols = cur.shape[1]
    x3 = jnp.concatenate([hist_ref[...], cur], axis=0).reshape((C + HIST_ROWS) // SUBLANES, SUBLANES, n_cols)
    sub = lax.broadcasted_iota(jnp.int32, x3.shape, 1)

    def shift(a3, k):
        r = pltpu.roll(a3, k, axis=1)
        from_prev_tile = jnp.concatenate([r[:1], r[:-1]], axis=0)
        return jnp.where(sub < k, from_prev_tile, r)

    w = lambda i: w_ref[i:i + 1, col0:col0 + n_cols]
    if width == 4:
        x1 = shift(x3, 1)
        y3 = (x3 * w(3) + x1 * w(2)) + shift(x3 * w(1) + x1 * w(0), 2)
    else:
        assert width == 3
        y3 = x3 * w(2) + shift(x3, 1) * w(1) + shift(x3, 2) * w(0)
    hist_ref[...] = cur[C - HIST_ROWS:C]
    return y3.reshape(C + HIST_ROWS, n_cols)[HIST_ROWS:]


def _conv_front_end(raw_qkv, raw_xbc, hcb, exts, conv_w, keep, C):
    exta, extb, extc = exts
    cwa_ref, cwb_ref, cbb_ref, cwc_ref = conv_w
    qkv_act = _silu(_causal_conv(exta, keep(raw_qkv), cwa_ref, CONV_A, C))
    xbc_act = _silu(_causal_conv(extb, keep(raw_xbc), cwb_ref, CONV_B, C) + cbb_ref[...])
    u_c = keep(hcb[:, D_C:2 * D_C] * hcb[:, 0:D_C])
    y_c = hcb[:, 2 * D_C:3 * D_C] * _causal_conv(extc, u_c, cwc_ref, CONV_C, C)
    return qkv_act, xbc_act, y_c


def _proj_kernel(x_ref, nw_ref, w_ref, *out_refs):
    hn = _rmsnorm(x_ref[...], nw_ref[...]).astype(BF16)
    for o_ref, off, width in zip(out_refs, PROJ_OFFSETS, PROJ_WIDTHS):
        o_ref[...] = jnp.dot(hn, w_ref[:, off:off + width], preferred_element_type=F32)


def _proj_call(x, norm_w, w_packed, layer, tm):
    T = x.shape[0]
    const = lambda i: (0, 0)
    return pl.pallas_call(
        _proj_kernel,
        grid=(T // tm,),
        in_specs=[
            pl.BlockSpec((tm, D_MODEL), lambda i: (i, 0)),
            pl.BlockSpec((1, D_MODEL), const),
            pl.BlockSpec((None, D_MODEL, D_PROJ_PACKED), lambda i: (layer, 0, 0), pipeline_mode=pl.Buffered(1)),
        ],
        out_specs=[pl.BlockSpec((tm, w), lambda i: (i, 0)) for w in PROJ_WIDTHS],
        out_shape=[jax.ShapeDtypeStruct((T, w), F32) for w in PROJ_WIDTHS],
        compiler_params=pltpu.CompilerParams(
            dimension_semantics=("parallel",), vmem_limit_bytes=VMEM_LIMIT_BYTES),
        name="proj",
    )(x, norm_w.reshape(1, D_MODEL), w_packed)


def _proj_conv_kernel(x_ref, nw_ref, w_ref, cwa_ref, cwb_ref, cbb_ref, cwc_ref, *rest, tm, pad, head_rows):
    n_ext = (D_QKV_A + D_XBC + D_C) // CONV_COLS
    (qkv_o, za_o, xbc_o, zb_o, yc_o, small_o, ha_o, hb_o, hc_o) = rest[-9 - n_ext:-n_ext]
    exts = rest[-n_ext:]
    j = lax.rem(pl.program_id(0), TILES_PER_PROMPT)
    x = x_ref[...]
    if head_rows:
        head_ref, xpad_o = rest[0], rest[-10 - n_ext]
        x = jnp.where(j == 0, jnp.concatenate([head_ref[...], x[:tm - head_rows]], axis=0), x)
        xpad_o[...] = x

    @pl.when(j == 0)
    def _():
        for ext in exts:
            ext[...] = jnp.zeros((HIST_ROWS, CONV_COLS), F32)

    valid = j * tm + lax.broadcasted_iota(jnp.int32, (tm, 1), 0) >= pad
    hn = _rmsnorm(jnp.where(valid, x, 0.0), nw_ref[...]).astype(BF16)
    mm = lambda off, width: jnp.dot(hn, w_ref[:, off:off + width], preferred_element_type=F32)
    off_qkv, off_za, off_xbc, off_zb, off_hcb, off_small = PROJ_OFFSETS

    def carry(ext, hist_ref, n, col0):
        hist_ref[:, col0:col0 + CONV_COLS] = ext[HIST_ROWS - n:HIST_ROWS, :]

    jobs = []
    for k in range(D_QKV_A // CONV_COLS):
        def tail(raw, k=k, ext=exts[k]):
            col0 = k * CONV_COLS
            qkv_o[:, col0:col0 + CONV_COLS] = _silu(_causal_conv(ext, raw, cwa_ref, CONV_A, tm, col0))
            carry(ext, ha_o, CONV_A - 1, col0)
        jobs.append((functools.partial(mm, off_qkv + k * CONV_COLS, CONV_COLS), tail))
    for k in range(D_XBC // CONV_COLS):
        def tail(raw, k=k, ext=exts[D_QKV_A // CONV_COLS + k]):
            col0 = k * CONV_COLS
            conv = _causal_conv(ext, raw, cwb_ref, CONV_B, tm, col0)
            xbc_o[:, col0:col0 + CONV_COLS] = _silu(conv + cbb_ref[:, col0:col0 + CONV_COLS])
            carry(ext, hb_o, CONV_B - 1, col0)
        jobs.append((functools.partial(mm, off_xbc + k * CONV_COLS, CONV_COLS), tail))

    def tail_c(hcb):
        u_c = hcb[:, D_C:2 * D_C] * hcb[:, 0:D_C]
        for k in range(D_C // CONV_COLS):
            col0, ext = k * CONV_COLS, exts[(D_QKV_A + D_XBC) // CONV_COLS + k]
            conv = _causal_conv(ext, u_c[:, col0:col0 + CONV_COLS], cwc_ref, CONV_C, tm, col0)
            y_c = hcb[:, 2 * D_C + col0:2 * D_C + col0 + CONV_COLS] * conv
            yc_o[:, col0:col0 + CONV_COLS] = y_c.astype(yc_o.dtype)
            carry(ext, hc_o, CONV_C - 1, col0)
    jobs.append((functools.partial(mm, off_hcb, D_HCB), tail_c))

    def tail_plain(raws):
        za_o[...], zb_o[...], small_o[...] = raws
    jobs.append((lambda: (mm(off_za, D_A), mm(off_zb, D_B), mm(off_small, D_SMALL)), tail_plain))

    pending = []
    for matmuls, tail in jobs:
        pending.append((matmuls(), tail))
        if len(pending) > JOB_LOOKAHEAD:
            raw, ready_tail = pending.pop(0)
            ready_tail(raw)
    for raw, ready_tail in pending:
        ready_tail(raw)


def _proj_conv_call(x, norm_w, w_packed, layer, conv_w, stacked, out_layer, depth, *, B, L, pad, head=None):
    T = B * L
    tm = L // TILES_PER_PROMPT
    const = lambda i: (0, 0)
    row = lambda i: (i, 0)
    hist = lambda i: (out_layer, i // TILES_PER_PROMPT, 0, 0)
    cwa, cwb, cbb, cwc = conv_w
    if head is None:
        x_spec = pl.BlockSpec((tm, D_MODEL), row)
    else:
        seq, head_rows = x.shape[0] // B, head.shape[0]

        def prompt_window(i):
            start = jnp.maximum(lax.rem(i, TILES_PER_PROMPT) * tm - head_rows, 0)
            return (pl.multiple_of((i // TILES_PER_PROMPT) * seq + start, SUBLANES), 0)
        x_spec = pl.BlockSpec((pl.Element(tm), pl.Element(D_MODEL)), prompt_window)
    in_specs = [
        x_spec,
        pl.BlockSpec((1, D_MODEL), const),
        pl.BlockSpec((None, D_MODEL, D_PROJ_PACKED), lambda i: (layer, 0, 0), pipeline_mode=pl.Buffered(1)),
        pl.BlockSpec((CONV_A, D_QKV_A), const),
        pl.BlockSpec((CONV_B, D_XBC), const),
        pl.BlockSpec((1, D_XBC), const),
        pl.BlockSpec((CONV_C, D_C), const),
    ]
    stacked = list(stacked or [])
    heads = [] if head is None else [head]
    in_specs += [pl.BlockSpec(head.shape, const) for head in heads]
    n_in = len(in_specs)
    in_specs += [pl.BlockSpec(memory_space=pl.ANY) for _ in stacked]
    widths = (D_MODEL,) * len(heads) + ACT_WIDTHS
    n_act = len(widths)
    yc_index = n_act - 2
    outs = pl.pallas_call(
        functools.partial(_proj_conv_kernel, tm=tm, pad=pad, head_rows=0 if head is None else head.shape[0]),
        grid=(T // tm,),
        in_specs=in_specs,
        out_specs=([pl.BlockSpec((tm, w), row) for w in widths]
                   + [pl.BlockSpec((None, None, n, w), hist) for n, w in zip(HIST_LENS, HIST_WIDTHS)]),
        out_shape=([jax.ShapeDtypeStruct((T, w), BF16 if i == yc_index else F32) for i, w in enumerate(widths)]
                   + [jax.ShapeDtypeStruct((depth, B, n, w), F32) for n, w in zip(HIST_LENS, HIST_WIDTHS)]),
        scratch_shapes=[pltpu.VMEM((HIST_ROWS, CONV_COLS), F32)
                        for _ in range(sum(HIST_WIDTHS) // CONV_COLS)],
        input_output_aliases={n_in + i: n_act + i for i in range(len(stacked))},
        compiler_params=pltpu.CompilerParams(
            dimension_semantics=("arbitrary",), vmem_limit_bytes=VMEM_LIMIT_BYTES),
        name="proj_conv",
    )(x, norm_w.reshape(1, D_MODEL), w_packed, cwa, cwb, cbb, cwc, *heads, *stacked)
    return outs[:n_act], outs[n_act:]


def _mix_kernel(*refs, G, C, pad, nC, preact):
    if preact:
        (qkv_ref, za_ref, xbc_ref, zb_ref, small_ref, small_next_ref,
         s0a_ref, s0b_ref, hp_ref, ona_ref, onb_ref) = refs[:11]
        mixed_ref, sa_out, sb_out, sa, sb, gate_s = refs[-6:]
    else:
        (qkv_ref, za_ref, xbc_ref, zb_ref, hcb_ref, small_ref, hista_ref, histb_ref, histc_ref,
         s0a_ref, s0b_ref, cwa_ref, cwb_ref, cbb_ref, cwc_ref, hp_ref, ona_ref, onb_ref) = refs[:18]
        (mixed_ref, yc_ref, sa_out, sb_out, ha_out, hb_out, hc_out, exta, extb, extc, sa, sb) = refs[-12:]
        exts, hist_refs, hist_outs = (exta, extb, extc), (hista_ref, histb_ref, histc_ref), (ha_out, hb_out, hc_out)
    c = pl.program_id(1)

    @pl.when(c == 0)
    def _():
        sa[...] = s0a_ref[...]
        sb[...] = s0b_ref[...]
        if not preact:
            for ext, h_ref in zip(exts, hist_refs):
                ext[...] = h_ref[...]

    row = lax.broadcasted_iota(jnp.int32, (C, C), 0)
    col = lax.broadcasted_iota(jnp.int32, (C, C), 1)
    tril = row >= col
    strict = row > col
    tril_b = jnp.where(tril, 1.0, 0.0).astype(BF16)

    if pad:
        tok = c * C + lax.broadcasted_iota(jnp.int32, (C, 1), 0)
        valid = tok >= pad
        keep = lambda v: jnp.where(valid, v, 0.0)
    else:
        keep = lambda v: v

    seqs = range(G)
    lane_col = lambda a, lane: a[:, lane:lane + 1]

    def gates(small, keep_rows):
        sp_ = keep_rows(_softplus(small + hp_ref[0:1, :]))
        G_ = _cumsum_rows(-jnp.exp(hp_ref[1:2, :]) * sp_, tril_b)
        return sp_, jax.nn.sigmoid(small), G_, G_.T

    if preact:
        @pl.when(c == 0)
        def _():
            for g in seqs:
                for i, a in enumerate(gates(small_ref[g], keep)):
                    gate_s[g, i] = a
        gate_vals = [[gate_s[g, i] for i in range(N_GATE_SLABS)] for g in seqs]
    else:
        gate_vals = [gates(small_ref[g], keep) for g in seqs]
    sp, beta_all, G_all, GT_all = (list(v) for v in zip(*gate_vals))
    eG_last = [jnp.exp(x[C - 1:C, :]) for x in G_all]

    def head_decays(g, lane):
        gb = jnp.broadcast_to(lane_col(G_all[g], lane), (C, D_SMALL))
        dlog = gb[:, :C] - GT_all[g][lane:lane + 1, :]
        dec = jnp.where(tril, jnp.exp(jnp.minimum(dlog, 0.0)), 0.0)
        return jnp.exp(gb), jnp.exp(G_all[g][C - 1:C, lane:lane + 1] - gb), dec

    if preact:
        qkv = [qkv_ref[g] for g in seqs]
        xbc = [xbc_ref[g] for g in seqs]
    else:
        qkv, xbc = [], []
        for g in seqs:
            qkv_act, xbc_act, y_c = _conv_front_end(
                qkv_ref[g], xbc_ref[g], hcb_ref[g], [ext.at[g] for ext in exts],
                (cwa_ref, cwb_ref, cbb_ref, cwc_ref), keep, C)
            qkv.append(qkv_act)
            xbc.append(xbc_act)
            yc_ref[g] = y_c

    chains = [(g, h) for g in seqs for h in range(H_A)]
    qs, ks, vs, kbs, egs, kdts, decs = [], [], [], [], [], [], []
    for g, h in chains:
        q = qkv[g][:, h * DK_A:(h + 1) * DK_A]
        k = qkv[g][:, D_QK_A + h * DK_A:D_QK_A + (h + 1) * DK_A]
        v = qkv[g][:, 2 * D_QK_A + h * DV_A:2 * D_QK_A + (h + 1) * DV_A]
        q = q * lax.rsqrt(jnp.sum(q * q, axis=-1, keepdims=True) + EPS) * (DK_A ** -0.5)
        k = k * lax.rsqrt(jnp.sum(k * k, axis=-1, keepdims=True) + EPS)
        beta = lane_col(beta_all[g], LANE_BETA + h)
        qs.append(q)
        ks.append(k)
        vs.append(v * beta)
        kbs.append(k * beta)
        eg, erem, dec = head_decays(g, LANE_GA + h)
        egs.append(eg)
        decs.append(dec)
        kdts.append((k * erem).T)
    kqs = [_dot_nt(jnp.concatenate([kb, q], axis=0), k) for kb, q, k in zip(kbs, qs, ks)]
    nmats = [jnp.where(strict, kq[:C] * dec, 0.0) for kq, dec in zip(kqs, decs)]
    qks = [kq[C:] * dec for kq, dec in zip(kqs, decs)]
    heads_per_group = H_B // G_B
    gw = D_B // G_B
    grp_of = lambda h: h // heads_per_group
    bgs = {(g, grp): xbc[g][:, D_B + grp * N_B:D_B + (grp + 1) * N_B] for g in seqs for grp in range(G_B)}
    cgs = {(g, grp): xbc[g][:, D_B + (G_B + grp) * N_B:D_B + (G_B + grp + 1) * N_B]
           for g in seqs for grp in range(G_B)}
    heads = [(g, h) for g in seqs for h in range(H_B)]
    x_hs = [xbc[g][:, h * P_B:(h + 1) * P_B] for g, h in heads]
    xdts = [x_h * lane_col(sp[g], LANE_GB + h) for (g, h), x_h in zip(heads, x_hs)]
    xdtts = [xdt.T for xdt in xdts]
    cdecs, bdecs, decs_b = [], [], []
    for g, h in heads:
        eg, erem, dec = head_decays(g, LANE_GB + h)
        cdecs.append(cgs[(g, grp_of(h))] * eg)
        bdecs.append(bgs[(g, grp_of(h))] * erem)
        decs_b.append(dec)
    tinvs = _unit_lower_inverse_offdiag(nmats, row, col, C)
    rhss = [jnp.concatenate([vb, kb * eg], axis=1) for vb, kb, eg in zip(vs, kbs, egs)]
    sols = [rhs + _dot(t, rhs) for t, rhs in zip(tinvs, rhss)]
    if preact:
        for g in seqs:
            for i, a in enumerate(gates(small_next_ref[g], lambda v: v)):
                gate_s[g, i] = a
    s_olds = [sa[g, h] for g, h in chains]
    wss = [_dot(jnp.concatenate([sol[:, DV_A:], q * eg], axis=0), s_old)
           for sol, q, eg, s_old in zip(sols, qs, egs, s_olds)]
    ws_ = [sol[:, :DV_A] - ws[:C] for sol, ws in zip(sols, wss)]
    os_ = [ws[C:] + _dot(qk, w) for ws, qk, w in zip(wss, qks, ws_)]
    for (g, h), kdt, w, s_old, o in zip(chains, kdts, ws_, s_olds, os_):
        sa[g, h] = lane_col(eG_last[g], LANE_GA + h) * s_old + _dot(kdt, w)
        gate = _silu(za_ref[g, :, h * DV_A:(h + 1) * DV_A])
        mixed_ref[g, :, h * DV_A:(h + 1) * DV_A] = (_rmsnorm(o, ona_ref[...]) * gate).astype(mixed_ref.dtype)

    cbts = {key: _dot_nt(cgs[key], bgs[key]) for key in bgs}
    h_olds = [sb[g, h] for g, h in heads]
    y_offs = [_dot_nt(cdec, h_old) for cdec, h_old in zip(cdecs, h_olds)]
    for (g, h), xdtt, bdec, h_old in zip(heads, xdtts, bdecs, h_olds):
        sb[g, h] = lane_col(eG_last[g], LANE_GB + h) * h_old + _dot(xdtt, bdec)
    ys = [y_off + _dot(cbts[(g, grp_of(h))] * dec, xdt) + hp_ref[2:3, LANE_GB + h:LANE_GB + h + 1] * x_h
          for (g, h), y_off, dec, xdt, x_h in zip(heads, y_offs, decs_b, xdts, x_hs)]
    for g in seqs:
        for grp in range(G_B):
            lo = g * H_B + grp * heads_per_group
            yg = jnp.concatenate(ys[lo:lo + heads_per_group], axis=1)
            yg = yg * _silu(zb_ref[g, :, grp * gw:(grp + 1) * gw])
            mixed_ref[g, :, D_A + grp * gw:D_A + (grp + 1) * gw] = _rmsnorm(
                yg, onb_ref[:, grp * gw:(grp + 1) * gw]).astype(mixed_ref.dtype)

    @pl.when(c == nC - 1)
    def _():
        sa_out[...] = sa[...]
        sb_out[...] = sb[...]
        if not preact:
            for o_ref, ext, n in zip(hist_outs, exts, HIST_LENS):
                o_ref[...] = ext[:, HIST_ROWS - n:HIST_ROWS, :]


def _mix_call(acts, hists, s0a, s0b, layer, weights, stacked, out_layer, depth, *, B, nC, C, G, pad, preact):
    L = nC * C
    conv_w, hp, ona, onb = weights
    tok = lambda b, c: (b, c, 0)
    per_b3 = lambda b, c: (b, 0, 0)
    out_b4 = lambda b, c: (out_layer, b, 0, 0)
    out_b5 = lambda b, c: (out_layer, b, 0, 0, 0)
    const = lambda b, c: (0, 0)
    state_specs = [
        pl.BlockSpec((None, G, H_A, DK_A, DV_A), lambda b, c: (layer, b, 0, 0, 0)),
        pl.BlockSpec((None, G, H_B, P_B, N_B), lambda b, c: (layer, b, 0, 0, 0)),
    ]
    head_specs = [pl.BlockSpec((8, D_SMALL), const), pl.BlockSpec((1, DV_A), const), pl.BlockSpec((1, D_B), const)]
    acts3 = [a.reshape(B, L, a.shape[-1]) for a in acts]
    in_specs = [pl.BlockSpec((G, C, a.shape[-1]), tok) for a in acts3]
    operands = list(acts3)
    if preact:
        assert C == D_SMALL, "the carried gate slabs assume (C, 128) == (128, C)"
        in_specs.append(pl.BlockSpec((G, C, D_SMALL), lambda b, c: (b, jnp.minimum(c + 1, nC - 1), 0)))
        operands.append(acts3[-1])
    else:
        in_specs += [pl.BlockSpec((G, HIST_ROWS, w), per_b3) for w in HIST_WIDTHS]
        operands += list(hists)
    in_specs += state_specs
    operands += [s0a, s0b]
    if not preact:
        in_specs += [pl.BlockSpec((CONV_A, D_QKV_A), const), pl.BlockSpec((CONV_B, D_XBC), const),
                     pl.BlockSpec((1, D_XBC), const), pl.BlockSpec((CONV_C, D_C), const)]
        operands += list(conv_w)
    in_specs += head_specs
    operands += [hp, ona, onb]

    out_specs = [pl.BlockSpec((G, C, D_AB), tok)]
    out_shape = [jax.ShapeDtypeStruct((B, L, D_AB), BF16 if C % (2 * SUBLANES) == 0 else F32)]
    if not preact:
        out_specs.append(pl.BlockSpec((G, C, D_C), tok))
        out_shape.append(jax.ShapeDtypeStruct((B, L, D_C), F32))
    n_plain = len(out_specs)
    out_specs += [pl.BlockSpec((None, G, H_A, DK_A, DV_A), out_b5), pl.BlockSpec((None, G, H_B, P_B, N_B), out_b5)]
    out_shape += [jax.ShapeDtypeStruct((depth, B, H_A, DK_A, DV_A), F32),
                  jax.ShapeDtypeStruct((depth, B, H_B, P_B, N_B), F32)]
    scratch = []
    if not preact:
        out_specs += [pl.BlockSpec((None, G, n, w), out_b4) for n, w in zip(HIST_LENS, HIST_WIDTHS)]
        out_shape += [jax.ShapeDtypeStruct((depth, B, n, w), F32) for n, w in zip(HIST_LENS, HIST_WIDTHS)]
        scratch += [pltpu.VMEM((G, HIST_ROWS, w), F32) for w in HIST_WIDTHS]
    scratch += [pltpu.VMEM((G, H_A, DK_A, DV_A), F32), pltpu.VMEM((G, H_B, P_B, N_B), F32)]
    if preact:
        scratch.append(pltpu.VMEM((G, N_GATE_SLABS, C, D_SMALL), F32))

    stacked = list(stacked or [])
    n_in = len(in_specs)
    in_specs += [pl.BlockSpec(memory_space=pl.ANY) for _ in stacked]
    outs = pl.pallas_call(
        functools.partial(_mix_kernel, G=G, C=C, pad=pad, nC=nC, preact=preact),
        grid=(B // G, nC),
        in_specs=in_specs,
        out_specs=out_specs,
        out_shape=out_shape,
        scratch_shapes=scratch,
        input_output_aliases={n_in + i: n_plain + i for i in range(len(stacked))},
        compiler_params=pltpu.CompilerParams(
            dimension_semantics=("parallel", "arbitrary"), vmem_limit_bytes=VMEM_LIMIT_BYTES),
        name="mix",
    )(*operands, *stacked)
    plain = [o.reshape(B * L, o.shape[-1]) for o in outs[:n_plain]]
    return plain, list(outs[n_plain:])


def _route(logits):
    lg = [logits[:, g:g + 1] for g in range(N_EGROUPS)]
    m = functools.reduce(jnp.maximum, lg)
    ex = [jnp.exp(l - m) for l in lg]
    den = functools.reduce(lambda a, b: a + b, ex)
    p = [e / den for e in ex]
    p_top, g_idx = p[0], jnp.zeros_like(p[0], dtype=jnp.int32)
    for g in range(1, N_EGROUPS):
        better = p[g] > p_top
        g_idx = jnp.where(better, g, g_idx)
        p_top = jnp.where(better, p[g], p_top)
    g_hot = [g_idx == g for g in range(N_EGROUPS)]
    le_sel = []
    for j in range(EXP_PER_GROUP):
        acc = None
        for g in range(N_EGROUPS):
            lane = N_EGROUPS + g * EXP_PER_GROUP + j
            t = jnp.where(g_hot[g], logits[:, lane:lane + 1], 0.0)
            acc = t if acc is None else acc + t
        le_sel.append(acc)
    m2 = functools.reduce(jnp.maximum, le_sel)
    ex2 = [jnp.exp(l - m2) for l in le_sel]
    den2 = functools.reduce(lambda a, b: a + b, ex2)
    s = [e / den2 for e in ex2]
    v1, i1 = s[0], jnp.zeros_like(g_idx)
    for j in range(1, EXP_PER_GROUP):
        better = s[j] > v1
        i1 = jnp.where(better, j, i1)
        v1 = jnp.where(better, s[j], v1)
    v2, i2 = jnp.full_like(v1, -1.0), jnp.zeros_like(g_idx)
    for j in range(EXP_PER_GROUP):
        s_rest = jnp.where(i1 == j, -1.0, s[j])
        better = s_rest > v2
        i2 = jnp.where(better, j, i2)
        v2 = jnp.where(better, s_rest, v2)
    tot = v1 + v2
    w1 = v1 / tot * p_top
    w2 = v2 / tot * p_top
    comb = []
    for g in range(N_EGROUPS):
        for j in range(EXP_PER_GROUP):
            inner = jnp.where(i1 == j, w1, jnp.where(i2 == j, w2, 0.0))
            comb.append(jnp.where(g_hot[g], inner, 0.0))
    return comb


def _post_kernel(x_ref, mab_ref, yc_ref, wout_ref, n2_ref, wr_ref, wg_ref, wu_ref, wd_ref, fn_ref,
                 o_ref, *, final):
    mixed = jnp.concatenate([mab_ref[...].astype(BF16), yc_ref[...].astype(BF16)], axis=1)
    x1 = x_ref[...] + jnp.dot(mixed, wout_ref[...], preferred_element_type=F32)
    hb = _rmsnorm(x1, n2_ref[...]).astype(BF16)
    comb = _route(jnp.dot(hb, wr_ref[...], preferred_element_type=F32))

    def activations(grp):
        acts = []
        for e in range(grp * EXPERT_BATCH, (grp + 1) * EXPERT_BATCH):
            hg = jnp.dot(hb, wg_ref[e], preferred_element_type=F32)
            hu = jnp.dot(hb, wu_ref[e], preferred_element_type=F32)
            acts.append((_silu(hg) * hu * comb[e]).astype(BF16))
        return jnp.concatenate(acts, axis=1)

    n_batches = N_EXPERTS // EXPERT_BATCH
    kb = EXPERT_BATCH * D_FF_E
    acts = [activations(0)]
    acc = x1
    for grp in range(n_batches):
        if grp + 1 < n_batches:
            acts.append(activations(grp + 1))
        acc = acc + jnp.dot(acts[grp], wd_ref[grp * kb:(grp + 1) * kb, :], preferred_element_type=F32)
    if final:
        acc = _rmsnorm(acc, fn_ref[...])
    o_ref[...] = acc


def _post_call(x, mixed_ab, y_c, layer, w_out, n2, w_router, w_gate, w_up, w_down, fn, *, tm, final, window=None):
    single = dict(pipeline_mode=pl.Buffered(1))
    const2 = lambda s: (0, 0)
    layer3 = lambda s: (layer, 0, 0)
    layer4 = lambda s: (layer, 0, 0, 0)
    if window is None:
        n_tiles = x.shape[0] // tm
        tok_specs = [pl.BlockSpec((tm, w), lambda s: (s, 0)) for w in (D_MODEL, D_AB, D_C)]
    else:
        B, L, start, length = window
        per_seq = length // tm
        n_tiles = B * per_seq
        window_row = lambda s: (pl.multiple_of((s // per_seq) * L + start + (s % per_seq) * tm, 8), 0)
        tok_specs = [pl.BlockSpec((pl.Element(tm), pl.Element(w)), window_row) for w in (D_MODEL, D_AB, D_C)]
    return pl.pallas_call(
        functools.partial(_post_kernel, final=final),
        grid=(n_tiles,),
        in_specs=tok_specs + [
            pl.BlockSpec((None, D_MIX, D_MODEL), layer3, **single),
            pl.BlockSpec((1, D_MODEL), const2),
            pl.BlockSpec((None, D_MODEL, D_SMALL), layer3, **single),
            pl.BlockSpec((None, N_EXPERTS, D_MODEL, D_FF_E), layer4, **single),
            pl.BlockSpec((None, N_EXPERTS, D_MODEL, D_FF_E), layer4, **single),
            pl.BlockSpec((None, N_EXPERTS * D_FF_E, D_MODEL), layer3, **single),
            pl.BlockSpec((1, D_MODEL), const2),
        ],
        out_specs=pl.BlockSpec((tm, D_MODEL), lambda s: (s, 0)),
        out_shape=jax.ShapeDtypeStruct((n_tiles * tm, D_MODEL), F32),
        compiler_params=pltpu.CompilerParams(
            dimension_semantics=("parallel",), vmem_limit_bytes=VMEM_LIMIT_BYTES),
        name="post",
    )(x, mixed_ab, y_c, w_out, n2, w_router, w_gate, w_up, w_down, fn)


def _pack_w_in_kernel(w_ref, o_ref):
    o = [0]
    for s in (D_QKV_A, D_A, H_A, H_A, D_XBC, D_B, H_B, D_C, D_C, D_C):
        o.append(o[-1] + s)
    pos = 0
    for lo, hi in ((o[0], o[2]), (o[4], o[6]), (o[7], o[10]), (o[2], o[4]), (o[6], o[7])):
        o_ref[:, pos:pos + hi - lo] = w_ref[:, lo:hi]
        pos += hi - lo
    o_ref[:, pos:] = jnp.zeros((o_ref.shape[0], D_PROJ_PACKED - pos), BF16)


def _pack_w_in(w):
    depth, _, d_proj = w.shape
    rows = PACK_ROWS
    return pl.pallas_call(
        _pack_w_in_kernel,
        grid=(depth, D_MODEL // rows),
        in_specs=[pl.BlockSpec((None, rows, d_proj), lambda l, i: (l, i, 0))],
        out_specs=pl.BlockSpec((None, rows, D_PROJ_PACKED), lambda l, i: (l, i, 0)),
        out_shape=jax.ShapeDtypeStruct((depth, D_MODEL, D_PROJ_PACKED), BF16),
        compiler_params=pltpu.CompilerParams(
            dimension_semantics=("parallel", "parallel"), vmem_limit_bytes=VMEM_LIMIT_BYTES),
        name="pack_w_in",
    )(w)


def _head_params(dt_bias_a, dt_bias_b, a_log_a, a_log_b, d_skip_b):
    z4 = jnp.zeros((H_A,), F32)
    tail = jnp.zeros((D_SMALL - 2 * H_A - H_B,), F32)
    rows = [
        jnp.concatenate([z4, dt_bias_a, dt_bias_b, tail]),
        jnp.concatenate([z4, a_log_a, a_log_b, tail]),
        jnp.concatenate([z4, z4, d_skip_b, tail]),
    ]
    return jnp.concatenate([jnp.stack(rows), jnp.zeros((8 - len(rows), D_SMALL), F32)], axis=0)


def _pad_hist(h):
    return jnp.pad(h, ((0, 0), (0, 0), (HIST_ROWS - h.shape[2], 0), (0, 0)))


def kernel(x_prompt, x_sample, state_gdn, state_gdn_conv, state_ssd, state_ssd_conv, state_sconv,
           meta_tokens, norm1_w, w_in, conv_a_w, a_log_a, dt_bias_a, onorm_a_w, conv_b_w, conv_b_b,
           a_log_b, dt_bias_b, d_skip_b, onorm_b_w, conv_c_w, w_out, norm2_w, w_router_g,
           w_router_e, w_gate, w_up, w_down, final_norm_w):
    bp, seq, _ = x_prompt.shape
    bs, dseq, _ = x_sample.shape
    depth = w_in.shape[0]
    pad = CHUNK - N_META
    lp = pad + N_META + seq
    ncp = lp // CHUNK

    head = jnp.concatenate([jnp.zeros((pad, D_MODEL), F32), meta_tokens], axis=0)
    x_p = x_prompt.reshape(bp * seq, D_MODEL)
    x_s = x_sample.reshape(bs * dseq, D_MODEL)

    zero_sa = jnp.zeros((1, bp, H_A, DK_A, DV_A), F32)
    zero_sb = jnp.zeros((1, bp, H_B, P_B, N_B), F32)
    hists_s = [_pad_hist(state_gdn_conv), _pad_hist(state_ssd_conv), _pad_hist(state_sconv)]

    fn = final_norm_w.reshape(1, D_MODEL)
    w_packed = _pack_w_in(w_in.astype(BF16))
    w_router = jnp.concatenate(
        [w_router_g, w_router_e, jnp.zeros((depth, D_MODEL, D_SMALL - N_EGROUPS - N_EXPERTS), F32)],
        axis=-1).astype(BF16)
    w_out_b, w_gate_b, w_up_b = w_out.astype(BF16), w_gate.astype(BF16), w_up.astype(BF16)
    w_down_b = w_down.reshape(depth, N_EXPERTS * D_FF_E, D_MODEL).astype(BF16)
    hist_p = st_p = st_s = None
    for l in range(depth):
        conv_w = (conv_a_w[l], conv_b_w[l], conv_b_b[l].reshape(1, D_XBC), conv_c_w[l])
        mix_w = (conv_w, _head_params(dt_bias_a[l], dt_bias_b[l], a_log_a[l], a_log_b[l], d_skip_b[l]),
                 onorm_a_w[l].reshape(1, DV_A), onorm_b_w[l].reshape(1, D_B))
        post_w = (l, w_out_b, norm2_w[l].reshape(1, D_MODEL), w_router, w_gate_b, w_up_b, w_down_b, fn)
        final = l == depth - 1

        acts_p, hist_p = _proj_conv_call(x_p, norm1_w[l], w_packed, l, conv_w, hist_p, l, depth,
                                         B=bp, L=lp, pad=pad, head=head if l == 0 else None)
        if l == 0:
            x_p, *acts_p = acts_p
        qkv_act, za, xbc_act, zb, yc_p, small = acts_p
        (mixed_p,), st_p = _mix_call((qkv_act, za, xbc_act, zb, small), None, zero_sa, zero_sb, 0, mix_w,
                                     st_p, l, depth, B=bp, nC=ncp, C=CHUNK, G=G_PROMPT, pad=pad, preact=True)
        window = (bp, lp, pad + N_META, seq) if final else None
        x_p = _post_call(x_p, mixed_p, yc_p, *post_w, tm=TM_POST, final=final, window=window)

        proj_s = _proj_call(x_s, norm1_w[l], w_packed, l, tm=TM_PROJ)
        (mixed_s, yc_s), st_s = _mix_call(proj_s, [h[l] for h in hists_s], state_gdn, state_ssd, l, mix_w,
                                          st_s, l, depth, B=bs, nC=1, C=dseq, G=G_SAMPLE, pad=0, preact=False)
        x_s = _post_call(x_s, mixed_s, yc_s, *post_w, tm=TM_POST, final=final)

    y_sample = x_s.reshape(bs, dseq, D_MODEL)
    return (x_p.reshape(bp, seq, D_MODEL), y_sample, st_p[0], hist_p[0], st_p[1], hist_p[1], hist_p[2],
            st_s[0], st_s[2], st_s[1], st_s[3], st_s[4])
```
